```python
import math
import jax
import jax.numpy as jnp
from jax import lax
import numpy as np


D_MODEL = 1024
BATCH = 16
SEQ = 2048
DEPTH = 2

GRID_W = 64
CTX_LEN = 256
RMS_EPS = 1e-6
N_MOD = 6
SHORT_W = 3

HY_WIDTH = 512
HY_EMB = 33
HY_FF = 64
HY_FAST = 0.3
HY_SLOW = 1.5
HY_TARGET = 1e-2
HY_SHIFT = 0.05

ML_HEADS = 4
ML_HEAD_DIM = 128
ML_WIDTH = ML_HEADS * ML_HEAD_DIM
ML_CHUNK = 64
IN0_COLS = 3 * HY_WIDTH + 4 * ML_WIDTH + 4 * ML_HEADS

MLA_HEADS = 8
MLA_NOPE = 128
MLA_ROPE = 64
MLA_V = 128
MLA_Q_RANK = 384
MLA_KV_RANK = 256
MLA_SCALE = (MLA_NOPE + MLA_ROPE) ** -0.5
ROPE_AXIS = MLA_ROPE // 2
ROPE_BASE = 10000.0
Q_BLOCK = 128

D_FF = 2816

kernel_name = 'hybrid_hyena_mlstm_mla_prefix_dit'

F32 = jnp.float32


def rmsnorm(x, g):
    xf = x.astype(F32)
    y = xf * lax.rsqrt(jnp.mean(xf * xf, axis=-1, keepdims=True) + RMS_EPS)
    return (y * g.astype(F32)).astype(x.dtype)


def ada_mod(cv, w, b):
    m = jax.nn.silu(cv) @ w + b
    return jnp.split(m, N_MOD, axis=-1)


def dwconv(x, w):
    k = w.shape[0]
    return lax.conv_general_dilated(x, w[:, None, :].astype(x.dtype), (1,), [(k // 2, k // 2)],
                                    dimension_numbers=('NWC', 'WIO', 'NWC'),
                                    feature_group_count=x.shape[-1])


def hyena_filters(L, fw1, fb1, fw2, fb2, fw3, freq):
    pos = jnp.arange(L, dtype=F32)
    t = (pos / (L - 1))[:, None]
    bands = (HY_EMB - 1) // 2
    f = jnp.linspace(1e-4, bands - 1, bands, dtype=F32)
    ang = (2.0 * math.pi / L) * pos[:, None] * f[None, :]
    z = jnp.concatenate([t, jnp.cos(ang), -jnp.sin(ang)], axis=-1)
    fq = freq.astype(F32)
    h = jnp.sin(fq * (z @ fw1.astype(F32) + fb1.astype(F32)))
    h = jnp.sin(fq * (h @ fw2.astype(F32) + fb2.astype(F32)))
    h = h @ fw3.astype(F32)
    deltas = jnp.linspace(math.log(HY_TARGET) / HY_FAST, math.log(HY_TARGET) / HY_SLOW, HY_WIDTH, dtype=F32)
    decay = jnp.exp(-t * jnp.abs(jnp.tile(deltas, 2))[None, :])
    return h * (decay + HY_SHIFT)


def hyena_mix(u, conv_w, conv_b, filt, bias):
    u = dwconv(u, conv_w) + conv_b
    x0, x1, v = jnp.split(u, 3, axis=-1)
    L = u.shape[1]
    g = (v * x1).astype(F32)
    h = hyena_filters(L, *filt)
    hf, hb = h[:, :HY_WIDTH], h[:, HY_WIDTH:]
    k2 = jnp.concatenate([hf, jnp.zeros((1, HY_WIDTH), F32), hb[:0:-1]], axis=0)
    n = 2 * L
    y = jnp.fft.irfft(jnp.fft.rfft(g, n=n, axis=1) * jnp.fft.rfft(k2, axis=0)[None], n=n, axis=1)[:, :L]
    y = y + g * bias.astype(F32)
    return y.astype(u.dtype) * x0


def mlstm_chunkwise(q, k, v, ig, lf, state):
    B, H, T, d = q.shape
    nc = T // ML_CHUNK

    def chunks(a):
        return jnp.moveaxis(a.reshape((B, H, nc, ML_CHUNK) + a.shape[3:]), 2, 0)

    causal = jnp.tril(jnp.ones((ML_CHUNK, ML_CHUNK), dtype=bool))

    def step(carry, inp):
        C, nv, m = carry
        qc, kc, vc, ic, fc = inp
        b = jnp.cumsum(fc, axis=-1)
        dlog = jnp.where(causal, b[..., :, None] - b[..., None, :] + ic[..., None, :], -jnp.inf)
        inter = b + m[..., None]
        m_t = jnp.maximum(inter, jnp.max(dlog, axis=-1))
        w_in = jnp.exp(dlog - m_t[..., None])
        w_st = jnp.exp(inter - m_t)
        s = jnp.einsum('bhtd,bhsd->bhts', qc, kc) * w_in
        num = jnp.einsum('bhts,bhsd->bhtd', s, vc) + w_st[..., None] * jnp.einsum('bhve,bhte->bhtv', C, qc)
        den = jnp.sum(s, axis=-1) + w_st * jnp.einsum('bhe,bhte->bht', nv, qc)
        h = num / jnp.maximum(jnp.abs(den), jnp.exp(-m_t))[..., None]
        bl = b[..., -1]
        gs = bl[..., None] - b + ic
        m_new = jnp.maximum(bl + m, jnp.max(gs, axis=-1))
        a = jnp.exp(bl + m - m_new)
        ws = jnp.exp(gs - m_new[..., None])
        C_new = a[..., None, None] * C + jnp.einsum('bhs,bhsv,bhse->bhve', ws, vc, kc)
        n_new = a[..., None] * nv + jnp.einsum('bhs,bhse->bhe', ws, kc)
        return (C_new, n_new, m_new), h

    state, hs = lax.scan(step, state, (chunks(q), chunks(k), chunks(v), chunks(ig), chunks(lf)))
    return jnp.moveaxis(hs, 0, 2).reshape(B, H, T, d), state


def mlstm_direction(q, k, v, ig, f_pre, state, reverse):
    lf = jax.nn.log_sigmoid(f_pre)
    if reverse:
        q, k, v = jnp.flip(q, 2), jnp.flip(k, 2), jnp.flip(v, 2)
        ig, lf = jnp.flip(ig, -1), jnp.flip(lf, -1)
    h, state = mlstm_chunkwise(q, k, v, ig, lf, state)
    if reverse:
        h = jnp.flip(h, 2)
    return h, state


def mlstm_inputs(u, conv_w, gate_b):
    B, L, _ = u.shape
    off = 3 * HY_WIDTH
    qk = jax.nn.silu(dwconv(u[..., off:off + 2 * ML_WIDTH], conv_w))
    q, k = jnp.split(qk, 2, axis=-1)
    v = u[..., off + 2 * ML_WIDTH:off + 3 * ML_WIDTH]
    og = u[..., off + 3 * ML_WIDTH:off + 4 * ML_WIDTH]
    gates = u[..., off + 4 * ML_WIDTH:].astype(F32) + gate_b.astype(F32)
    gates = jnp.transpose(gates.reshape(B, L, 4, ML_HEADS), (2, 0, 3, 1))

    def heads(a):
        return jnp.transpose(a.reshape(B, L, ML_HEADS, ML_HEAD_DIM), (0, 2, 1, 3)).astype(F32)

    return heads(q) * ML_HEAD_DIM ** -0.5, heads(k), heads(v), og, gates


def mlstm_out(h, og, norm_g):
    B, H, L, d = h.shape
    t = jnp.swapaxes(h, 1, 2)
    t = t * lax.rsqrt(jnp.mean(t * t, axis=-1, keepdims=True) + RMS_EPS)
    t = t.reshape(B, L, H * d) * norm_g.astype(F32)
    return (t * jax.nn.sigmoid(og.astype(F32))).astype(og.dtype)


def mlstm_mix(u_ctx, u_lat, conv_w, gate_b, norm_g):
    qc, kc, vc, oc, gc = mlstm_inputs(u_ctx, conv_w, gate_b)
    ql, kl, vl, ol, gl = mlstm_inputs(u_lat, conv_w, gate_b)
    B = u_ctx.shape[0]
    zero = (jnp.zeros((B, ML_HEADS, ML_HEAD_DIM, ML_HEAD_DIM), F32),
            jnp.zeros((B, ML_HEADS, ML_HEAD_DIM), F32),
            jnp.zeros((B, ML_HEADS), F32))
    hc_f, st_f = mlstm_direction(qc, kc, vc, gc[0], gc[1], zero, False)
    hl_f, _ = mlstm_direction(ql, kl, vl, gl[0], gl[1], st_f, False)
    hc_b, st_b = mlstm_direction(qc, kc, vc, gc[2], gc[3], zero, True)
    hl_b, _ = mlstm_direction(ql, kl, vl, gl[2], gl[3], st_b, True)
    return mlstm_out(hc_f + hc_b, oc, norm_g), mlstm_out(hl_f + hl_b, ol, norm_g)


def hyena_mlstm_mixer(h_ctx, h_lat, w_in, hy_conv_w, hy_conv_b, hy_filter, hy_bias,
                      ml_conv_w, ml_gate_b, ml_norm_g, w_out):
    u_ctx = h_ctx @ w_in
    u_lat = h_lat @ w_in
    hy_c = hyena_mix(u_ctx[..., :3 * HY_WIDTH], hy_conv_w, hy_conv_b, hy_filter, hy_bias)
    hy_l = hyena_mix(u_lat[..., :3 * HY_WIDTH], hy_conv_w, hy_conv_b, hy_filter, hy_bias)
    ml_c, ml_l = mlstm_mix(u_ctx, u_lat, ml_conv_w, ml_gate_b, ml_norm_g)
    y_ctx = jnp.concatenate([hy_c, ml_c], axis=-1) @ w_out
    y_lat = jnp.concatenate([hy_l, ml_l], axis=-1) @ w_out
    return y_ctx, y_lat


def rope_tables(T):
    n_rows = T // GRID_W
    row = jnp.repeat(jnp.arange(n_rows, dtype=F32), GRID_W)
    col = jnp.tile(jnp.arange(GRID_W, dtype=F32), n_rows)
    inv = ROPE_BASE ** (-jnp.arange(0, ROPE_AXIS, 2, dtype=F32) / ROPE_AXIS)
    ang = jnp.stack([row[:, None] * inv, col[:, None] * inv], axis=1)
    return jnp.cos(ang), jnp.sin(ang)


def apply_rope2d(x, cos, sin):
    xa = x.reshape(x.shape[:-1] + (2, ROPE_AXIS))
    x1, x2 = jnp.split(xa, 2, axis=-1)
    c = cos[None, :, None].astype(x.dtype)
    s = sin[None, :, None].astype(x.dtype)
    return jnp.concatenate([x1 * c - x2 * s, x2 * c + x1 * s], axis=-1).reshape(x.shape)


def mla_down(h, w_down):
    return jnp.split(h @ w_down, [MLA_Q_RANK, MLA_Q_RANK + MLA_KV_RANK], axis=-1)


def mla_q(q_a, q_norm, w_uq, rope):
    B, L, _ = q_a.shape
    q = (rmsnorm(q_a, q_norm) @ w_uq).reshape(B, L, MLA_HEADS, MLA_NOPE + MLA_ROPE)
    if rope is None:
        return q
    q_n, q_r = jnp.split(q, [MLA_NOPE], axis=-1)
    return jnp.concatenate([q_n, apply_rope2d(q_r, *rope)], axis=-1)


def mla_kv(kv_a, k_r, kv_norm, w_ukv, rope):
    B, L, _ = kv_a.shape
    kv = (rmsnorm(kv_a, kv_norm) @ w_ukv).reshape(B, L, MLA_HEADS, MLA_NOPE + MLA_V)
    k_n, v = jnp.split(kv, [MLA_NOPE], axis=-1)
    k_r = k_r[:, :, None, :]
    if rope is not None:
        k_r = apply_rope2d(k_r, *rope)
    k = jnp.concatenate([k_n, jnp.broadcast_to(k_r, (B, L, MLA_HEADS, MLA_ROPE))], axis=-1)
    return k, v


def attend(q, k, v):
    s = jnp.einsum('bqhd,bkhd->bhqk', q, k).astype(F32) * MLA_SCALE
    p = jax.nn.softmax(s, axis=-1).astype(v.dtype)
    return jnp.einsum('bhqk,bkhd->bqhd', p, v)


def mla_mixer(h_ctx, h_lat, ctx_queries, w_down, q_norm, kv_norm, w_uq, w_ukv, w_o):
    B, S, _ = h_lat.shape
    rope = rope_tables(S)
    qa_c, kva_c, kr_c = mla_down(h_ctx, w_down)
    qa_l, kva_l, kr_l = mla_down(h_lat, w_down)
    k_c, v_c = mla_kv(kva_c, kr_c, kv_norm, w_ukv, None)
    k_l, v_l = mla_kv(kva_l, kr_l, kv_norm, w_ukv, rope)
    q_l = mla_q(qa_l, q_norm, w_uq, rope)
    k_all = jnp.concatenate([k_c, k_l], axis=1)
    v_all = jnp.concatenate([v_c, v_l], axis=1)
    nb = S // Q_BLOCK
    qb = jnp.moveaxis(q_l.reshape(B, nb, Q_BLOCK, MLA_HEADS, MLA_NOPE + MLA_ROPE), 1, 0)
    o_l = lax.map(lambda blk: attend(blk, k_all, v_all), qb)
    y_lat = jnp.moveaxis(o_l, 0, 1).reshape(B, S, MLA_HEADS * MLA_V) @ w_o
    y_ctx = None
    if ctx_queries:
        q_c = mla_q(qa_c, q_norm, w_uq, None)
        y_ctx = attend(q_c, k_c, v_c).reshape(B, h_ctx.shape[1], MLA_HEADS * MLA_V) @ w_o
    return y_ctx, y_lat


def conv_ffn(h, w_up, conv_w, conv_b, w_down):
    g, u = jnp.split(h @ w_up, 2, axis=-1)
    g = dwconv(g, conv_w) + conv_b
    return (jax.nn.silu(g) * u) @ w_down


def setup_inputs(seed: int = 0) -> dict:
    key = jax.random.key(seed)
    keys = iter(jax.random.split(key, 64))

    def nrm(shape, scale):
        return jax.random.normal(next(keys), shape, F32) * scale

    def gain(n):
        return 1.0 + nrm((n,), 0.02)

    def ffn():
        return (nrm((D_MODEL, 2 * D_FF), D_MODEL ** -0.5), nrm((SHORT_W, D_FF), SHORT_W ** -0.5),
                nrm((D_FF,), 0.02), nrm((D_FF, D_MODEL), D_FF ** -0.5))

    x = nrm((BATCH, SEQ, D_MODEL), 1.0)
    c = nrm((BATCH, D_MODEL), 1.0)
    ctx = nrm((BATCH, CTX_LEN, D_MODEL), 1.0)
    c_ctx = nrm((D_MODEL,), 1.0)
    forget_b = jnp.linspace(3.0, 6.0, ML_HEADS, dtype=F32)
    ml_gate_b = jnp.concatenate([nrm((ML_HEADS,), 0.1), forget_b + nrm((ML_HEADS,), 0.01),
                                 nrm((ML_HEADS,), 0.1), forget_b + nrm((ML_HEADS,), 0.01)])
    f0 = ffn()
    f1 = ffn()
    mix_w = HY_WIDTH + ML_WIDTH
    return {
        'x': x, 'c': c, 'ctx': ctx, 'c_ctx': c_ctx,
        'ada_w_0': nrm((D_MODEL, N_MOD * D_MODEL), 0.5 * D_MODEL ** -0.5),
        'ada_b_0': nrm((N_MOD * D_MODEL,), 0.02),
        'norm_mix_0': gain(D_MODEL), 'norm_ffn_0': gain(D_MODEL),
        'w_in_0': nrm((D_MODEL, IN0_COLS), D_MODEL ** -0.5),
        'hy_conv_w': nrm((SHORT_W, 3 * HY_WIDTH), SHORT_W ** -0.5),
        'hy_conv_b': nrm((3 * HY_WIDTH,), 0.02),
        'hy_fw1': nrm((HY_EMB, HY_FF), HY_EMB ** -0.5),
        'hy_fb1': nrm((HY_FF,), 0.02),
        'hy_fw2': nrm((HY_FF, HY_FF), HY_FF ** -0.5),
        'hy_fb2': nrm((HY_FF,), 0.02),
        'hy_fw3': nrm((HY_FF, 2 * HY_WIDTH), 0.05 * HY_FF ** -0.5),
        'hy_freq': gain(HY_FF),
        'hy_bias': nrm((HY_WIDTH,), 0.1),
        'ml_conv_w': nrm((SHORT_W, 2 * ML_WIDTH), SHORT_W ** -0.5),
        'ml_gate_b': ml_gate_b,
        'ml_norm_g': gain(ML_WIDTH),
        'w_out_0': nrm((mix_w, D_MODEL), mix_w ** -0.5),
        'ffn_up_0': f0[0], 'ffn_conv_w_0': f0[1], 'ffn_conv_b_0': f0[2], 'ffn_down_0': f0[3],
        'ada_w_1': nrm((D_MODEL, N_MOD * D_MODEL), 0.5 * D_MODEL ** -0.5),
        'ada_b_1': nrm((N_MOD * D_MODEL,), 0.02),
        'norm_mix_1': gain(D_MODEL), 'norm_ffn_1': gain(D_MODEL),
        'mla_w_down': nrm((D_MODEL, MLA_Q_RANK + MLA_KV_RANK + MLA_ROPE), D_MODEL ** -0.5),
        'mla_q_norm': gain(MLA_Q_RANK), 'mla_kv_norm': gain(MLA_KV_RANK),
        'mla_w_uq': nrm((MLA_Q_RANK, MLA_HEADS * (MLA_NOPE + MLA_ROPE)), MLA_Q_RANK ** -0.5),
        'mla_w_ukv': nrm((MLA_KV_RANK, MLA_HEADS * (MLA_NOPE + MLA_V)), MLA_KV_RANK ** -0.5),
        'mla_w_o': nrm((MLA_HEADS * MLA_V, D_MODEL), (MLA_HEADS * MLA_V) ** -0.5),
        'ffn_up_1': f1[0], 'ffn_conv_w_1': f1[1], 'ffn_conv_b_1': f1[2], 'ffn_down_1': f1[3],
        'final_norm': gain(D_MODEL),
    }


def reference(x, c, ctx, c_ctx, ada_w_0, ada_b_0, norm_mix_0, norm_ffn_0, w_in_0, hy_conv_w, hy_conv_b,
              hy_fw1, hy_fb1, hy_fw2, hy_fb2, hy_fw3, hy_freq, hy_bias, ml_conv_w, ml_gate_b, ml_norm_g,
              w_out_0, ffn_up_0, ffn_conv_w_0, ffn_conv_b_0, ffn_down_0, ada_w_1, ada_b_1, norm_mix_1,
              norm_ffn_1, mla_w_down, mla_q_norm, mla_kv_norm, mla_w_uq, mla_w_ukv, mla_w_o, ffn_up_1,
              ffn_conv_w_1, ffn_conv_b_1, ffn_down_1, final_norm):
    ada_w = (ada_w_0, ada_w_1)
    ada_b = (ada_b_0, ada_b_1)
    norm_mix = (norm_mix_0, norm_mix_1)
    norm_ffn = (norm_ffn_0, norm_ffn_1)
    ffn_p = ((ffn_up_0, ffn_conv_w_0, ffn_conv_b_0, ffn_down_0),
             (ffn_up_1, ffn_conv_w_1, ffn_conv_b_1, ffn_down_1))
    hy_filter = (hy_fw1, hy_fb1, hy_fw2, hy_fb2, hy_fw3, hy_freq)
    x_lat, x_ctx = x, ctx
    for layer in range(DEPTH):
        last = layer == DEPTH - 1
        sh1, sc1, g1, sh2, sc2, g2 = ada_mod(c[:, None, :], ada_w[layer], ada_b[layer])
        csh1, csc1, cg1, csh2, csc2, cg2 = ada_mod(c_ctx, ada_w[layer], ada_b[layer])
        h_lat = rmsnorm(x_lat, norm_mix[layer]) * (1 + sc1) + sh1
        h_ctx = rmsnorm(x_ctx, norm_mix[layer]) * (1 + csc1) + csh1
        if layer % 2 == 0:
            y_ctx, y_lat = hyena_mlstm_mixer(h_ctx, h_lat, w_in_0, hy_conv_w, hy_conv_b, hy_filter, hy_bias,
                                             ml_conv_w, ml_gate_b, ml_norm_g, w_out_0)
        else:
            y_ctx, y_lat = mla_mixer(h_ctx, h_lat, not last, mla_w_down, mla_q_norm, mla_kv_norm,
                                     mla_w_uq, mla_w_ukv, mla_w_o)
        x_lat = x_lat + g1 * y_lat
        x_lat = x_lat + g2 * conv_ffn(rmsnorm(x_lat, norm_ffn[layer]) * (1 + sc2) + sh2, *ffn_p[layer])
        if not last:
            x_ctx = x_ctx + cg1 * y_ctx
            x_ctx = x_ctx + cg2 * conv_ffn(rmsnorm(x_ctx, norm_ffn[layer]) * (1 + csc2) + csh2, *ffn_p[layer])
    return rmsnorm(x_lat, final_norm)
```

```python
import functools
import math

import numpy as np
import jax
import jax.numpy as jnp
from jax import lax
from jax.experimental import pallas as pl
from jax.experimental.pallas import tpu as pltpu

F32 = jnp.float32
BF16 = jnp.bfloat16
HI = lax.Precision.HIGHEST

RMS_EPS = 1e-6
GRID_W = 64
SHORT_W = 3
HY_WIDTH = 512
HY_EMB = 33
HY_FAST = 0.3
HY_SLOW = 1.5
HY_TARGET = 1e-2
HY_SHIFT = 0.05
ML_HEADS = 4
ML_HEAD_DIM = 128
ML_WIDTH = ML_HEADS * ML_HEAD_DIM
ML_CHUNK = 64
MLA_HEADS = 8
MLA_NOPE = 128
MLA_ROPE = 64
MLA_V = 128
MLA_Q_RANK = 384
MLA_KV_RANK = 256
MLA_SCALE = (MLA_NOPE + MLA_ROPE) ** -0.5
ROPE_AXIS = MLA_ROPE // 2
ROPE_BASE = 10000.0

TM = 256
LANES = 128
MIB = 1024 * 1024


def _params(sem, vmem_mib):
    return pltpu.CompilerParams(dimension_semantics=sem, vmem_limit_bytes=vmem_mib * MIB)


def _silu(x):
    return x * jax.nn.sigmoid(x)


def _rms(x):
    return x * lax.rsqrt(jnp.mean(x * x, axis=-1, keepdims=True) + RMS_EPS)


def _bdot(a, b):
    return jnp.dot(a, b, preferred_element_type=F32)


def _ada_kernel(cv_ref, w_ref, b_ref, o_ref):
    s = _silu(cv_ref[...])
    o_ref[...] = jnp.dot(s, w_ref[...], preferred_element_type=F32, precision=HI) + b_ref[...]


def _ada(cv, w, b):
    R, D = cv.shape
    N = w.shape[1]
    tn = N // 4
    return pl.pallas_call(
        _ada_kernel,
        grid=(N // tn,),
        in_specs=[pl.BlockSpec((R, D), lambda j: (0, 0)),
                  pl.BlockSpec((D, tn), lambda j: (0, j)),
                  pl.BlockSpec((1, tn), lambda j: (0, j))],
        out_specs=pl.BlockSpec((R, tn), lambda j: (0, j)),
        out_shape=jax.ShapeDtypeStruct((R, N), F32),
        compiler_params=_params(("arbitrary",), 40),
        name="ada_mod",
    )(cv, w, b.reshape(1, N))


def _modulation(c, c_ctx, w, b):
    B, D = c.shape
    R = -(-(B + 1) // 8) * 8
    cv = jnp.concatenate([c, c_ctx[None, :], jnp.zeros((R - B - 1, D), F32)], axis=0)
    m = _ada(cv, w, b)
    lat = m[:B].reshape(B, 1, 6, D)
    cx = jnp.broadcast_to(m[B].reshape(1, 1, 6, D), (B, 1, 6, D))
    return jnp.concatenate([lat, cx], axis=1)


def _modnorm(x, mod, g, si):
    sh = mod[si:si + 1, :]
    sc = mod[si + 1:si + 2, :]
    return (_rms(x) * g) * (1.0 + sc) + sh


def _proj_kernel(n_out, si, has_bias, x_ref, mod_ref, g_ref, *refs):
    w_refs = refs[:n_out]
    nb = 1 if has_bias else 0
    o_refs = refs[n_out + nb:]
    h = _modnorm(x_ref[...], mod_ref[...], g_ref[...], si).astype(BF16)
    for j in range(n_out):
        y = _bdot(h, w_refs[j][...])
        if has_bias and j == n_out - 1:
            y = y + refs[n_out][...]
        o_refs[j][...] = y.astype(o_refs[j].dtype)


def _proj(x, mod, g, ws, si, n_rows, n_lat_tiles, out_dtypes, bias=None, vmem=48):
    B, _, D = x.shape
    nt = n_rows // TM
    in_specs = [pl.BlockSpec((None, TM, D), lambda b, i: (b, i, 0)),
                pl.BlockSpec((None, None, 6, D), lambda b, i: (b, (i >= n_lat_tiles).astype(jnp.int32), 0, 0)),
                pl.BlockSpec((1, D), lambda b, i: (0, 0))]
    args = [x, mod, g.reshape(1, D)]
    for w in ws:
        in_specs.append(pl.BlockSpec(w.shape, lambda b, i: (0, 0)))
        args.append(w)
    if bias is not None:
        in_specs.append(pl.BlockSpec(bias.shape, lambda b, i: (0, 0)))
        args.append(bias)
    out_specs = [pl.BlockSpec((None, TM, w.shape[1]), lambda b, i: (b, i, 0)) for w in ws]
    out_shape = [jax.ShapeDtypeStruct((B, n_rows, w.shape[1]), dt) for w, dt in zip(ws, out_dtypes)]
    return pl.pallas_call(
        functools.partial(_proj_kernel, len(ws), si, bias is not None),
        grid=(B, nt), in_specs=in_specs, out_specs=out_specs, out_shape=out_shape,
        compiler_params=_params(("parallel", "arbitrary"), vmem),
        name="modnorm_proj",
    )(*args)


def _filter_tables(L):
    pos = np.arange(L, dtype=np.float64)
    t = pos / (L - 1)
    bands = (HY_EMB - 1) // 2
    f = np.linspace(1e-4, bands - 1, bands)
    ang = (2.0 * math.pi / L) * pos[:, None] * f[None, :]
    z = np.concatenate([t[:, None], np.cos(ang), -np.sin(ang)], axis=-1)
    zp = np.zeros((L, LANES), np.float64)
    zp[:, :HY_EMB] = z
    return zp.astype(np.float32)


def _filt_mlp_kernel(tl, z_ref, w1, b1, w2, b2, w3, fq, absd, bias_ref, o_ref):
    i = pl.program_id(0)
    z = z_ref[...]
    f = fq[...]
    h = jnp.sin(f * (jnp.dot(z, w1[...], preferred_element_type=F32, precision=HI) + b1[...]))
    h = jnp.sin(f * (jnp.dot(h, w2[...], preferred_element_type=F32, precision=HI) + b2[...]))
    h = jnp.dot(h, w3[...], preferred_element_type=F32, precision=HI)
    t = z[:, 0:1]
    h = h * (jnp.exp(-t * absd[...]) + HY_SHIFT)
    row = lax.broadcasted_iota(jnp.int32, (tl, 1), 0) + i * tl
    col = lax.broadcasted_iota(jnp.int32, (1, 2 * HY_WIDTH), 1)
    h = jnp.where(row == 0, jnp.where(col < HY_WIDTH, h + bias_ref[...], 0.0), h)
    o_ref[...] = h


def _pad2(a, r, c):
    return jnp.zeros((r, c), F32).at[:a.shape[0], :a.shape[1]].set(a.astype(F32))


def _hyena_filter_taps(L, fw1, fb1, fw2, fb2, fw3, freq, hy_bias):
    tl = min(L, 512)
    deltas = np.linspace(math.log(HY_TARGET) / HY_FAST, math.log(HY_TARGET) / HY_SLOW, HY_WIDTH)
    absd = jnp.asarray(np.abs(np.tile(deltas, 2))[None, :].astype(np.float32))
    z = jnp.asarray(_filter_tables(L))
    C2 = 2 * HY_WIDTH
    bias_full = jnp.concatenate([hy_bias.astype(F32), jnp.zeros((HY_WIDTH,), F32)]).reshape(1, C2)
    full = lambda shape: pl.BlockSpec(shape, lambda i: (0, 0))
    return pl.pallas_call(
        functools.partial(_filt_mlp_kernel, tl),
        grid=(L // tl,),
        in_specs=[pl.BlockSpec((tl, LANES), lambda i: (i, 0)),
                  full((LANES, LANES)), full((1, LANES)), full((LANES, LANES)), full((1, LANES)),
                  full((LANES, C2)), full((1, LANES)), full((1, C2)), full((1, C2))],
        out_specs=pl.BlockSpec((tl, C2), lambda i: (i, 0)),
        out_shape=jax.ShapeDtypeStruct((L, C2), F32),
        compiler_params=_params(("arbitrary",), 32),
        name="hyena_filter_mlp",
    )(z, _pad2(fw1, LANES, LANES), _pad2(fb1[None, :], 1, LANES), _pad2(fw2, LANES, LANES),
      _pad2(fb2[None, :], 1, LANES), _pad2(fw3, LANES, C2), _pad2(freq[None, :], 1, LANES), absd, bias_full)


def _dft_tables(L):
    n = 2 * L
    f = np.arange(L, dtype=np.int64)[:, None]
    s = np.arange(L, dtype=np.int64)[None, :]
    ang = (2.0 * math.pi / n) * ((f * s) % n).astype(np.float64)
    wc = np.cos(ang)
    ws = np.sin(ang)
    ws[0, :] = np.where(np.arange(L) % 2 == 0, 1.0, -1.0)
    w = np.concatenate([wc, ws], axis=0).astype(np.float32)
    return w, np.ascontiguousarray(w.T)


def _filt_dft_kernel(L, ft, wc_ref, ws_ref, h_ref, ka_ref, kb_ref, hb_s):
    f = pl.program_id(0)

    @pl.when(f == 0)
    def _():
        hb_s[...] = h_ref[...].astype(BF16)

    hb = hb_s[...]
    a = _bdot(wc_ref[...], hb)
    b = _bdot(ws_ref[...], hb)
    C = HY_WIDTH
    row0 = (lax.broadcasted_iota(jnp.int32, (ft, 1), 0) + f * ft) == 0
    sc = jnp.where(row0, 0.5 / L, 1.0 / L)
    ka_ref[...] = (a[:, :C] + a[:, C:]) * sc
    kb_ref[...] = jnp.where(row0, b[:, :C] + b[:, C:], b[:, :C] - b[:, C:]) * sc


def _hyena_filter_spectrum(L, ft, wf, taps):
    nf = L // ft
    C = HY_WIDTH
    return pl.pallas_call(
        functools.partial(_filt_dft_kernel, L, ft),
        grid=(nf,),
        in_specs=[pl.BlockSpec((ft, L), lambda f: (f, 0)),
                  pl.BlockSpec((ft, L), lambda f: (nf + f, 0)),
                  pl.BlockSpec((L, 2 * C), lambda f: (0, 0))],
        out_specs=[pl.BlockSpec((ft, C), lambda f: (f, 0)), pl.BlockSpec((ft, C), lambda f: (f, 0))],
        out_shape=[jax.ShapeDtypeStruct((L, C), F32), jax.ShapeDtypeStruct((L, C), F32)],
        scratch_shapes=[pltpu.VMEM((L, 2 * C), BF16)],
        compiler_params=_params(("arbitrary",), 48),
        name="hyena_filter_dft",
    )(wf, wf, taps)


def _conv3_rows(x, w, first, last):
    n = x.shape[0]
    xm = jnp.where(first, 0.0, pltpu.roll(x, 1, 0))
    xp = jnp.where(last, 0.0, pltpu.roll(x, n - 1, 0))
    return w[0:1, :] * xm + w[1:2, :] * x + w[2:3, :] * xp


def _hyena_kernel(L, ft, nf, x0_ref, x1_ref, v_ref, cw_ref, cb_ref, wc_ref, ws_ref, wct_ref, wst_ref,
                  ka_ref, kb_ref, *rest):
    o_ref, g_s, acc_s = rest[-3:]
    f = pl.program_id(2)
    row = lax.broadcasted_iota(jnp.int32, (L, 1), 0)
    first = row == 0
    last = row == L - 1

    @pl.when(f == 0)
    def _():
        x1 = _conv3_rows(x1_ref[...], cw_ref[1], first, last) + cb_ref[1:2, :]
        v = _conv3_rows(v_ref[...], cw_ref[2], first, last) + cb_ref[2:3, :]
        g_s[...] = (v * x1).astype(BF16)
        acc_s[...] = jnp.zeros_like(acc_s)

    g = g_s[...]
    a = _bdot(wc_ref[...], g)
    b = _bdot(ws_ref[...], g)
    row0 = (lax.broadcasted_iota(jnp.int32, (ft, 1), 0) + f * ft) == 0
    ka = ka_ref[...]
    kb = kb_ref[...]
    kbm = jnp.where(row0, 0.0, kb)
    kam = jnp.where(row0, kb, ka)
    yr = (a * ka - b * kbm).astype(BF16)
    yb = (a * kbm + b * kam).astype(BF16)
    acc_s[...] += _bdot(wct_ref[...], yr) + _bdot(wst_ref[...], yb)

    @pl.when(f == nf - 1)
    def _():
        x0 = _conv3_rows(x0_ref[...], cw_ref[0], first, last) + cb_ref[0:1, :]
        o_ref[...] = acc_s[...] * x0


def _hyena_mix(x0, x1, v, cw, cb, wf, wft, ka, kb, L, row_block, prev_out):
    B, T, C = x0.shape
    cbw = 256
    ncb = C // cbw
    ft = min(L, 512)
    nf = L // ft
    seg = lambda b, c, f: (b, row_block, c)
    in_specs = [pl.BlockSpec((None, L, cbw), seg)] * 3 + [
        pl.BlockSpec((3, SHORT_W, cbw), lambda b, c, f: (0, 0, c)),
        pl.BlockSpec((3, cbw), lambda b, c, f: (0, c)),
        pl.BlockSpec((ft, L), lambda b, c, f: (f, 0)),
        pl.BlockSpec((ft, L), lambda b, c, f: (nf + f, 0)),
        pl.BlockSpec((L, ft), lambda b, c, f: (0, f)),
        pl.BlockSpec((L, ft), lambda b, c, f: (0, nf + f)),
        pl.BlockSpec((ft, cbw), lambda b, c, f: (f, c)),
        pl.BlockSpec((ft, cbw), lambda b, c, f: (f, c)),
    ]
    args = [x0, x1, v, cw, cb, wf, wf, wft, wft, ka, kb]
    aliases = {}
    if prev_out is not None:
        in_specs.append(pl.BlockSpec(memory_space=pl.ANY))
        args.append(prev_out)
        aliases = {len(args) - 1: 0}
    return pl.pallas_call(
        functools.partial(_hyena_kernel, L, ft, nf),
        grid=(B, ncb, nf),
        in_specs=in_specs,
        out_specs=pl.BlockSpec((None, L, cbw), seg),
        out_shape=jax.ShapeDtypeStruct((B, T, C), F32),
        scratch_shapes=[pltpu.VMEM((L, cbw), BF16), pltpu.VMEM((L, cbw), F32)],
        input_output_aliases=aliases,
        compiler_params=_params(("parallel", "parallel", "arbitrary"), 56),
        name="hyena_mix",
    )(*args)


def _log_sigmoid(x):
    return jnp.minimum(x, 0.0) - jnp.log(1.0 + jnp.exp(-jnp.abs(x)))


def _mlstm_kernel(S, T, q_ref, k_ref, v_ref, og_ref, gcol_ref, grow_ref, cwq_ref, cwk_ref, ng_ref, o_ref,
                  q_s, kt_s, va_s, hf_s, hb_s, cf_s, cb_s):
    d = ML_HEAD_DIM
    lc = ML_CHUNK
    nc = T // lc
    ncl = S // lc
    row = lax.broadcasted_iota(jnp.int32, (T, 1), 0)
    first = (row == 0) | (row == S)
    last = (row == S - 1) | (row == T - 1)
    q_s[...] = (_silu(_conv3_rows(q_ref[...], cwq_ref[...], first, last)) * (d ** -0.5)).astype(BF16)
    kc = _silu(_conv3_rows(k_ref[...], cwk_ref[...], first, last))
    for c in range(nc):
        kt_s[c] = kc[c * lc:(c + 1) * lc, :].T.astype(BF16)
    ones_col = (lax.broadcasted_iota(jnp.int32, (T, d), 1) == 0).astype(F32)
    va_s[...] = jnp.concatenate([v_ref[...], ones_col], axis=1).astype(BF16)
    cf_s[...] = jnp.zeros_like(cf_s)
    cb_s[...] = jnp.zeros_like(cb_s)

    ti = lax.broadcasted_iota(jnp.int32, (lc, lc), 0)
    si = lax.broadcasted_iota(jnp.int32, (lc, lc), 1)
    tril = si <= ti
    triu = si >= ti

    def direction(c, kind, mask, mask_t, c_s, h_s, m_prev):
        r0 = pl.multiple_of(c * lc, lc)
        q = q_s[pl.ds(r0, lc), :]
        kt = kt_s[c]
        va = va_s[pl.ds(r0, lc), :]
        gc = gcol_ref[c]
        gr = grow_ref[c]
        i_col = gc[:, kind:kind + 1]
        lf_col = _log_sigmoid(gc[:, kind + 1:kind + 2])
        i_row = gr[kind:kind + 1, :]
        lf_row = _log_sigmoid(gr[kind + 1:kind + 2, :])
        b_col = jnp.sum(jnp.where(mask, lf_row, 0.0), axis=1, keepdims=True)
        b_row = jnp.sum(jnp.where(mask_t, lf_col, 0.0), axis=0, keepdims=True)
        bl = jnp.sum(lf_row, axis=1, keepdims=True)
        dlog = jnp.where(mask, b_col - b_row + i_row, -jnp.inf)
        inter = b_col + m_prev
        m_t = jnp.maximum(inter, jnp.max(dlog, axis=1, keepdims=True))
        w_in = jnp.exp(dlog - m_t)
        w_st = jnp.exp(inter - m_t)
        s = _bdot(q, kt) * w_in
        ct = c_s[...]
        na = _bdot(s.astype(BF16), va) + w_st * _bdot(q, ct.astype(BF16))
        num = na[:, :d]
        den = na[:, d:d + 1]
        h_s[pl.ds(r0, lc), :] = num / jnp.maximum(jnp.abs(den), jnp.exp(-m_t))
        gs_row = bl - b_row + i_row
        gs_col = bl - b_col + i_col
        m_new = jnp.maximum(bl + m_prev, jnp.max(gs_row, axis=1, keepdims=True))
        a = jnp.exp(bl + m_prev - m_new)
        ws = jnp.exp(gs_col - m_new)
        c_s[...] = a * ct + _bdot(kt, (ws * va.astype(F32)).astype(BF16))
        return m_new

    def step(i, carry):
        m_f, m_b = carry
        c_f = lax.rem(i + ncl, nc)
        c_b = nc - 1 - i
        m_f = direction(c_f, 0, tril, triu, cf_s, hf_s, m_f)
        m_b = direction(c_b, 2, triu, tril, cb_s, hb_s, m_b)
        return m_f, m_b

    zero = jnp.zeros((1, 1), F32)
    lax.fori_loop(0, nc, step, (zero, zero))

    h = hf_s[...] + hb_s[...]
    o_ref[...] = (_rms(h) * ng_ref[...]) * jax.nn.sigmoid(og_ref[...])


def _mlstm_mix(q, k, v, og, gcol, grow, cwq, cwk, ng, S):
    B, T, W = q.shape
    H = W // ML_HEAD_DIM
    d = ML_HEAD_DIM
    nc = T // ML_CHUNK
    head = pl.BlockSpec((None, T, d), lambda b, h: (b, 0, h))
    return pl.pallas_call(
        functools.partial(_mlstm_kernel, S, T),
        grid=(B, H),
        in_specs=[head, head, head, head,
                  pl.BlockSpec((None, None, nc, ML_CHUNK, 4), lambda b, h: (b, h, 0, 0, 0)),
                  pl.BlockSpec((None, None, nc, 4, ML_CHUNK), lambda b, h: (b, h, 0, 0, 0)),
                  pl.BlockSpec((SHORT_W, d), lambda b, h: (0, h)),
                  pl.BlockSpec((SHORT_W, d), lambda b, h: (0, h)),
                  pl.BlockSpec((1, d), lambda b, h: (0, h))],
        out_specs=head,
        out_shape=jax.ShapeDtypeStruct((B, T, W), F32),
        scratch_shapes=[pltpu.VMEM((T, d), BF16), pltpu.VMEM((nc, d, ML_CHUNK), BF16),
                        pltpu.VMEM((T, 2 * d), BF16), pltpu.VMEM((T, d), F32), pltpu.VMEM((T, d), F32),
                        pltpu.VMEM((d, 2 * d), F32), pltpu.VMEM((d, 2 * d), F32)],
        compiler_params=_params(("parallel", "parallel"), 48),
        name="mlstm_mix",
    )(q, k, v, og, gcol, grow, cwq, cwk, ng)


def _outproj_kernel(n_in, x_ref, mod_ref, *refs):
    a_refs = refs[:n_in]
    w_refs = refs[n_in:2 * n_in]
    o_ref = refs[2 * n_in]
    y = _bdot(a_refs[0][...].astype(BF16), w_refs[0][...])
    for j in range(1, n_in):
        y = y + _bdot(a_refs[j][...].astype(BF16), w_refs[j][...])
    o_ref[...] = x_ref[...] + mod_ref[2:3, :] * y


def _outproj(x, mod, acts, ws, n_rows, n_lat_tiles):
    B, _, D = x.shape
    nt = n_rows // TM
    tile = lambda w: pl.BlockSpec((None, TM, w), lambda b, i: (b, i, 0))
    in_specs = [tile(D), pl.BlockSpec((None, None, 6, D), lambda b, i: (b, (i >= n_lat_tiles).astype(jnp.int32), 0, 0))]
    in_specs += [tile(a.shape[2]) for a in acts]
    in_specs += [pl.BlockSpec(w.shape, lambda b, i: (0, 0)) for w in ws]
    return pl.pallas_call(
        functools.partial(_outproj_kernel, len(acts)),
        grid=(B, nt), in_specs=in_specs, out_specs=tile(D),
        out_shape=jax.ShapeDtypeStruct((B, n_rows, D), F32),
        compiler_params=_params(("parallel", "arbitrary"), 40),
        name="out_proj",
    )(x, mod, *acts, *ws)


def _ffn_down_kernel(final, n_lat_tiles, x_ref, mod_ref, g_ref, gp_ref, gn_ref, u_ref, cw_ref, cb_ref,
                     wd_ref, fn_ref, o_ref):
    i = pl.program_id(1)
    nt = pl.num_programs(1)
    g = g_ref[...]
    row = lax.broadcasted_iota(jnp.int32, (TM, 1), 0)
    has_prev = jnp.logical_and(i != 0, i != n_lat_tiles)
    has_next = jnp.logical_and(i != n_lat_tiles - 1, i != nt - 1)
    gp = jnp.where(has_prev, gp_ref[7:8, :], 0.0)
    gn = jnp.where(has_next, gn_ref[0:1, :], 0.0)
    gm = jnp.where(row == 0, gp, pltpu.roll(g, 1, 0))
    gq = jnp.where(row == TM - 1, gn, pltpu.roll(g, TM - 1, 0))
    cw = cw_ref[...]
    gc = cw[0:1, :] * gm + cw[1:2, :] * g + cw[2:3, :] * gq + cb_ref[...]
    a = (_silu(gc) * u_ref[...]).astype(BF16)
    o = x_ref[...] + mod_ref[5:6, :] * _bdot(a, wd_ref[...])
    if final:
        o = _rms(o) * fn_ref[...]
    o_ref[...] = o


def _ffn_down(x, mod, g, u, cw, cb, wd, fn, n_lat_tiles, final):
    B, R, Fh = g.shape
    D = x.shape[2]
    nt = R // TM
    hb = TM // 8
    nhb = R // 8
    in_specs = [pl.BlockSpec((None, TM, D), lambda b, i: (b, i, 0)),
                pl.BlockSpec((None, None, 6, D), lambda b, i: (b, (i >= n_lat_tiles).astype(jnp.int32), 0, 0)),
                pl.BlockSpec((None, TM, Fh), lambda b, i: (b, i, 0)),
                pl.BlockSpec((None, 8, Fh), lambda b, i: (b, jnp.maximum(i * hb - 1, 0), 0)),
                pl.BlockSpec((None, 8, Fh), lambda b, i: (b, jnp.minimum((i + 1) * hb, nhb - 1), 0)),
                pl.BlockSpec((None, TM, Fh), lambda b, i: (b, i, 0)),
                pl.BlockSpec((SHORT_W, Fh), lambda b, i: (0, 0)),
                pl.BlockSpec((1, Fh), lambda b, i: (0, 0)),
                pl.BlockSpec((Fh, D), lambda b, i: (0, 0)),
                pl.BlockSpec((1, D), lambda b, i: (0, 0))]
    return pl.pallas_call(
        functools.partial(_ffn_down_kernel, final, n_lat_tiles),
        grid=(B, nt), in_specs=in_specs,
        out_specs=pl.BlockSpec((None, TM, D), lambda b, i: (b, i, 0)),
        out_shape=jax.ShapeDtypeStruct((B, R, D), F32),
        compiler_params=_params(("parallel", "arbitrary"), 48),
        name="ffn_down",
    )(x, mod, g, g, g, u, cw, cb.reshape(1, Fh), wd, fn.reshape(1, D))


def _conv_ffn(x, mod, norm_g, w_up, cw, cb, w_down, fn, n_rows, n_lat_tiles, final):
    Fh = w_up.shape[1] // 2
    wg = w_up[:, :Fh].astype(BF16)
    wu = w_up[:, Fh:].astype(BF16)
    g, u = _proj(x, mod, norm_g, [wg, wu], 3, n_rows, n_lat_tiles, [F32, F32], vmem=56)
    return _ffn_down(x, mod, g, u, cw, cb, w_down.astype(BF16), fn, n_lat_tiles, final)


def _rope_tables(S, T):
    p = np.arange(S)
    inv = ROPE_BASE ** (-np.arange(0, ROPE_AXIS, 2, dtype=np.float64) / ROPE_AXIS)
    ar = (p // GRID_W)[:, None] * inv[None, :]
    ac = (p % GRID_W)[:, None] * inv[None, :]
    cos = np.concatenate([np.cos(ar), np.cos(ar), np.cos(ac), np.cos(ac)], axis=1)
    sin = np.concatenate([-np.sin(ar), np.sin(ar), -np.sin(ac), np.sin(ac)], axis=1)
    cos = np.concatenate([cos, np.ones((T - S, MLA_ROPE))], axis=0)
    sin = np.concatenate([sin, np.zeros((T - S, MLA_ROPE))], axis=0)
    return cos.astype(np.float32), sin.astype(np.float32)


def _mla_proj_kernel(x_ref, mod_ref, g_ref, wdn_ref, qn_ref, kvn_ref, wuq_ref, wukv_ref, cos_ref, sin_ref,
                     qo_ref, ko_ref, vo_ref):
    H, dn, dr = MLA_HEADS, MLA_NOPE, MLA_ROPE
    h = _modnorm(x_ref[...], mod_ref[...], g_ref[...], 0).astype(BF16)
    dnp = _bdot(h, wdn_ref[...])
    qa = dnp[:, :MLA_Q_RANK]
    kva = dnp[:, MLA_Q_RANK:MLA_Q_RANK + MLA_KV_RANK]
    kr2 = dnp[:, MLA_Q_RANK + MLA_KV_RANK:]
    q = _bdot((_rms(qa) * qn_ref[...]).astype(BF16), wuq_ref[...])
    kv = _bdot((_rms(kva) * kvn_ref[...]).astype(BF16), wukv_ref[...])
    cosr = cos_ref[...]
    sinr = sin_ref[...]
    kr = kr2[:, :dr] * cosr + kr2[:, dr:] * sinr
    ro = H * dn
    for hd in range(H):
        qr = q[:, ro + hd * dr:ro + (hd + 1) * dr] * cosr + q[:, ro + H * dr + hd * dr:ro + H * dr + (hd + 1) * dr] * sinr
        qh = jnp.concatenate([q[:, hd * dn:(hd + 1) * dn], qr], axis=1) * MLA_SCALE
        qo_ref[hd] = qh.astype(BF16)
        ko_ref[hd] = jnp.concatenate([kv[:, hd * dn:(hd + 1) * dn], kr], axis=1).astype(BF16)
        vo_ref[hd] = kv[:, ro + hd * MLA_V:ro + (hd + 1) * MLA_V].astype(BF16)


def _mla_proj(x, mod, norm_g, w_down, q_norm, kv_norm, w_uq, w_ukv, S):
    B, T, D = x.shape
    H, dn, dr, dv = MLA_HEADS, MLA_NOPE, MLA_ROPE, MLA_V
    dk = dn + dr
    nt = T // TM
    n_lat_tiles = S // TM
    swap = np.arange(dr) ^ (ROPE_AXIS // 2)
    wkr = w_down[:, MLA_Q_RANK + MLA_KV_RANK:]
    wdn = jnp.concatenate([w_down, wkr[:, swap]], axis=1).astype(BF16)
    wq = w_uq.reshape(MLA_Q_RANK, H, dk)
    wq_n = wq[:, :, :dn].reshape(MLA_Q_RANK, H * dn)
    wq_r = wq[:, :, dn:]
    wuq = jnp.concatenate([wq_n, wq_r.reshape(MLA_Q_RANK, H * dr), wq_r[:, :, swap].reshape(MLA_Q_RANK, H * dr)],
                          axis=1).astype(BF16)
    wkv = w_ukv.reshape(MLA_KV_RANK, H, dn + dv)
    wukv = jnp.concatenate([wkv[:, :, :dn].reshape(MLA_KV_RANK, H * dn), wkv[:, :, dn:].reshape(MLA_KV_RANK, H * dv)],
                           axis=1).astype(BF16)
    cos, sin = _rope_tables(S, T)
    full = lambda a: pl.BlockSpec(a.shape, lambda b, i: (0,) * a.ndim)
    in_specs = [pl.BlockSpec((None, TM, D), lambda b, i: (b, i, 0)),
                pl.BlockSpec((None, None, 6, D), lambda b, i: (b, (i >= n_lat_tiles).astype(jnp.int32), 0, 0)),
                pl.BlockSpec((1, D), lambda b, i: (0, 0)),
                full(wdn), pl.BlockSpec((1, MLA_Q_RANK), lambda b, i: (0, 0)),
                pl.BlockSpec((1, MLA_KV_RANK), lambda b, i: (0, 0)), full(wuq), full(wukv),
                pl.BlockSpec((TM, dr), lambda b, i: (i, 0)), pl.BlockSpec((TM, dr), lambda b, i: (i, 0))]
    headed = lambda w: pl.BlockSpec((None, H, TM, w), lambda b, i: (b, 0, i, 0))
    return pl.pallas_call(
        _mla_proj_kernel,
        grid=(B, nt), in_specs=in_specs,
        out_specs=[headed(dk), headed(dk), headed(dv)],
        out_shape=[jax.ShapeDtypeStruct((B, H, T, dk), BF16), jax.ShapeDtypeStruct((B, H, T, dk), BF16),
                   jax.ShapeDtypeStruct((B, H, T, dv), BF16)],
        compiler_params=_params(("parallel", "arbitrary"), 40),
        name="mla_proj",
    )(x, mod, norm_g.reshape(1, D), wdn, q_norm.reshape(1, -1), kv_norm.reshape(1, -1), wuq, wukv,
      jnp.asarray(cos), jnp.asarray(sin))


def _attn_kernel(q_ref, k_ref, v_ref, o_ref):
    s = lax.dot_general(q_ref[...], k_ref[...], (((1,), (1,)), ((), ())), preferred_element_type=F32)
    p = jnp.exp(s - jnp.max(s, axis=-1, keepdims=True))
    l = jnp.sum(p, axis=-1, keepdims=True)
    o_ref[...] = (_bdot(p.astype(BF16), v_ref[...]) / l).astype(o_ref.dtype)


def _attention(q, k, v, S):
    B, H, T, dk = k.shape
    dv = v.shape[3]
    tq = min(S, 512)
    return pl.pallas_call(
        _attn_kernel,
        grid=(B, H, S // tq),
        in_specs=[pl.BlockSpec((None, None, tq, dk), lambda b, h, i: (b, h, i, 0)),
                  pl.BlockSpec((None, None, T, dk), lambda b, h, i: (b, h, 0, 0)),
                  pl.BlockSpec((None, None, T, dv), lambda b, h, i: (b, h, 0, 0))],
        out_specs=pl.BlockSpec((None, tq, dv), lambda b, h, i: (b, i, h)),
        out_shape=jax.ShapeDtypeStruct((B, S, H * dv), BF16),
        compiler_params=_params(("parallel", "parallel", "arbitrary"), 48),
        name="mla_attention",
    )(q, k, v)


def kernel(x, c, ctx, c_ctx, ada_w_0, ada_b_0, norm_mix_0, norm_ffn_0, w_in_0, hy_conv_w, hy_conv_b, hy_fw1, hy_fb1, hy_fw2, hy_fb2, hy_fw3, hy_freq, hy_bias, ml_conv_w, ml_gate_b, ml_norm_g, w_out_0, ffn_up_0, ffn_conv_w_0, ffn_conv_b_0, ffn_down_0, ada_w_1, ada_b_1, norm_mix_1, norm_ffn_1, mla_w_down, mla_q_norm, mla_kv_norm, mla_w_uq, mla_w_ukv, mla_w_o, ffn_up_1, ffn_conv_w_1, ffn_conv_b_1, ffn_down_1, final_norm):
    B, S, D = x.shape
    CL = ctx.shape[1]
    T = S + CL
    assert S % TM == 0 and CL % TM == 0 and S % CL == 0 and S % GRID_W == 0
    nlt = S // TM
    C = HY_WIDTH
    W = ML_WIDTH
    H = ML_HEADS

    xs = jnp.concatenate([x, ctx], axis=1)
    mod0 = _modulation(c, c_ctx, ada_w_0, ada_b_0)
    mod1 = _modulation(c, c_ctx, ada_w_1, ada_b_1)

    o = 3 * C
    wi = w_in_0.astype(BF16)
    w_gate = jnp.zeros((D, LANES), BF16).at[:, :4 * H].set(wi[:, o + 4 * W:])
    b_gate = jnp.zeros((1, LANES), F32).at[0, :4 * H].set(ml_gate_b.astype(F32))
    ws0 = [wi[:, 0:C], wi[:, C:2 * C], wi[:, 2 * C:3 * C], wi[:, o:o + W], wi[:, o + W:o + 2 * W],
           wi[:, o + 2 * W:o + 3 * W], wi[:, o + 3 * W:o + 4 * W], w_gate]
    hx0, hx1, hv, mq, mk, mv, mog, gates = _proj(xs, mod0, norm_mix_0, ws0, 0, T, nlt, [F32] * 8, bias=b_gate)

    filt = (hy_fw1, hy_fb1, hy_fw2, hy_fb2, hy_fw3, hy_freq, hy_bias)
    hcw = jnp.transpose(hy_conv_w.reshape(SHORT_W, 3, C), (1, 0, 2)).astype(F32)
    hcb = hy_conv_b.reshape(3, C).astype(F32)
    hy = None
    for L, blk in ((S, 0), (CL, S // CL)):
        wf_np, wft_np = _dft_tables(L)
        wf = jnp.asarray(wf_np).astype(BF16)
        wft = jnp.asarray(wft_np).astype(BF16)
        taps = _hyena_filter_taps(L, *filt)
        ka, kb = _hyena_filter_spectrum(L, min(L, 512), wf, taps)
        hy = _hyena_mix(hx0, hx1, hv, hcw, hcb, wf, wft, ka, kb, L, blk, hy)

    nc = T // ML_CHUNK
    g4 = gates[:, :, :4 * H].reshape(B, nc, ML_CHUNK, 4, H)
    gcol = jnp.transpose(g4, (0, 4, 1, 2, 3))
    grow = jnp.transpose(g4, (0, 4, 1, 3, 2))
    mcw = ml_conv_w.astype(F32)
    ml = _mlstm_mix(mq, mk, mv, mog, gcol, grow, mcw[:, :W], mcw[:, W:], ml_norm_g.reshape(1, W).astype(F32), S)

    wo = w_out_0.astype(BF16)
    xs = _outproj(xs, mod0, [hy, ml], [wo[:C], wo[C:]], T, nlt)
    xs = _conv_ffn(xs, mod0, norm_ffn_0, ffn_up_0, ffn_conv_w_0, ffn_conv_b_0, ffn_down_0, final_norm, T, nlt, False)

    q, k, v = _mla_proj(xs, mod1, norm_mix_1, mla_w_down, mla_q_norm, mla_kv_norm, mla_w_uq, mla_w_ukv, S)
    att = _attention(q, k, v, S)
    xl = _outproj(xs, mod1, [att], [mla_w_o.astype(BF16)], S, nlt)
    return _conv_ffn(xl, mod1, norm_ffn_1, ffn_up_1, ffn_conv_w_1, ffn_conv_b_1, ffn_down_1, final_norm, S, nlt, True)
```

```python
import functools
import math

import numpy as np
import jax
import jax.numpy as jnp
from jax import lax
from jax.experimental import pallas as pl
from jax.experimental.pallas import tpu as pltpu

F32 = jnp.float32
BF16 = jnp.bfloat16
HI = lax.Precision.HIGHEST

RMS_EPS = 1e-6
GRID_W = 64
SHORT_W = 3
HY_WIDTH = 512
HY_EMB = 33
HY_FAST = 0.3
HY_SLOW = 1.5
HY_TARGET = 1e-2
HY_SHIFT = 0.05
ML_HEADS = 4
ML_HEAD_DIM = 128
ML_WIDTH = ML_HEADS * ML_HEAD_DIM
ML_BLOCK = 256
MLA_HEADS = 8
MLA_NOPE = 128
MLA_ROPE = 64
MLA_V = 128
MLA_Q_RANK = 384
MLA_KV_RANK = 256
MLA_SCALE = (MLA_NOPE + MLA_ROPE) ** -0.5
ROPE_AXIS = MLA_ROPE // 2
ROPE_BASE = 10000.0

TM = 256
LANES = 128
MIB = 1024 * 1024


def _params(sem, vmem_mib):
    return pltpu.CompilerParams(dimension_semantics=sem, vmem_limit_bytes=vmem_mib * MIB)


def _sigmoid(x):
    return 0.5 * jnp.tanh(0.5 * x) + 0.5


def _silu(x):
    return x * _sigmoid(x)


def _log_sigmoid(x):
    return jnp.minimum(x, 0.0) - jnp.log(1.0 + jnp.exp(-jnp.abs(x)))


def _rms(x):
    return x * lax.rsqrt(jnp.mean(x * x, axis=-1, keepdims=True) + RMS_EPS)


def _bdot(a, b):
    return jnp.dot(a, b, preferred_element_type=F32)


def _ada_kernel(cv_ref, w_ref, b_ref, o_ref):
    s = _silu(cv_ref[...])
    o_ref[...] = jnp.dot(s, w_ref[...], preferred_element_type=F32, precision=HI) + b_ref[...]


def _ada(cv, w, b):
    R, D = cv.shape
    N = w.shape[1]
    tn = N // 4
    return pl.pallas_call(
        _ada_kernel,
        grid=(N // tn,),
        in_specs=[pl.BlockSpec((R, D), lambda j: (0, 0)),
                  pl.BlockSpec((D, tn), lambda j: (0, j)),
                  pl.BlockSpec((1, tn), lambda j: (0, j))],
        out_specs=pl.BlockSpec((R, tn), lambda j: (0, j)),
        out_shape=jax.ShapeDtypeStruct((R, N), F32),
        compiler_params=_params(("arbitrary",), 40),
        name="ada_mod",
    )(cv, w, b.reshape(1, N))


def _modulation(c, c_ctx, w, b):
    B, D = c.shape
    R = -(-(B + 1) // 8) * 8
    cv = jnp.concatenate([c, c_ctx[None, :], jnp.zeros((R - B - 1, D), F32)], axis=0)
    m = _ada(cv, w, b)
    lat = m[:B].reshape(B, 1, 6, D)
    cx = jnp.broadcast_to(m[B].reshape(1, 1, 6, D), (B, 1, 6, D))
    return jnp.concatenate([lat, cx], axis=1)


def _modnorm(x, mod, g, si):
    sh = mod[si:si + 1, :]
    sc = mod[si + 1:si + 2, :]
    return (_rms(x) * g) * (1.0 + sc) + sh


def _proj_kernel(n_out, si, has_bias, x_ref, mod_ref, g_ref, *refs):
    w_refs = refs[:n_out]
    nb = 1 if has_bias else 0
    o_refs = refs[n_out + nb:]
    h = _modnorm(x_ref[...], mod_ref[...], g_ref[...], si).astype(BF16)
    for j in range(n_out):
        y = _bdot(h, w_refs[j][...])
        if has_bias and j == n_out - 1:
            y = _mlstm_gate_prep(y + refs[n_out][...])
        o_refs[j][...] = y.astype(o_refs[j].dtype)


def _mlstm_gate_prep(g):
    n = g.shape[0]
    H = ML_HEADS
    lf = _log_sigmoid(g)
    t = lax.broadcasted_iota(jnp.int32, (n, n), 0)
    s = lax.broadcasted_iota(jnp.int32, (n, n), 1)
    pre = jnp.dot((s <= t).astype(F32), lf, preferred_element_type=F32, precision=HI)
    suf = jnp.dot((s >= t).astype(F32), lf, preferred_element_type=F32, precision=HI)
    col = lax.broadcasted_iota(jnp.int32, (1, g.shape[1]), 1)
    g = jnp.where((col >= H) & (col < 2 * H), pre, g)
    return jnp.where((col >= 3 * H) & (col < 4 * H), suf, g)


def _proj(x, mod, g, ws, si, n_rows, n_lat_tiles, out_dtypes, bias=None, vmem=48):
    B, _, D = x.shape
    nt = n_rows // TM
    in_specs = [pl.BlockSpec((None, TM, D), lambda b, i: (b, i, 0)),
                pl.BlockSpec((None, None, 6, D), lambda b, i: (b, (i >= n_lat_tiles).astype(jnp.int32), 0, 0)),
                pl.BlockSpec((1, D), lambda b, i: (0, 0))]
    args = [x, mod, g.reshape(1, D)]
    for w in ws:
        in_specs.append(pl.BlockSpec(w.shape, lambda b, i: (0, 0)))
        args.append(w)
    if bias is not None:
        in_specs.append(pl.BlockSpec(bias.shape, lambda b, i: (0, 0)))
        args.append(bias)
    out_specs = [pl.BlockSpec((None, TM, w.shape[1]), lambda b, i: (b, i, 0)) for w in ws]
    out_shape = [jax.ShapeDtypeStruct((B, n_rows, w.shape[1]), dt) for w, dt in zip(ws, out_dtypes)]
    return pl.pallas_call(
        functools.partial(_proj_kernel, len(ws), si, bias is not None),
        grid=(B, nt), in_specs=in_specs, out_specs=out_specs, out_shape=out_shape,
        compiler_params=_params(("parallel", "arbitrary"), vmem),
        name="modnorm_proj",
    )(*args)


def _filter_tables(L):
    pos = np.arange(L, dtype=np.float64)
    t = pos / (L - 1)
    bands = (HY_EMB - 1) // 2
    f = np.linspace(1e-4, bands - 1, bands)
    ang = (2.0 * math.pi / L) * pos[:, None] * f[None, :]
    z = np.concatenate([t[:, None], np.cos(ang), -np.sin(ang)], axis=-1)
    zp = np.zeros((L, LANES), np.float64)
    zp[:, :HY_EMB] = z
    return zp.astype(np.float32)


def _filt_mlp_kernel(tl, z_ref, w1, b1, w2, b2, w3, fq, absd, bias_ref, o_ref):
    i = pl.program_id(0)
    z = z_ref[...]
    f = fq[...]
    h = jnp.sin(f * (jnp.dot(z, w1[...], preferred_element_type=F32, precision=HI) + b1[...]))
    h = jnp.sin(f * (jnp.dot(h, w2[...], preferred_element_type=F32, precision=HI) + b2[...]))
    h = jnp.dot(h, w3[...], preferred_element_type=F32, precision=HI)
    t = z[:, 0:1]
    h = h * (jnp.exp(-t * absd[...]) + HY_SHIFT)
    row = lax.broadcasted_iota(jnp.int32, (tl, 1), 0) + i * tl
    col = lax.broadcasted_iota(jnp.int32, (1, 2 * HY_WIDTH), 1)
    h = jnp.where(row == 0, jnp.where(col < HY_WIDTH, h + bias_ref[...], 0.0), h)
    o_ref[...] = h


def _pad2(a, r, c):
    return jnp.zeros((r, c), F32).at[:a.shape[0], :a.shape[1]].set(a.astype(F32))


def _hyena_filter_taps(L, fw1, fb1, fw2, fb2, fw3, freq, hy_bias):
    tl = min(L, 512)
    deltas = np.linspace(math.log(HY_TARGET) / HY_FAST, math.log(HY_TARGET) / HY_SLOW, HY_WIDTH)
    absd = jnp.asarray(np.abs(np.tile(deltas, 2))[None, :].astype(np.float32))
    z = jnp.asarray(_filter_tables(L))
    C2 = 2 * HY_WIDTH
    bias_full = jnp.concatenate([hy_bias.astype(F32), jnp.zeros((HY_WIDTH,), F32)]).reshape(1, C2)
    full = lambda shape: pl.BlockSpec(shape, lambda i: (0, 0))
    return pl.pallas_call(
        functools.partial(_filt_mlp_kernel, tl),
        grid=(L // tl,),
        in_specs=[pl.BlockSpec((tl, LANES), lambda i: (i, 0)),
                  full((LANES, LANES)), full((1, LANES)), full((LANES, LANES)), full((1, LANES)),
                  full((LANES, C2)), full((1, LANES)), full((1, C2)), full((1, C2))],
        out_specs=pl.BlockSpec((tl, C2), lambda i: (i, 0)),
        out_shape=jax.ShapeDtypeStruct((L, C2), F32),
        compiler_params=_params(("arbitrary",), 32),
        name="hyena_filter_mlp",
    )(z, _pad2(fw1, LANES, LANES), _pad2(fb1[None, :], 1, LANES), _pad2(fw2, LANES, LANES),
      _pad2(fb2[None, :], 1, LANES), _pad2(fw3, LANES, C2), _pad2(freq[None, :], 1, LANES), absd, bias_full)


def _dft_tables(L):
    n = 2 * L
    f = np.arange(L, dtype=np.int64)[:, None]
    s = np.arange(L, dtype=np.int64)[None, :]
    ang = (2.0 * math.pi / n) * ((f * s) % n).astype(np.float64)
    wc = np.cos(ang)
    ws = np.sin(ang)
    ws[0, :] = np.where(np.arange(L) % 2 == 0, 1.0, -1.0)
    w = np.concatenate([wc, ws], axis=0).astype(np.float32)
    return w, np.ascontiguousarray(w.T)


def _filt_dft_kernel(L, ft, wc_ref, ws_ref, h_ref, ka_ref, kb_ref, hb_s):
    f = pl.program_id(0)

    @pl.when(f == 0)
    def _():
        hb_s[...] = h_ref[...].astype(BF16)

    hb = hb_s[...]
    a = _bdot(wc_ref[...], hb)
    b = _bdot(ws_ref[...], hb)
    C = HY_WIDTH
    row0 = (lax.broadcasted_iota(jnp.int32, (ft, 1), 0) + f * ft) == 0
    sc = jnp.where(row0, 0.5 / L, 1.0 / L)
    ka_ref[...] = (a[:, :C] + a[:, C:]) * sc
    kb_ref[...] = jnp.where(row0, b[:, :C] + b[:, C:], b[:, :C] - b[:, C:]) * sc


def _hyena_filter_spectrum(L, ft, wf, taps):
    nf = L // ft
    C = HY_WIDTH
    return pl.pallas_call(
        functools.partial(_filt_dft_kernel, L, ft),
        grid=(nf,),
        in_specs=[pl.BlockSpec((ft, L), lambda f: (f, 0)),
                  pl.BlockSpec((ft, L), lambda f: (nf + f, 0)),
                  pl.BlockSpec((L, 2 * C), lambda f: (0, 0))],
        out_specs=[pl.BlockSpec((ft, C), lambda f: (f, 0)), pl.BlockSpec((ft, C), lambda f: (f, 0))],
        out_shape=[jax.ShapeDtypeStruct((L, C), F32), jax.ShapeDtypeStruct((L, C), F32)],
        scratch_shapes=[pltpu.VMEM((L, 2 * C), BF16)],
        compiler_params=_params(("arbitrary",), 48),
        name="hyena_filter_dft",
    )(wf, wf, taps)


def _conv3_rows(x, w, first, last):
    n = x.shape[0]
    xm = jnp.where(first, 0.0, pltpu.roll(x, 1, 0))
    xp = jnp.where(last, 0.0, pltpu.roll(x, n - 1, 0))
    return w[0:1, :] * xm + w[1:2, :] * x + w[2:3, :] * xp


def _hyena_kernel(L, ft, nf, x0_ref, x1_ref, v_ref, cw_ref, cb_ref, wc_ref, ws_ref, wct_ref, wst_ref,
                  ka_ref, kb_ref, *rest):
    o_ref, g_s, acc_s = rest[-3:]
    f = pl.program_id(2)
    row = lax.broadcasted_iota(jnp.int32, (L, 1), 0)
    first = row == 0
    last = row == L - 1

    @pl.when(f == 0)
    def _():
        x1 = _conv3_rows(x1_ref[...], cw_ref[1], first, last) + cb_ref[1:2, :]
        v = _conv3_rows(v_ref[...], cw_ref[2], first, last) + cb_ref[2:3, :]
        g_s[...] = (v * x1).astype(BF16)
        acc_s[...] = jnp.zeros_like(acc_s)

    g = g_s[...]
    a = _bdot(wc_ref[...], g)
    b = _bdot(ws_ref[...], g)
    row0 = (lax.broadcasted_iota(jnp.int32, (ft, 1), 0) + f * ft) == 0
    ka = ka_ref[...]
    kb = kb_ref[...]
    kbm = jnp.where(row0, 0.0, kb)
    kam = jnp.where(row0, kb, ka)
    yr = (a * ka - b * kbm).astype(BF16)
    yb = (a * kbm + b * kam).astype(BF16)
    acc_s[...] += _bdot(wct_ref[...], yr) + _bdot(wst_ref[...], yb)

    @pl.when(f == nf - 1)
    def _():
        x0 = _conv3_rows(x0_ref[...], cw_ref[0], first, last) + cb_ref[0:1, :]
        o_ref[...] = acc_s[...] * x0


def _hyena_mix(x0, x1, v, cw, cb, wf, wft, ka, kb, L, row_block, prev_out):
    B, T, C = x0.shape
    cbw = 256
    ncb = C // cbw
    ft = min(L, 512)
    nf = L // ft
    seg = lambda b, c, f: (b, row_block, c)
    in_specs = [pl.BlockSpec((None, L, cbw), seg)] * 3 + [
        pl.BlockSpec((3, SHORT_W, cbw), lambda b, c, f: (0, 0, c)),
        pl.BlockSpec((3, cbw), lambda b, c, f: (0, c)),
        pl.BlockSpec((ft, L), lambda b, c, f: (f, 0)),
        pl.BlockSpec((ft, L), lambda b, c, f: (nf + f, 0)),
        pl.BlockSpec((L, ft), lambda b, c, f: (0, f)),
        pl.BlockSpec((L, ft), lambda b, c, f: (0, nf + f)),
        pl.BlockSpec((ft, cbw), lambda b, c, f: (f, c)),
        pl.BlockSpec((ft, cbw), lambda b, c, f: (f, c)),
    ]
    args = [x0, x1, v, cw, cb, wf, wf, wft, wft, ka, kb]
    aliases = {}
    if prev_out is not None:
        in_specs.append(pl.BlockSpec(memory_space=pl.ANY))
        args.append(prev_out)
        aliases = {len(args) - 1: 0}
    return pl.pallas_call(
        functools.partial(_hyena_kernel, L, ft, nf),
        grid=(B, ncb, nf),
        in_specs=in_specs,
        out_specs=pl.BlockSpec((None, L, cbw), seg),
        out_shape=jax.ShapeDtypeStruct((B, T, C), F32),
        scratch_shapes=[pltpu.VMEM((L, cbw), BF16), pltpu.VMEM((L, cbw), F32)],
        input_output_aliases=aliases,
        compiler_params=_params(("parallel", "parallel", "arbitrary"), 56),
        name="hyena_mix",
    )(*args)


def _mlstm_kernel(S, T, q_ref, k_ref, v_ref, og_ref, gcol_ref, grow_ref, cwq_ref, cwk_ref, ng_ref, o_ref,
                  q_s, kt_s, va_s, hf_s, hb_s):
    d = ML_HEAD_DIM
    lc = ML_BLOCK
    nc = T // lc
    ncl = S // lc
    row = lax.broadcasted_iota(jnp.int32, (T, 1), 0)
    first = (row == 0) | (row == S)
    last = (row == S - 1) | (row == T - 1)
    q_s[...] = (_silu(_conv3_rows(q_ref[...], cwq_ref[...], first, last)) * (d ** -0.5)).astype(BF16)
    kc = _silu(_conv3_rows(k_ref[...], cwk_ref[...], first, last))
    for c in range(nc):
        kt_s[c] = kc[c * lc:(c + 1) * lc, :].T.astype(BF16)
    ones_col = (lax.broadcasted_iota(jnp.int32, (T, d), 1) == 0).astype(F32)
    va_s[...] = jnp.concatenate([v_ref[...], ones_col], axis=1).astype(BF16)

    ti = lax.broadcasted_iota(jnp.int32, (lc, lc), 0)
    si = lax.broadcasted_iota(jnp.int32, (lc, lc), 1)
    tril = si <= ti
    triu = si >= ti

    def direction(c, kind, mask, end, h_s, ct, m_prev):
        q = q_s[c * lc:(c + 1) * lc, :]
        kt = kt_s[c]
        va = va_s[c * lc:(c + 1) * lc, :]
        gc = gcol_ref[c]
        gr = grow_ref[c]
        i_col = gc[:, kind:kind + 1]
        b_col = gc[:, kind + 1:kind + 2]
        i_row = gr[kind:kind + 1, :]
        b_row = gr[kind + 1:kind + 2, :]
        bl = b_row[:, end:end + 1]
        r_row = i_row - b_row
        dlog = jnp.where(mask, b_col + r_row, -jnp.inf)
        inter = b_col + m_prev
        m_t = jnp.maximum(inter, jnp.max(dlog, axis=1, keepdims=True))
        w_in = jnp.exp(dlog - m_t)
        w_st = jnp.exp(inter - m_t)
        s = _bdot(q, kt) * w_in
        na = _bdot(s.astype(BF16), va) + w_st * _bdot(q, ct.astype(BF16))
        den = na[:, d:d + 1]
        inv = 1.0 / jnp.maximum(jnp.abs(den), jnp.exp(-m_t))
        h_s[c * lc:(c + 1) * lc, :] = na[:, :d] * inv
        gs_row = bl + r_row
        gs_col = bl - b_col + i_col
        m_new = jnp.maximum(bl + m_prev, jnp.max(gs_row, axis=1, keepdims=True))
        a = jnp.exp(bl + m_prev - m_new)
        ws = jnp.exp(gs_col - m_new)
        ct = a * ct + _bdot(kt, (ws * va.astype(F32)).astype(BF16))
        return ct, m_new

    c_f = c_b = jnp.zeros((d, 2 * d), F32)
    m_f = m_b = jnp.zeros((1, 1), F32)
    for i in range(nc):
        c_f, m_f = direction((i + ncl) % nc, 0, tril, lc - 1, hf_s, c_f, m_f)
        c_b, m_b = direction(nc - 1 - i, 2, triu, 0, hb_s, c_b, m_b)

    h = hf_s[...] + hb_s[...]
    o_ref[...] = (_rms(h) * ng_ref[...]) * _sigmoid(og_ref[...])


def _mlstm_mix(q, k, v, og, gcol, grow, cwq, cwk, ng, S):
    B, T, W = q.shape
    H = W // ML_HEAD_DIM
    d = ML_HEAD_DIM
    nc = T // ML_BLOCK
    head = pl.BlockSpec((None, T, d), lambda b, h: (b, 0, h))
    return pl.pallas_call(
        functools.partial(_mlstm_kernel, S, T),
        grid=(B, H),
        in_specs=[head, head, head, head,
                  pl.BlockSpec((None, None, nc, ML_BLOCK, 4), lambda b, h: (b, h, 0, 0, 0)),
                  pl.BlockSpec((None, None, nc, 4, ML_BLOCK), lambda b, h: (b, h, 0, 0, 0)),
                  pl.BlockSpec((SHORT_W, d), lambda b, h: (0, h)),
                  pl.BlockSpec((SHORT_W, d), lambda b, h: (0, h)),
                  pl.BlockSpec((1, d), lambda b, h: (0, h))],
        out_specs=head,
        out_shape=jax.ShapeDtypeStruct((B, T, W), F32),
        scratch_shapes=[pltpu.VMEM((T, d), BF16), pltpu.VMEM((nc, d, ML_BLOCK), BF16),
                        pltpu.VMEM((T, 2 * d), BF16), pltpu.VMEM((T, d), F32), pltpu.VMEM((T, d), F32)],
        compiler_params=_params(("parallel", "parallel"), 48),
        name="mlstm_mix",
    )(q, k, v, og, gcol, grow, cwq, cwk, ng)


def _outproj_kernel(n_in, x_ref, mod_ref, *refs):
    a_refs = refs[:n_in]
    w_refs = refs[n_in:2 * n_in]
    o_ref = refs[2 * n_in]
    y = _bdot(a_refs[0][...].astype(BF16), w_refs[0][...])
    for j in range(1, n_in):
        y = y + _bdot(a_refs[j][...].astype(BF16), w_refs[j][...])
    o_ref[...] = x_ref[...] + mod_ref[2:3, :] * y


def _outproj(x, mod, acts, ws, n_rows, n_lat_tiles):
    B, _, D = x.shape
    nt = n_rows // TM
    tile = lambda w: pl.BlockSpec((None, TM, w), lambda b, i: (b, i, 0))
    in_specs = [tile(D), pl.BlockSpec((None, None, 6, D), lambda b, i: (b, (i >= n_lat_tiles).astype(jnp.int32), 0, 0))]
    in_specs += [tile(a.shape[2]) for a in acts]
    in_specs += [pl.BlockSpec(w.shape, lambda b, i: (0, 0)) for w in ws]
    return pl.pallas_call(
        functools.partial(_outproj_kernel, len(acts)),
        grid=(B, nt), in_specs=in_specs, out_specs=tile(D),
        out_shape=jax.ShapeDtypeStruct((B, n_rows, D), F32),
        compiler_params=_params(("parallel", "arbitrary"), 40),
        name="out_proj",
    )(x, mod, *acts, *ws)


def _ffn_down_kernel(final, n_lat_tiles, x_ref, mod_ref, g_ref, gp_ref, gn_ref, u_ref, cw_ref, cb_ref,
                     wd_ref, fn_ref, o_ref):
    i = pl.program_id(1)
    nt = pl.num_programs(1)
    g = g_ref[...]
    row = lax.broadcasted_iota(jnp.int32, (TM, 1), 0)
    has_prev = jnp.logical_and(i != 0, i != n_lat_tiles)
    has_next = jnp.logical_and(i != n_lat_tiles - 1, i != nt - 1)
    gp = jnp.where(has_prev, gp_ref[7:8, :], 0.0)
    gn = jnp.where(has_next, gn_ref[0:1, :], 0.0)
    gm = jnp.where(row == 0, gp, pltpu.roll(g, 1, 0))
    gq = jnp.where(row == TM - 1, gn, pltpu.roll(g, TM - 1, 0))
    cw = cw_ref[...]
    gc = cw[0:1, :] * gm + cw[1:2, :] * g + cw[2:3, :] * gq + cb_ref[...]
    a = (_silu(gc) * u_ref[...]).astype(BF16)
    o = x_ref[...] + mod_ref[5:6, :] * _bdot(a, wd_ref[...])
    if final:
        o = _rms(o) * fn_ref[...]
    o_ref[...] = o


def _ffn_down(x, mod, g, u, cw, cb, wd, fn, n_lat_tiles, final):
    B, R, Fh = g.shape
    D = x.shape[2]
    nt = R // TM
    hb = TM // 8
    nhb = R // 8
    in_specs = [pl.BlockSpec((None, TM, D), lambda b, i: (b, i, 0)),
                pl.BlockSpec((None, None, 6, D), lambda b, i: (b, (i >= n_lat_tiles).astype(jnp.int32), 0, 0)),
                pl.BlockSpec((None, TM, Fh), lambda b, i: (b, i, 0)),
                pl.BlockSpec((None, 8, Fh), lambda b, i: (b, jnp.maximum(i * hb - 1, 0), 0)),
                pl.BlockSpec((None, 8, Fh), lambda b, i: (b, jnp.minimum((i + 1) * hb, nhb - 1), 0)),
                pl.BlockSpec((None, TM, Fh), lambda b, i: (b, i, 0)),
                pl.BlockSpec((SHORT_W, Fh), lambda b, i: (0, 0)),
                pl.BlockSpec((1, Fh), lambda b, i: (0, 0)),
                pl.BlockSpec((Fh, D), lambda b, i: (0, 0)),
                pl.BlockSpec((1, D), lambda b, i: (0, 0))]
    return pl.pallas_call(
        functools.partial(_ffn_down_kernel, final, n_lat_tiles),
        grid=(B, nt), in_specs=in_specs,
        out_specs=pl.BlockSpec((None, TM, D), lambda b, i: (b, i, 0)),
        out_shape=jax.ShapeDtypeStruct((B, R, D), F32),
        compiler_params=_params(("parallel", "arbitrary"), 48),
        name="ffn_down",
    )(x, mod, g, g, g, u, cw, cb.reshape(1, Fh), wd, fn.reshape(1, D))


def _conv_ffn(x, mod, norm_g, w_up, cw, cb, w_down, fn, n_rows, n_lat_tiles, final):
    Fh = w_up.shape[1] // 2
    wg = w_up[:, :Fh].astype(BF16)
    wu = w_up[:, Fh:].astype(BF16)
    g, u = _proj(x, mod, norm_g, [wg, wu], 3, n_rows, n_lat_tiles, [F32, F32], vmem=56)
    return _ffn_down(x, mod, g, u, cw, cb, w_down.astype(BF16), fn, n_lat_tiles, final)


def _rope_tables(S, T):
    p = np.arange(S)
    inv = ROPE_BASE ** (-np.arange(0, ROPE_AXIS, 2, dtype=np.float64) / ROPE_AXIS)
    ar = (p // GRID_W)[:, None] * inv[None, :]
    ac = (p % GRID_W)[:, None] * inv[None, :]
    cos = np.concatenate([np.cos(ar), np.cos(ar), np.cos(ac), np.cos(ac)], axis=1)
    sin = np.concatenate([-np.sin(ar), np.sin(ar), -np.sin(ac), np.sin(ac)], axis=1)
    cos = np.concatenate([cos, np.ones((T - S, MLA_ROPE))], axis=0)
    sin = np.concatenate([sin, np.zeros((T - S, MLA_ROPE))], axis=0)
    return cos.astype(np.float32), sin.astype(np.float32)


def _mla_proj_kernel(x_ref, mod_ref, g_ref, wdn_ref, qn_ref, kvn_ref, wuq_ref, wukv_ref, cos_ref, sin_ref,
                     qo_ref, ko_ref, vo_ref):
    H, dn, dr = MLA_HEADS, MLA_NOPE, MLA_ROPE
    h = _modnorm(x_ref[...], mod_ref[...], g_ref[...], 0).astype(BF16)
    dnp = _bdot(h, wdn_ref[...])
    qa = dnp[:, :MLA_Q_RANK]
    kva = dnp[:, MLA_Q_RANK:MLA_Q_RANK + MLA_KV_RANK]
    kr2 = dnp[:, MLA_Q_RANK + MLA_KV_RANK:]
    q = _bdot((_rms(qa) * qn_ref[...]).astype(BF16), wuq_ref[...])
    kv = _bdot((_rms(kva) * kvn_ref[...]).astype(BF16), wukv_ref[...])
    cosr = cos_ref[...]
    sinr = sin_ref[...]
    kr = kr2[:, :dr] * cosr + kr2[:, dr:] * sinr
    ro = H * dn
    for hd in range(H):
        qr = q[:, ro + hd * dr:ro + (hd + 1) * dr] * cosr + q[:, ro + H * dr + hd * dr:ro + H * dr + (hd + 1) * dr] * sinr
        qh = jnp.concatenate([q[:, hd * dn:(hd + 1) * dn], qr], axis=1) * (MLA_SCALE * math.log2(math.e))
        qo_ref[hd] = qh.astype(BF16)
        ko_ref[hd] = jnp.concatenate([kv[:, hd * dn:(hd + 1) * dn], kr], axis=1).astype(BF16)
        vo_ref[hd] = kv[:, ro + hd * MLA_V:ro + (hd + 1) * MLA_V].astype(BF16)


def _mla_proj(x, mod, norm_g, w_down, q_norm, kv_norm, w_uq, w_ukv, S):
    B, T, D = x.shape
    H, dn, dr, dv = MLA_HEADS, MLA_NOPE, MLA_ROPE, MLA_V
    dk = dn + dr
    nt = T // TM
    n_lat_tiles = S // TM
    swap = np.arange(dr) ^ (ROPE_AXIS // 2)
    wkr = w_down[:, MLA_Q_RANK + MLA_KV_RANK:]
    wdn = jnp.concatenate([w_down, wkr[:, swap]], axis=1).astype(BF16)
    wq = w_uq.reshape(MLA_Q_RANK, H, dk)
    wq_n = wq[:, :, :dn].reshape(MLA_Q_RANK, H * dn)
    wq_r = wq[:, :, dn:]
    wuq = jnp.concatenate([wq_n, wq_r.reshape(MLA_Q_RANK, H * dr), wq_r[:, :, swap].reshape(MLA_Q_RANK, H * dr)],
                          axis=1).astype(BF16)
    wkv = w_ukv.reshape(MLA_KV_RANK, H, dn + dv)
    wukv = jnp.concatenate([wkv[:, :, :dn].reshape(MLA_KV_RANK, H * dn), wkv[:, :, dn:].reshape(MLA_KV_RANK, H * dv)],
                           axis=1).astype(BF16)
    cos, sin = _rope_tables(S, T)
    full = lambda a: pl.BlockSpec(a.shape, lambda b, i: (0,) * a.ndim)
    in_specs = [pl.BlockSpec((None, TM, D), lambda b, i: (b, i, 0)),
                pl.BlockSpec((None, None, 6, D), lambda b, i: (b, (i >= n_lat_tiles).astype(jnp.int32), 0, 0)),
                pl.BlockSpec((1, D), lambda b, i: (0, 0)),
                full(wdn), pl.BlockSpec((1, MLA_Q_RANK), lambda b, i: (0, 0)),
                pl.BlockSpec((1, MLA_KV_RANK), lambda b, i: (0, 0)), full(wuq), full(wukv),
                pl.BlockSpec((TM, dr), lambda b, i: (i, 0)), pl.BlockSpec((TM, dr), lambda b, i: (i, 0))]
    headed = lambda w: pl.BlockSpec((None, H, TM, w), lambda b, i: (b, 0, i, 0))
    return pl.pallas_call(
        _mla_proj_kernel,
        grid=(B, nt), in_specs=in_specs,
        out_specs=[headed(dk), headed(dk), headed(dv)],
        out_shape=[jax.ShapeDtypeStruct((B, H, T, dk), BF16), jax.ShapeDtypeStruct((B, H, T, dk), BF16),
                   jax.ShapeDtypeStruct((B, H, T, dv), BF16)],
        compiler_params=_params(("parallel", "arbitrary"), 40),
        name="mla_proj",
    )(x, mod, norm_g.reshape(1, D), wdn, q_norm.reshape(1, -1), kv_norm.reshape(1, -1), wuq, wukv,
      jnp.asarray(cos), jnp.asarray(sin))


ATT_HEADS_PER_STEP = 2


def _attn_kernel(q_ref, k_ref, v_ref, o_ref, va_s):
    hp, T, dv = v_ref.shape

    @pl.when(pl.program_id(2) == 0)
    def _():
        ones_col = (lax.broadcasted_iota(jnp.int32, (T, dv), 1) == 0).astype(BF16)
        for j in range(hp):
            va_s[j] = jnp.concatenate([v_ref[j], ones_col], axis=1)

    for j in range(hp):
        s = lax.dot_general(q_ref[j], k_ref[j], (((1,), (1,)), ((), ())), preferred_element_type=F32)
        p = jnp.exp2(s - jnp.max(s, axis=-1, keepdims=True))
        na = _bdot(p.astype(BF16), va_s[j])
        o_ref[:, j * dv:(j + 1) * dv] = (na[:, :dv] * (1.0 / na[:, dv:dv + 1])).astype(o_ref.dtype)


def _attention(q, k, v, S):
    B, H, T, dk = k.shape
    dv = v.shape[3]
    tq = min(S, 512)
    hp = ATT_HEADS_PER_STEP
    return pl.pallas_call(
        _attn_kernel,
        grid=(B, H // hp, S // tq),
        in_specs=[pl.BlockSpec((None, hp, tq, dk), lambda b, h, i: (b, h, i, 0)),
                  pl.BlockSpec((None, hp, T, dk), lambda b, h, i: (b, h, 0, 0)),
                  pl.BlockSpec((None, hp, T, dv), lambda b, h, i: (b, h, 0, 0))],
        out_specs=pl.BlockSpec((None, tq, hp * dv), lambda b, h, i: (b, i, h)),
        out_shape=jax.ShapeDtypeStruct((B, S, H * dv), BF16),
        scratch_shapes=[pltpu.VMEM((hp, T, 2 * dv), BF16)],
        compiler_params=_params(("parallel", "parallel", "arbitrary"), 48),
        name="mla_attention",
    )(q, k, v)


def kernel(x, c, ctx, c_ctx, ada_w_0, ada_b_0, norm_mix_0, norm_ffn_0, w_in_0, hy_conv_w, hy_conv_b, hy_fw1, hy_fb1, hy_fw2, hy_fb2, hy_fw3, hy_freq, hy_bias, ml_conv_w, ml_gate_b, ml_norm_g, w_out_0, ffn_up_0, ffn_conv_w_0, ffn_conv_b_0, ffn_down_0, ada_w_1, ada_b_1, norm_mix_1, norm_ffn_1, mla_w_down, mla_q_norm, mla_kv_norm, mla_w_uq, mla_w_ukv, mla_w_o, ffn_up_1, ffn_conv_w_1, ffn_conv_b_1, ffn_down_1, final_norm):
    B, S, D = x.shape
    CL = ctx.shape[1]
    T = S + CL
    assert S % TM == 0 and CL % TM == 0 and S % CL == 0 and S % GRID_W == 0 and TM == ML_BLOCK
    nlt = S // TM
    C = HY_WIDTH
    W = ML_WIDTH
    H = ML_HEADS

    xs = jnp.concatenate([x, ctx], axis=1)
    mod0 = _modulation(c, c_ctx, ada_w_0, ada_b_0)
    mod1 = _modulation(c, c_ctx, ada_w_1, ada_b_1)

    o = 3 * C
    wi = w_in_0.astype(BF16)
    w_gate = jnp.zeros((D, LANES), BF16).at[:, :4 * H].set(wi[:, o + 4 * W:])
    b_gate = jnp.zeros((1, LANES), F32).at[0, :4 * H].set(ml_gate_b.astype(F32))
    ws0 = [wi[:, 0:C], wi[:, C:2 * C], wi[:, 2 * C:3 * C], wi[:, o:o + W], wi[:, o + W:o + 2 * W],
           wi[:, o + 2 * W:o + 3 * W], wi[:, o + 3 * W:o + 4 * W], w_gate]
    hx0, hx1, hv, mq, mk, mv, mog, gates = _proj(xs, mod0, norm_mix_0, ws0, 0, T, nlt, [F32] * 8, bias=b_gate)

    filt = (hy_fw1, hy_fb1, hy_fw2, hy_fb2, hy_fw3, hy_freq, hy_bias)
    hcw = jnp.transpose(hy_conv_w.reshape(SHORT_W, 3, C), (1, 0, 2)).astype(F32)
    hcb = hy_conv_b.reshape(3, C).astype(F32)
    hy = None
    for L, blk in ((S, 0), (CL, S // CL)):
        wf_np, wft_np = _dft_tables(L)
        wf = jnp.asarray(wf_np).astype(BF16)
        wft = jnp.asarray(wft_np).astype(BF16)
        taps = _hyena_filter_taps(L, *filt)
        ka, kb = _hyena_filter_spectrum(L, min(L, 512), wf, taps)
        hy = _hyena_mix(hx0, hx1, hv, hcw, hcb, wf, wft, ka, kb, L, blk, hy)

    nc = T // ML_BLOCK
    g4 = gates[:, :, :4 * H].reshape(B, nc, ML_BLOCK, 4, H)
    gcol = jnp.transpose(g4, (0, 4, 1, 2, 3))
    grow = jnp.transpose(g4, (0, 4, 1, 3, 2))
    mcw = ml_conv_w.astype(F32)
    ml = _mlstm_mix(mq, mk, mv, mog, gcol, grow, mcw[:, :W], mcw[:, W:], ml_norm_g.reshape(1, W).astype(F32), S)

    wo = w_out_0.astype(BF16)
    xs = _outproj(xs, mod0, [hy, ml], [wo[:C], wo[C:]], T, nlt)
    xs = _conv_ffn(xs, mod0, norm_ffn_0, ffn_up_0, ffn_conv_w_0, ffn_conv_b_0, ffn_down_0, final_norm, T, nlt, False)

    q, k, v = _mla_proj(xs, mod1, norm_mix_1, mla_w_down, mla_q_norm, mla_kv_norm, mla_w_uq, mla_w_ukv, S)
    att = _attention(q, k, v, S)
    xl = _outproj(xs, mod1, [att], [mla_w_o.astype(BF16)], S, nlt)
    return _conv_ffn(xl, mod1, norm_ffn_1, ffn_up_1, ffn_conv_w_1, ffn_conv_b_1, ffn_down_1, final_norm, S, nlt, True)
```

```python
import functools
import math

import numpy as np
import jax
import jax.numpy as jnp
from jax import lax
from jax.experimental import pallas as pl
from jax.experimental.pallas import tpu as pltpu

F32 = jnp.float32
BF16 = jnp.bfloat16
HI = lax.Precision.HIGHEST

RMS_EPS = 1e-6
GRID_W = 64
SHORT_W = 3
HY_WIDTH = 512
HY_EMB = 33
HY_FAST = 0.3
HY_SLOW = 1.5
HY_TARGET = 1e-2
HY_SHIFT = 0.05
ML_HEADS = 4
ML_HEAD_DIM = 128
ML_WIDTH = ML_HEADS * ML_HEAD_DIM
ML_BLOCK = 256
MLA_HEADS = 8
MLA_NOPE = 128
MLA_ROPE = 64
MLA_V = 128
MLA_Q_RANK = 384
MLA_KV_RANK = 256
MLA_SCALE = (MLA_NOPE + MLA_ROPE) ** -0.5
ROPE_AXIS = MLA_ROPE // 2
ROPE_BASE = 10000.0

TM = 256
LANES = 128
MIB = 1024 * 1024


def _params(sem, vmem_mib):
    return pltpu.CompilerParams(dimension_semantics=sem, vmem_limit_bytes=vmem_mib * MIB)


def _sigmoid(x):
    return 0.5 * jnp.tanh(0.5 * x) + 0.5


def _silu(x):
    return x * _sigmoid(x)


def _log_sigmoid(x):
    return jnp.minimum(x, 0.0) - jnp.log(1.0 + jnp.exp(-jnp.abs(x)))


def _rms(x):
    return x * lax.rsqrt(jnp.mean(x * x, axis=-1, keepdims=True) + RMS_EPS)


def _bdot(a, b):
    return jnp.dot(a, b, preferred_element_type=F32)


def _ada_kernel(cv_ref, w_ref, b_ref, o_ref):
    s = _silu(cv_ref[...])
    o_ref[...] = jnp.dot(s, w_ref[...], preferred_element_type=F32, precision=HI) + b_ref[...]


def _ada(cv, w, b):
    R, D = cv.shape
    N = w.shape[1]
    tn = N // 4
    return pl.pallas_call(
        _ada_kernel,
        grid=(N // tn,),
        in_specs=[pl.BlockSpec((R, D), lambda j: (0, 0)),
                  pl.BlockSpec((D, tn), lambda j: (0, j)),
                  pl.BlockSpec((1, tn), lambda j: (0, j))],
        out_specs=pl.BlockSpec((R, tn), lambda j: (0, j)),
        out_shape=jax.ShapeDtypeStruct((R, N), F32),
        compiler_params=_params(("arbitrary",), 40),
        name="ada_mod",
    )(cv, w, b.reshape(1, N))


def _modulation(c, c_ctx, w, b):
    B, D = c.shape
    R = -(-(B + 1) // 8) * 8
    cv = jnp.concatenate([c, c_ctx[None, :], jnp.zeros((R - B - 1, D), F32)], axis=0)
    m = _ada(cv, w, b)
    lat = m[:B].reshape(B, 1, 6, D)
    cx = jnp.broadcast_to(m[B].reshape(1, 1, 6, D), (B, 1, 6, D))
    return jnp.concatenate([lat, cx], axis=1)


def _modnorm(x, mod, g, si):
    sh = mod[si:si + 1, :]
    sc = mod[si + 1:si + 2, :]
    return (_rms(x) * g) * (1.0 + sc) + sh


def _row_specs(a, n_lat_tiles):
    if isinstance(a, tuple):
        w = a[0].shape[2]
        return ([pl.BlockSpec((None, TM, w), lambda b, i: (b, jnp.minimum(i, n_lat_tiles - 1), 0)),
                 pl.BlockSpec((None, TM, w), lambda b, i: (b, jnp.maximum(i - n_lat_tiles, 0), 0))], list(a))
    return [pl.BlockSpec((None, TM, a.shape[2]), lambda b, i: (b, i, 0))], [a]


def _load_rows(refs, n_lat_tiles):
    if len(refs) == 1:
        return refs[0][...]
    return jnp.where(pl.program_id(1) >= n_lat_tiles, refs[1][...], refs[0][...])


def _proj_kernel(n_x, n_out, si, has_bias, n_lat_tiles, *refs):
    x_refs = refs[:n_x]
    mod_ref, g_ref = refs[n_x:n_x + 2]
    refs = refs[n_x + 2:]
    w_refs = refs[:n_out]
    nb = 1 if has_bias else 0
    o_refs = refs[n_out + nb:]
    h = _modnorm(_load_rows(x_refs, n_lat_tiles), mod_ref[...], g_ref[...], si).astype(BF16)
    for j in range(n_out):
        y = _bdot(h, w_refs[j][...])
        if has_bias and j == n_out - 1:
            y = _mlstm_gate_prep(y + refs[n_out][...])
        o_refs[j][...] = y.astype(o_refs[j].dtype)


def _mlstm_gate_prep(g):
    n = g.shape[0]
    H = ML_HEADS
    lf = _log_sigmoid(g)
    t = lax.broadcasted_iota(jnp.int32, (n, n), 0)
    s = lax.broadcasted_iota(jnp.int32, (n, n), 1)
    pre = jnp.dot((s <= t).astype(F32), lf, preferred_element_type=F32, precision=HI)
    suf = jnp.dot((s >= t).astype(F32), lf, preferred_element_type=F32, precision=HI)
    col = lax.broadcasted_iota(jnp.int32, (1, g.shape[1]), 1)
    g = jnp.where((col >= H) & (col < 2 * H), pre, g)
    return jnp.where((col >= 3 * H) & (col < 4 * H), suf, g)


def _proj(x, mod, g, ws, si, n_rows, n_lat_tiles, out_dtypes, bias=None, vmem=48):
    B, _, _, D = mod.shape
    nt = n_rows // TM
    in_specs, args = _row_specs(x, n_lat_tiles)
    n_x = len(args)
    in_specs += [pl.BlockSpec((None, None, 6, D), lambda b, i: (b, (i >= n_lat_tiles).astype(jnp.int32), 0, 0)),
                 pl.BlockSpec((1, D), lambda b, i: (0, 0))]
    args += [mod, g.reshape(1, D)]
    for w in ws:
        in_specs.append(pl.BlockSpec(w.shape, lambda b, i: (0, 0)))
        args.append(w)
    if bias is not None:
        in_specs.append(pl.BlockSpec(bias.shape, lambda b, i: (0, 0)))
        args.append(bias)
    out_specs = [pl.BlockSpec((None, TM, w.shape[1]), lambda b, i: (b, i, 0)) for w in ws]
    out_shape = [jax.ShapeDtypeStruct((B, n_rows, w.shape[1]), dt) for w, dt in zip(ws, out_dtypes)]
    return pl.pallas_call(
        functools.partial(_proj_kernel, n_x, len(ws), si, bias is not None, n_lat_tiles),
        grid=(B, nt), in_specs=in_specs, out_specs=out_specs, out_shape=out_shape,
        compiler_params=_params(("parallel", "arbitrary"), vmem),
        name="modnorm_proj",
    )(*args)


def _filter_tables(L):
    pos = np.arange(L, dtype=np.float64)
    t = pos / (L - 1)
    bands = (HY_EMB - 1) // 2
    f = np.linspace(1e-4, bands - 1, bands)
    ang = (2.0 * math.pi / L) * pos[:, None] * f[None, :]
    z = np.concatenate([t[:, None], np.cos(ang), -np.sin(ang)], axis=-1)
    zp = np.zeros((L, LANES), np.float64)
    zp[:, :HY_EMB] = z
    return zp.astype(np.float32)


def _filt_mlp_kernel(tl, z_ref, w1, b1, w2, b2, w3, fq, absd, bias_ref, o_ref):
    i = pl.program_id(0)
    z = z_ref[...]
    f = fq[...]
    h = jnp.sin(f * (jnp.dot(z, w1[...], preferred_element_type=F32, precision=HI) + b1[...]))
    h = jnp.sin(f * (jnp.dot(h, w2[...], preferred_element_type=F32, precision=HI) + b2[...]))
    h = jnp.dot(h, w3[...], preferred_element_type=F32, precision=HI)
    t = z[:, 0:1]
    h = h * (jnp.exp(-t * absd[...]) + HY_SHIFT)
    row = lax.broadcasted_iota(jnp.int32, (tl, 1), 0) + i * tl
    col = lax.broadcasted_iota(jnp.int32, (1, 2 * HY_WIDTH), 1)
    h = jnp.where(row == 0, jnp.where(col < HY_WIDTH, h + bias_ref[...], 0.0), h)
    o_ref[...] = h


def _pad2(a, r, c):
    return jnp.zeros((r, c), F32).at[:a.shape[0], :a.shape[1]].set(a.astype(F32))


def _hyena_filter_taps(L, fw1, fb1, fw2, fb2, fw3, freq, hy_bias):
    tl = min(L, 512)
    deltas = np.linspace(math.log(HY_TARGET) / HY_FAST, math.log(HY_TARGET) / HY_SLOW, HY_WIDTH)
    absd = jnp.asarray(np.abs(np.tile(deltas, 2))[None, :].astype(np.float32))
    z = jnp.asarray(_filter_tables(L))
    C2 = 2 * HY_WIDTH
    bias_full = jnp.concatenate([hy_bias.astype(F32), jnp.zeros((HY_WIDTH,), F32)]).reshape(1, C2)
    full = lambda shape: pl.BlockSpec(shape, lambda i: (0, 0))
    return pl.pallas_call(
        functools.partial(_filt_mlp_kernel, tl),
        grid=(L // tl,),
        in_specs=[pl.BlockSpec((tl, LANES), lambda i: (i, 0)),
                  full((LANES, LANES)), full((1, LANES)), full((LANES, LANES)), full((1, LANES)),
                  full((LANES, C2)), full((1, LANES)), full((1, C2)), full((1, C2))],
        out_specs=pl.BlockSpec((tl, C2), lambda i: (i, 0)),
        out_shape=jax.ShapeDtypeStruct((L, C2), F32),
        compiler_params=_params(("arbitrary",), 32),
        name="hyena_filter_mlp",
    )(z, _pad2(fw1, LANES, LANES), _pad2(fb1[None, :], 1, LANES), _pad2(fw2, LANES, LANES),
      _pad2(fb2[None, :], 1, LANES), _pad2(fw3, LANES, C2), _pad2(freq[None, :], 1, LANES), absd, bias_full)


def _dft_tables(L):
    n = 2 * L
    f = np.arange(L, dtype=np.int64)[:, None]
    s = np.arange(L, dtype=np.int64)[None, :]
    ang = (2.0 * math.pi / n) * ((f * s) % n).astype(np.float64)
    wc = np.cos(ang)
    ws = np.sin(ang)
    ws[0, :] = np.where(np.arange(L) % 2 == 0, 1.0, -1.0)
    w = np.concatenate([wc, ws], axis=0).astype(np.float32)
    return w, np.ascontiguousarray(w.T)


def _filt_dft_kernel(L, ft, wc_ref, ws_ref, h_ref, ka_ref, kb_ref, hb_s):
    f = pl.program_id(0)

    @pl.when(f == 0)
    def _():
        hb_s[...] = h_ref[...].astype(BF16)

    hb = hb_s[...]
    a = _bdot(wc_ref[...], hb)
    b = _bdot(ws_ref[...], hb)
    C = HY_WIDTH
    row0 = (lax.broadcasted_iota(jnp.int32, (ft, 1), 0) + f * ft) == 0
    sc = jnp.where(row0, 0.5 / L, 1.0 / L)
    ka_ref[...] = (a[:, :C] + a[:, C:]) * sc
    kb_ref[...] = jnp.where(row0, b[:, :C] + b[:, C:], b[:, :C] - b[:, C:]) * sc


def _hyena_filter_spectrum(L, ft, wf, taps):
    nf = L // ft
    C = HY_WIDTH
    return pl.pallas_call(
        functools.partial(_filt_dft_kernel, L, ft),
        grid=(nf,),
        in_specs=[pl.BlockSpec((ft, L), lambda f: (f, 0)),
                  pl.BlockSpec((ft, L), lambda f: (nf + f, 0)),
                  pl.BlockSpec((L, 2 * C), lambda f: (0, 0))],
        out_specs=[pl.BlockSpec((ft, C), lambda f: (f, 0)), pl.BlockSpec((ft, C), lambda f: (f, 0))],
        out_shape=[jax.ShapeDtypeStruct((L, C), F32), jax.ShapeDtypeStruct((L, C), F32)],
        scratch_shapes=[pltpu.VMEM((L, 2 * C), BF16)],
        compiler_params=_params(("arbitrary",), 48),
        name="hyena_filter_dft",
    )(wf, wf, taps)


def _conv3_rows(x, w, first, last):
    n = x.shape[0]
    xm = jnp.where(first, 0.0, pltpu.roll(x, 1, 0))
    xp = jnp.where(last, 0.0, pltpu.roll(x, n - 1, 0))
    return w[0:1, :] * xm + w[1:2, :] * x + w[2:3, :] * xp


def _hyena_kernel(L, ft, nf, x0_ref, x1_ref, v_ref, cw_ref, cb_ref, wc_ref, ws_ref, wct_ref, wst_ref,
                  ka_ref, kb_ref, o_ref, g_s, acc_s):
    f = pl.program_id(2)
    row = lax.broadcasted_iota(jnp.int32, (L, 1), 0)
    first = row == 0
    last = row == L - 1

    @pl.when(f == 0)
    def _():
        x1 = _conv3_rows(x1_ref[...].astype(F32), cw_ref[1], first, last) + cb_ref[1:2, :]
        v = _conv3_rows(v_ref[...].astype(F32), cw_ref[2], first, last) + cb_ref[2:3, :]
        g_s[...] = (v * x1).astype(BF16)
        acc_s[...] = jnp.zeros_like(acc_s)

    g = g_s[...]
    a = _bdot(wc_ref[...], g)
    b = _bdot(ws_ref[...], g)
    row0 = (lax.broadcasted_iota(jnp.int32, (ft, 1), 0) + f * ft) == 0
    ka = ka_ref[...]
    kb = kb_ref[...]
    kbm = jnp.where(row0, 0.0, kb)
    kam = jnp.where(row0, kb, ka)
    yr = (a * ka - b * kbm).astype(BF16)
    yb = (a * kbm + b * kam).astype(BF16)
    acc_s[...] += _bdot(wct_ref[...], yr) + _bdot(wst_ref[...], yb)

    @pl.when(f == nf - 1)
    def _():
        x0 = _conv3_rows(x0_ref[...].astype(F32), cw_ref[0], first, last) + cb_ref[0:1, :]
        o_ref[...] = (acc_s[...] * x0).astype(o_ref.dtype)


def _hyena_mix(x0, x1, v, cw, cb, wf, wft, ka, kb, L, row_block):
    B, T, C = x0.shape
    cbw = 256
    ncb = C // cbw
    ft = min(L, 512)
    nf = L // ft
    in_specs = [pl.BlockSpec((None, L, cbw), lambda b, c, f: (b, row_block, c))] * 3 + [
        pl.BlockSpec((3, SHORT_W, cbw), lambda b, c, f: (0, 0, c)),
        pl.BlockSpec((3, cbw), lambda b, c, f: (0, c)),
        pl.BlockSpec((ft, L), lambda b, c, f: (f, 0)),
        pl.BlockSpec((ft, L), lambda b, c, f: (nf + f, 0)),
        pl.BlockSpec((L, ft), lambda b, c, f: (0, f)),
        pl.BlockSpec((L, ft), lambda b, c, f: (0, nf + f)),
        pl.BlockSpec((ft, cbw), lambda b, c, f: (f, c)),
        pl.BlockSpec((ft, cbw), lambda b, c, f: (f, c)),
    ]
    return pl.pallas_call(
        functools.partial(_hyena_kernel, L, ft, nf),
        grid=(B, ncb, nf),
        in_specs=in_specs,
        out_specs=pl.BlockSpec((None, L, cbw), lambda b, c, f: (b, 0, c)),
        out_shape=jax.ShapeDtypeStruct((B, L, C), BF16),
        scratch_shapes=[pltpu.VMEM((L, cbw), BF16), pltpu.VMEM((L, cbw), F32)],
        compiler_params=_params(("parallel", "parallel", "arbitrary"), 56),
        name="hyena_mix",
    )(x0, x1, v, cw, cb, wf, wf, wft, wft, ka, kb)


def _mlstm_kernel(S, T, q_ref, k_ref, v_ref, og_ref, gcol_ref, grow_ref, cwq_ref, cwk_ref, ng_ref, o_ref,
                  k_s, qt_s, vat_s, hf_s, hb_s, na_s, dc_s, ml_s):
    d = ML_HEAD_DIM
    lc = ML_BLOCK
    nc = T // lc
    ncl = S // lc
    row = lax.broadcasted_iota(jnp.int32, (T, 1), 0)
    first = (row == 0) | (row == S)
    last = (row == S - 1) | (row == T - 1)
    qc = _silu(_conv3_rows(q_ref[...].astype(F32), cwq_ref[...], first, last)) * (d ** -0.5)
    k_s[...] = _silu(_conv3_rows(k_ref[...].astype(F32), cwk_ref[...], first, last)).astype(BF16)
    ones_row = (lax.broadcasted_iota(jnp.int32, (d, lc), 0) == 0).astype(F32)
    for c in range(nc):
        qt_s[c] = qc[c * lc:(c + 1) * lc, :].T.astype(BF16)
        vt = v_ref[c * lc:(c + 1) * lc, :].astype(F32).T
        vat_s[c] = jnp.concatenate([vt, ones_row], axis=0).astype(BF16)

    si = lax.broadcasted_iota(jnp.int32, (lc, lc), 0)
    ti = lax.broadcasted_iota(jnp.int32, (lc, lc), 1)
    past = si <= ti
    future = si >= ti

    scans = ((0, past, lc - 1), (2, future, 0))

    b_tot = [[None] * nc for _ in scans]
    g_max = [[None] * nc for _ in scans]
    for c in range(nc):
        k = k_s[c * lc:(c + 1) * lc, :]
        vat = vat_s[c]
        st = _bdot(k, qt_s[c])
        vatf = vat.astype(F32)
        gc = gcol_ref[c]
        gr = grow_ref[c]
        for dn, (kind, mask, end) in enumerate(scans):
            r_col = gc[:, kind:kind + 1] - gc[:, kind + 1:kind + 2]
            b_row = gr[kind + 1:kind + 2, :]
            r_row = gr[kind:kind + 1, :] - b_row
            dlog = jnp.where(mask, r_col + b_row, -jnp.inf)
            m_loc = jnp.max(dlog, axis=0, keepdims=True)
            na_s[dn, c] = _bdot(vat, (st * jnp.exp(dlog - m_loc)).astype(BF16))
            ml_s[dn, c] = m_loc
            b_tot[dn][c] = b_row[:, end:end + 1]
            gs_row = b_tot[dn][c] + r_row
            g_max[dn][c] = jnp.max(gs_row, axis=1, keepdims=True)
            dc_s[dn, c] = _bdot((vatf * jnp.exp(gs_row - g_max[dn][c])).astype(BF16), k)

    def advance(dn, c, cs, m_prev):
        kind = scans[dn][0]
        m_loc = ml_s[dn, c]
        inter = grow_ref[c][kind + 1:kind + 2, :] + m_prev
        m_t = jnp.maximum(inter, m_loc)
        e_in = jnp.exp(m_loc - m_t)
        e_st = jnp.exp(inter - m_t)
        n_in = na_s[dn, c]
        n_st = _bdot(cs.astype(BF16), qt_s[c])
        den = e_in * n_in[d:d + 1, :] + e_st * n_st[d:d + 1, :]
        inv = 1.0 / jnp.maximum(jnp.abs(den), jnp.exp(-m_t))
        ht = (e_in * inv) * n_in[:d, :] + (e_st * inv) * n_st[:d, :]
        m_new = jnp.maximum(b_tot[dn][c] + m_prev, g_max[dn][c])
        cs = jnp.exp(b_tot[dn][c] + m_prev - m_new) * cs + jnp.exp(g_max[dn][c] - m_new) * dc_s[dn, c]
        return ht, cs, m_new

    c_f = c_b = jnp.zeros((2 * d, d), F32)
    m_f = m_b = jnp.zeros((1, 1), F32)
    for i in range(nc):
        cf, cb = (i + ncl) % nc, nc - 1 - i
        hf_s[cf], c_f, m_f = advance(0, cf, c_f, m_f)
        hb_s[cb], c_b, m_b = advance(1, cb, c_b, m_b)

    for c in range(nc):
        h = (hf_s[c] + hb_s[c]).T
        og = og_ref[c * lc:(c + 1) * lc, :].astype(F32)
        o_ref[c * lc:(c + 1) * lc, :] = ((_rms(h) * ng_ref[...]) * _sigmoid(og)).astype(o_ref.dtype)


def _mlstm_mix(q, k, v, og, gcol, grow, cwq, cwk, ng, S):
    B, T, W = q.shape
    H = W // ML_HEAD_DIM
    d = ML_HEAD_DIM
    nc = T // ML_BLOCK
    head = pl.BlockSpec((None, T, d), lambda b, h: (b, 0, h))
    return pl.pallas_call(
        functools.partial(_mlstm_kernel, S, T),
        grid=(B, H),
        in_specs=[head, head, head, head,
                  pl.BlockSpec((None, None, nc, ML_BLOCK, 4), lambda b, h: (b, h, 0, 0, 0)),
                  pl.BlockSpec((None, None, nc, 4, ML_BLOCK), lambda b, h: (b, h, 0, 0, 0)),
                  pl.BlockSpec((SHORT_W, d), lambda b, h: (0, h)),
                  pl.BlockSpec((SHORT_W, d), lambda b, h: (0, h)),
                  pl.BlockSpec((1, d), lambda b, h: (0, h))],
        out_specs=head,
        out_shape=jax.ShapeDtypeStruct((B, T, W), BF16),
        scratch_shapes=[pltpu.VMEM((T, d), BF16), pltpu.VMEM((nc, d, ML_BLOCK), BF16),
                        pltpu.VMEM((nc, 2 * d, ML_BLOCK), BF16),
                        pltpu.VMEM((nc, d, ML_BLOCK), F32), pltpu.VMEM((nc, d, ML_BLOCK), F32),
                        pltpu.VMEM((2, nc, 2 * d, ML_BLOCK), F32), pltpu.VMEM((2, nc, 2 * d, d), F32),
                        pltpu.VMEM((2, nc, 1, ML_BLOCK), F32)],
        compiler_params=_params(("parallel", "parallel"), 48),
        name="mlstm_mix",
    )(q, k, v, og, gcol, grow, cwq, cwk, ng)


def _outproj_kernel(counts, n_lat_tiles, mod_ref, *refs):
    groups = []
    for n in counts:
        groups.append(refs[:n])
        refs = refs[n:]
    n_act = len(counts) - 1
    w_refs = refs[:n_act]
    o_ref = refs[n_act]
    y = None
    for a_refs, w_ref in zip(groups[1:], w_refs):
        t = _bdot(_load_rows(a_refs, n_lat_tiles).astype(BF16), w_ref[...])
        y = t if y is None else y + t
    o_ref[...] = _load_rows(groups[0], n_lat_tiles).astype(F32) + mod_ref[2:3, :] * y


def _outproj(x, mod, acts, ws, n_rows, n_lat_tiles):
    B, _, _, D = mod.shape
    nt = n_rows // TM
    in_specs = [pl.BlockSpec((None, None, 6, D), lambda b, i: (b, (i >= n_lat_tiles).astype(jnp.int32), 0, 0))]
    args = [mod]
    counts = []
    for a in [x] + list(acts):
        sp, ar = _row_specs(a, n_lat_tiles)
        in_specs += sp
        args += ar
        counts.append(len(ar))
    in_specs += [pl.BlockSpec(w.shape, lambda b, i: (0, 0)) for w in ws]
    return pl.pallas_call(
        functools.partial(_outproj_kernel, tuple(counts), n_lat_tiles),
        grid=(B, nt), in_specs=in_specs,
        out_specs=pl.BlockSpec((None, TM, D), lambda b, i: (b, i, 0)),
        out_shape=jax.ShapeDtypeStruct((B, n_rows, D), F32),
        compiler_params=_params(("parallel", "arbitrary"), 40),
        name="out_proj",
    )(*args, *ws)


HALO = 8
FFN_SPLIT = 2


def _ffn_kernel(final, n_lat_tiles, x_ref, xp_ref, xn_ref, mod_ref, g_ref, wg_ref, wu_ref, cw_ref, cb_ref,
                wd_ref, fn_ref, o_ref):
    i = pl.program_id(1)
    nt = pl.num_programs(1)
    x = x_ref[...]
    mod = mod_ref[...]
    xe = jnp.concatenate([xp_ref[...], x, xn_ref[...]], axis=0)
    hf = _modnorm(xe, mod, g_ref[...], 3)
    he = hf.astype(BF16)
    h = hf[HALO:HALO + TM].astype(BF16)
    has_prev = jnp.logical_and(i != 0, i != n_lat_tiles)
    has_next = jnp.logical_and(i != n_lat_tiles - 1, i != nt - 1)
    row = lax.broadcasted_iota(jnp.int32, (TM + 2 * HALO, 1), 0)
    inside = jnp.logical_and(jnp.logical_or(row >= HALO, has_prev), jnp.logical_or(row < TM + HALO, has_next))
    fh = wg_ref.shape[1]
    fc = fh // FFN_SPLIT
    y = None
    for c in range(FFN_SPLIT):
        cols = slice(c * fc, (c + 1) * fc)
        ge = jnp.where(inside, _bdot(he, wg_ref[:, cols]), 0.0)
        gm = pltpu.roll(ge, 1, 0)[HALO:HALO + TM]
        gq = pltpu.roll(ge, TM + 2 * HALO - 1, 0)[HALO:HALO + TM]
        cw = cw_ref[:, cols]
        gc = cw[0:1, :] * gm + cw[1:2, :] * ge[HALO:HALO + TM] + cw[2:3, :] * gq + cb_ref[:, cols]
        a = (_silu(gc) * _bdot(h, wu_ref[:, cols])).astype(BF16)
        t = _bdot(a, wd_ref[cols, :])
        y = t if y is None else y + t
    o = x + mod[5:6, :] * y
    if final:
        o = _rms(o) * fn_ref[...]
    o_ref[...] = o


def _conv_ffn(x, mod, norm_g, w_up, cw, cb, w_down, fn, n_lat_tiles, final):
    B, R, D = x.shape
    Fh = w_up.shape[1] // 2
    wg = w_up[:, :Fh].astype(BF16)
    wu = w_up[:, Fh:].astype(BF16)
    nt = R // TM
    hb = TM // HALO
    nhb = R // HALO
    once = lambda shape: pl.BlockSpec(shape, lambda b, i: (0, 0), pipeline_mode=pl.Buffered(1))
    in_specs = [pl.BlockSpec((None, TM, D), lambda b, i: (b, i, 0)),
                pl.BlockSpec((None, HALO, D), lambda b, i: (b, jnp.maximum(i * hb - 1, 0), 0)),
                pl.BlockSpec((None, HALO, D), lambda b, i: (b, jnp.minimum((i + 1) * hb, nhb - 1), 0)),
                pl.BlockSpec((None, None, 6, D), lambda b, i: (b, (i >= n_lat_tiles).astype(jnp.int32), 0, 0)),
                once((1, D)), once((D, Fh)), once((D, Fh)), once((SHORT_W, Fh)), once((1, Fh)), once((Fh, D)),
                once((1, D))]
    return pl.pallas_call(
        functools.partial(_ffn_kernel, final, n_lat_tiles),
        grid=(B, nt), in_specs=in_specs,
        out_specs=pl.BlockSpec((None, TM, D), lambda b, i: (b, i, 0)),
        out_shape=jax.ShapeDtypeStruct((B, R, D), F32),
        compiler_params=_params(("parallel", "arbitrary"), 56),
        name="conv_ffn",
    )(x, x, x, mod, norm_g.reshape(1, D), wg, wu, cw.astype(F32), cb.reshape(1, Fh).astype(F32),
      w_down.astype(BF16), fn.reshape(1, D))


def _rope_tables(S, T):
    p = np.arange(S)
    inv = ROPE_BASE ** (-np.arange(0, ROPE_AXIS, 2, dtype=np.float64) / ROPE_AXIS)
    ar = (p // GRID_W)[:, None] * inv[None, :]
    ac = (p % GRID_W)[:, None] * inv[None, :]
    cos = np.concatenate([np.cos(ar), np.cos(ar), np.cos(ac), np.cos(ac)], axis=1)
    sin = np.concatenate([-np.sin(ar), np.sin(ar), -np.sin(ac), np.sin(ac)], axis=1)
    cos = np.concatenate([cos, np.ones((T - S, MLA_ROPE))], axis=0)
    sin = np.concatenate([sin, np.zeros((T - S, MLA_ROPE))], axis=0)
    return cos.astype(np.float32), sin.astype(np.float32)


def _mla_proj_kernel(x_ref, mod_ref, g_ref, wdn_ref, qn_ref, kvn_ref, wuq_ref, wukv_ref, cos_ref, sin_ref,
                     qo_ref, ko_ref, vo_ref):
    H, dn, dr = MLA_HEADS, MLA_NOPE, MLA_ROPE
    h = _modnorm(x_ref[...], mod_ref[...], g_ref[...], 0).astype(BF16)
    dnp = _bdot(h, wdn_ref[...])
    qa = dnp[:, :MLA_Q_RANK]
    kva = dnp[:, MLA_Q_RANK:MLA_Q_RANK + MLA_KV_RANK]
    kr2 = dnp[:, MLA_Q_RANK + MLA_KV_RANK:]
    q = _bdot((_rms(qa) * qn_ref[...]).astype(BF16), wuq_ref[...])
    kv = _bdot((_rms(kva) * kvn_ref[...]).astype(BF16), wukv_ref[...])
    cosr = cos_ref[...]
    sinr = sin_ref[...]
    kr = kr2[:, :dr] * cosr + kr2[:, dr:] * sinr
    ro = H * dn
    for hd in range(H):
        qr = q[:, ro + hd * dr:ro + (hd + 1) * dr] * cosr + q[:, ro + H * dr + hd * dr:ro + H * dr + (hd + 1) * dr] * sinr
        qh = jnp.concatenate([q[:, hd * dn:(hd + 1) * dn], qr], axis=1) * (MLA_SCALE * math.log2(math.e))
        qo_ref[hd] = qh.astype(BF16)
        ko_ref[hd] = jnp.concatenate([kv[:, hd * dn:(hd + 1) * dn], kr], axis=1).astype(BF16)
        vo_ref[hd] = kv[:, ro + hd * MLA_V:ro + (hd + 1) * MLA_V].astype(BF16)


def _mla_proj(x, mod, norm_g, w_down, q_norm, kv_norm, w_uq, w_ukv, S):
    B, T, D = x.shape
    H, dn, dr, dv = MLA_HEADS, MLA_NOPE, MLA_ROPE, MLA_V
    dk = dn + dr
    nt = T // TM
    n_lat_tiles = S // TM
    swap = np.arange(dr) ^ (ROPE_AXIS // 2)
    wkr = w_down[:, MLA_Q_RANK + MLA_KV_RANK:]
    wdn = jnp.concatenate([w_down, wkr[:, swap]], axis=1).astype(BF16)
    wq = w_uq.reshape(MLA_Q_RANK, H, dk)
    wq_n = wq[:, :, :dn].reshape(MLA_Q_RANK, H * dn)
    wq_r = wq[:, :, dn:]
    wuq = jnp.concatenate([wq_n, wq_r.reshape(MLA_Q_RANK, H * dr), wq_r[:, :, swap].reshape(MLA_Q_RANK, H * dr)],
                          axis=1).astype(BF16)
    wkv = w_ukv.reshape(MLA_KV_RANK, H, dn + dv)
    wukv = jnp.concatenate([wkv[:, :, :dn].reshape(MLA_KV_RANK, H * dn), wkv[:, :, dn:].reshape(MLA_KV_RANK, H * dv)],
                           axis=1).astype(BF16)
    cos, sin = _rope_tables(S, T)
    full = lambda a: pl.BlockSpec(a.shape, lambda b, i: (0,) * a.ndim)
    in_specs = [pl.BlockSpec((None, TM, D), lambda b, i: (b, i, 0)),
                pl.BlockSpec((None, None, 6, D), lambda b, i: (b, (i >= n_lat_tiles).astype(jnp.int32), 0, 0)),
                pl.BlockSpec((1, D), lambda b, i: (0, 0)),
                full(wdn), pl.BlockSpec((1, MLA_Q_RANK), lambda b, i: (0, 0)),
                pl.BlockSpec((1, MLA_KV_RANK), lambda b, i: (0, 0)), full(wuq), full(wukv),
                pl.BlockSpec((TM, dr), lambda b, i: (i, 0)), pl.BlockSpec((TM, dr), lambda b, i: (i, 0))]
    headed = lambda w: pl.BlockSpec((None, H, TM, w), lambda b, i: (b, 0, i, 0))
    return pl.pallas_call(
        _mla_proj_kernel,
        grid=(B, nt), in_specs=in_specs,
        out_specs=[headed(dk), headed(dk), headed(dv)],
        out_shape=[jax.ShapeDtypeStruct((B, H, T, dk), BF16), jax.ShapeDtypeStruct((B, H, T, dk), BF16),
                   jax.ShapeDtypeStruct((B, H, T, dv), BF16)],
        compiler_params=_params(("parallel", "arbitrary"), 40),
        name="mla_proj",
    )(x, mod, norm_g.reshape(1, D), wdn, q_norm.reshape(1, -1), kv_norm.reshape(1, -1), wuq, wukv,
      jnp.asarray(cos), jnp.asarray(sin))


ATT_HEADS_PER_STEP = 2
ATT_KEY_SLAB = 768


def _attn_kernel(q_ref, k_ref, v_ref, o_ref, va_s):
    hp, T, dv = v_ref.shape

    @pl.when(pl.program_id(2) == 0)
    def _():
        ones_col = (lax.broadcasted_iota(jnp.int32, (T, dv), 1) == 0).astype(BF16)
        for j in range(hp):
            va_s[j] = jnp.concatenate([v_ref[j], ones_col], axis=1)

    kw = max(w for w in range(LANES, ATT_KEY_SLAB + 1, LANES) if T % w == 0)
    slabs = [slice(c * kw, (c + 1) * kw) for c in range(T // kw)]
    nt = (((1,), (1,)), ((), ()))
    s = [[lax.dot_general(q_ref[j], k_ref[j, sl, :], nt, preferred_element_type=F32) for sl in slabs]
         for j in range(hp)]
    for j in range(hp):
        m = functools.reduce(jnp.maximum, [jnp.max(sc, axis=-1, keepdims=True) for sc in s[j]])
        na = None
        for sc, sl in zip(s[j], slabs):
            t = _bdot(jnp.exp2(sc - m).astype(BF16), va_s[j, sl, :])
            na = t if na is None else na + t
        o_ref[:, j * dv:(j + 1) * dv] = (na[:, :dv] * (1.0 / na[:, dv:dv + 1])).astype(o_ref.dtype)


def _attention(q, k, v, S):
    B, H, T, dk = k.shape
    dv = v.shape[3]
    tq = min(S, 512)
    hp = ATT_HEADS_PER_STEP
    return pl.pallas_call(
        _attn_kernel,
        grid=(B, H // hp, S // tq),
        in_specs=[pl.BlockSpec((None, hp, tq, dk), lambda b, h, i: (b, h, i, 0)),
                  pl.BlockSpec((None, hp, T, dk), lambda b, h, i: (b, h, 0, 0)),
                  pl.BlockSpec((None, hp, T, dv), lambda b, h, i: (b, h, 0, 0))],
        out_specs=pl.BlockSpec((None, tq, hp * dv), lambda b, h, i: (b, i, h)),
        out_shape=jax.ShapeDtypeStruct((B, S, H * dv), BF16),
        scratch_shapes=[pltpu.VMEM((hp, T, 2 * dv), BF16)],
        compiler_params=_params(("parallel", "parallel", "arbitrary"), 48),
        name="mla_attention",
    )(q, k, v)


def kernel(x, c, ctx, c_ctx, ada_w_0, ada_b_0, norm_mix_0, norm_ffn_0, w_in_0, hy_conv_w, hy_conv_b, hy_fw1, hy_fb1, hy_fw2, hy_fb2, hy_fw3, hy_freq, hy_bias, ml_conv_w, ml_gate_b, ml_norm_g, w_out_0, ffn_up_0, ffn_conv_w_0, ffn_conv_b_0, ffn_down_0, ada_w_1, ada_b_1, norm_mix_1, norm_ffn_1, mla_w_down, mla_q_norm, mla_kv_norm, mla_w_uq, mla_w_ukv, mla_w_o, ffn_up_1, ffn_conv_w_1, ffn_conv_b_1, ffn_down_1, final_norm):
    B, S, D = x.shape
    CL = ctx.shape[1]
    T = S + CL
    assert S % TM == 0 and CL % TM == 0 and S % CL == 0 and S % GRID_W == 0 and TM == ML_BLOCK
    nlt = S // TM
    C = HY_WIDTH
    W = ML_WIDTH
    H = ML_HEADS

    mod0 = _modulation(c, c_ctx, ada_w_0, ada_b_0)
    mod1 = _modulation(c, c_ctx, ada_w_1, ada_b_1)

    o = 3 * C
    wi = w_in_0.astype(BF16)
    w_gate = jnp.zeros((D, LANES), BF16).at[:, :4 * H].set(wi[:, o + 4 * W:])
    b_gate = jnp.zeros((1, LANES), F32).at[0, :4 * H].set(ml_gate_b.astype(F32))
    ws0 = [wi[:, 0:C], wi[:, C:2 * C], wi[:, 2 * C:3 * C], wi[:, o:o + W], wi[:, o + W:o + 2 * W],
           wi[:, o + 2 * W:o + 3 * W], wi[:, o + 3 * W:o + 4 * W], w_gate]
    hx0, hx1, hv, mq, mk, mv, mog, gates = _proj((x, ctx), mod0, norm_mix_0, ws0, 0, T, nlt, [BF16] * 7 + [F32],
                                                 bias=b_gate)

    filt = (hy_fw1, hy_fb1, hy_fw2, hy_fb2, hy_fw3, hy_freq, hy_bias)
    hcw = jnp.transpose(hy_conv_w.reshape(SHORT_W, 3, C), (1, 0, 2)).astype(F32)
    hcb = hy_conv_b.reshape(3, C).astype(F32)
    hy = []
    for L, blk in ((S, 0), (CL, S // CL)):
        wf_np, wft_np = _dft_tables(L)
        wf = jnp.asarray(wf_np).astype(BF16)
        wft = jnp.asarray(wft_np).astype(BF16)
        taps = _hyena_filter_taps(L, *filt)
        ka, kb = _hyena_filter_spectrum(L, min(L, 512), wf, taps)
        hy.append(_hyena_mix(hx0, hx1, hv, hcw, hcb, wf, wft, ka, kb, L, blk))

    nc = T // ML_BLOCK
    g4 = gates[:, :, :4 * H].reshape(B, nc, ML_BLOCK, 4, H)
    gcol = jnp.transpose(g4, (0, 4, 1, 2, 3))
    grow = jnp.transpose(g4, (0, 4, 1, 3, 2))
    mcw = ml_conv_w.astype(F32)
    ml = _mlstm_mix(mq, mk, mv, mog, gcol, grow, mcw[:, :W], mcw[:, W:], ml_norm_g.reshape(1, W).astype(F32), S)

    wo = w_out_0.astype(BF16)
    xs = _outproj((x, ctx), mod0, [tuple(hy), ml], [wo[:C], wo[C:]], T, nlt)
    xs = _conv_ffn(xs, mod0, norm_ffn_0, ffn_up_0, ffn_conv_w_0, ffn_conv_b_0, ffn_down_0, final_norm, nlt, False)

    q, k, v = _mla_proj(xs, mod1, norm_mix_1, mla_w_down, mla_q_norm, mla_kv_norm, mla_w_uq, mla_w_ukv, S)
    att = _attention(q, k, v, S)
    xl = _outproj(xs, mod1, [att], [mla_w_o.astype(BF16)], S, nlt)
    return _conv_ffn(xl, mod1, norm_ffn_1, ffn_up_1, ffn_conv_w_1, ffn_conv_b_1, ffn_down_1, final_norm, nlt, True)
```

```python
import functools
import math

import numpy as np
import jax
import jax.numpy as jnp
from jax import lax
from jax.experimental import pallas as pl
from jax.experimental.pallas import tpu as pltpu

F32 = jnp.float32
BF16 = jnp.bfloat16
HI = lax.Precision.HIGHEST

RMS_EPS = 1e-6
GRID_W = 64
SHORT_W = 3
HY_WIDTH = 512
HY_EMB = 33
HY_FAST = 0.3
HY_SLOW = 1.5
HY_TARGET = 1e-2
HY_SHIFT = 0.05
ML_HEADS = 4
ML_HEAD_DIM = 128
ML_WIDTH = ML_HEADS * ML_HEAD_DIM
ML_BLOCK = 256
MLA_HEADS = 8
MLA_NOPE = 128
MLA_ROPE = 64
MLA_V = 128
MLA_Q_RANK = 384
MLA_KV_RANK = 256
MLA_SCALE = (MLA_NOPE + MLA_ROPE) ** -0.5
MLA_DK_PAD = 256
ROPE_AXIS = MLA_ROPE // 2
ROPE_BASE = 10000.0

TM = 256
LANES = 128
MIB = 1024 * 1024


def _params(sem, vmem_mib):
    return pltpu.CompilerParams(dimension_semantics=sem, vmem_limit_bytes=vmem_mib * MIB)


def _sigmoid(x):
    return 0.5 * jnp.tanh(0.5 * x) + 0.5


def _silu(x):
    return x * _sigmoid(x)


def _log_sigmoid(x):
    return jnp.minimum(x, 0.0) - jnp.log(1.0 + jnp.exp(-jnp.abs(x)))


def _rms(x):
    return x * lax.rsqrt(jnp.mean(x * x, axis=-1, keepdims=True) + RMS_EPS)


def _bdot(a, b):
    return jnp.dot(a, b, preferred_element_type=F32)


def _ada_kernel(cv_ref, w_ref, b_ref, o_ref):
    s = _silu(cv_ref[...])
    o_ref[...] = jnp.dot(s, w_ref[...], preferred_element_type=F32, precision=HI) + b_ref[...]


def _ada(cv, w, b):
    R, D = cv.shape
    N = w.shape[1]
    tn = N // 4
    return pl.pallas_call(
        _ada_kernel,
        grid=(N // tn,),
        in_specs=[pl.BlockSpec((R, D), lambda j: (0, 0)),
                  pl.BlockSpec((D, tn), lambda j: (0, j)),
                  pl.BlockSpec((1, tn), lambda j: (0, j))],
        out_specs=pl.BlockSpec((R, tn), lambda j: (0, j)),
        out_shape=jax.ShapeDtypeStruct((R, N), F32),
        compiler_params=_params(("arbitrary",), 40),
        name="ada_mod",
    )(cv, w, b.reshape(1, N))


def _modulation(c, c_ctx, w, b):
    B, D = c.shape
    R = -(-(B + 1) // 8) * 8
    cv = jnp.concatenate([c, c_ctx[None, :], jnp.zeros((R - B - 1, D), F32)], axis=0)
    m = _ada(cv, w, b)
    lat = m[:B].reshape(B, 1, 6, D)
    cx = jnp.broadcast_to(m[B].reshape(1, 1, 6, D), (B, 1, 6, D))
    return jnp.concatenate([lat, cx], axis=1)


def _modnorm(x, mod, g, si):
    sh = mod[si:si + 1, :]
    sc = mod[si + 1:si + 2, :]
    return (_rms(x) * g) * (1.0 + sc) + sh


def _row_specs(a, n_lat_tiles):
    if isinstance(a, tuple):
        w = a[0].shape[2]
        return ([pl.BlockSpec((None, TM, w), lambda b, i: (b, jnp.minimum(i, n_lat_tiles - 1), 0)),
                 pl.BlockSpec((None, TM, w), lambda b, i: (b, jnp.maximum(i - n_lat_tiles, 0), 0))], list(a))
    return [pl.BlockSpec((None, TM, a.shape[2]), lambda b, i: (b, i, 0))], [a]


def _load_rows(refs, n_lat_tiles):
    if len(refs) == 1:
        return refs[0][...]
    return jnp.where(pl.program_id(1) >= n_lat_tiles, refs[1][...], refs[0][...])


def _proj_kernel(n_x, n_out, si, has_bias, n_lat_tiles, *refs):
    x_refs = refs[:n_x]
    mod_ref, g_ref = refs[n_x:n_x + 2]
    refs = refs[n_x + 2:]
    w_refs = refs[:n_out]
    nb = 1 if has_bias else 0
    o_refs = refs[n_out + nb:]
    h = _modnorm(_load_rows(x_refs, n_lat_tiles), mod_ref[...], g_ref[...], si).astype(BF16)
    for j in range(n_out):
        y = _bdot(h, w_refs[j][...])
        if has_bias and j == n_out - 1:
            y = _mlstm_gate_prep(y + refs[n_out][...])
        o_refs[j][...] = y.astype(o_refs[j].dtype)


def _mlstm_gate_prep(g):
    n = g.shape[0]
    H = ML_HEADS
    lf = _log_sigmoid(g)
    t = lax.broadcasted_iota(jnp.int32, (n, n), 0)
    s = lax.broadcasted_iota(jnp.int32, (n, n), 1)
    hi = lf.astype(BF16)
    r1 = lf - hi.astype(F32)
    mid = r1.astype(BF16)
    parts = jnp.concatenate([hi, mid, (r1 - mid.astype(F32)).astype(BF16)], axis=1)
    w = g.shape[1]
    pre3 = _bdot((s <= t).astype(BF16), parts)
    suf3 = _bdot((s >= t).astype(BF16), parts)
    pre = pre3[:, :w] + pre3[:, w:2 * w] + pre3[:, 2 * w:]
    suf = suf3[:, :w] + suf3[:, w:2 * w] + suf3[:, 2 * w:]
    col = lax.broadcasted_iota(jnp.int32, (1, g.shape[1]), 1)
    g = jnp.where((col >= H) & (col < 2 * H), pre, g)
    return jnp.where((col >= 3 * H) & (col < 4 * H), suf, g)


def _proj(x, mod, g, ws, si, n_rows, n_lat_tiles, out_dtypes, bias=None, vmem=48):
    B, _, _, D = mod.shape
    nt = n_rows // TM
    in_specs, args = _row_specs(x, n_lat_tiles)
    n_x = len(args)
    in_specs += [pl.BlockSpec((None, None, 6, D), lambda b, i: (b, (i >= n_lat_tiles).astype(jnp.int32), 0, 0)),
                 pl.BlockSpec((1, D), lambda b, i: (0, 0))]
    args += [mod, g.reshape(1, D)]
    for w in ws:
        in_specs.append(pl.BlockSpec(w.shape, lambda b, i: (0, 0)))
        args.append(w)
    if bias is not None:
        in_specs.append(pl.BlockSpec(bias.shape, lambda b, i: (0, 0)))
        args.append(bias)
    out_specs = [pl.BlockSpec((None, TM, w.shape[1]), lambda b, i: (b, i, 0)) for w in ws]
    out_shape = [jax.ShapeDtypeStruct((B, n_rows, w.shape[1]), dt) for w, dt in zip(ws, out_dtypes)]
    return pl.pallas_call(
        functools.partial(_proj_kernel, n_x, len(ws), si, bias is not None, n_lat_tiles),
        grid=(B, nt), in_specs=in_specs, out_specs=out_specs, out_shape=out_shape,
        compiler_params=_params(("parallel", "arbitrary"), vmem),
        name="modnorm_proj",
    )(*args)


def _filter_tables(L):
    pos = np.arange(L, dtype=np.float64)
    t = pos / (L - 1)
    bands = (HY_EMB - 1) // 2
    f = np.linspace(1e-4, bands - 1, bands)
    ang = (2.0 * math.pi / L) * pos[:, None] * f[None, :]
    z = np.concatenate([t[:, None], np.cos(ang), -np.sin(ang)], axis=-1)
    zp = np.zeros((L, LANES), np.float64)
    zp[:, :HY_EMB] = z
    return zp.astype(np.float32)


def _filt_mlp_kernel(tl, z_ref, w1, b1, w2, b2, w3, fq, absd, bias_ref, o_ref):
    i = pl.program_id(0)
    z = z_ref[...]
    f = fq[...]
    h = jnp.sin(f * (jnp.dot(z, w1[...], preferred_element_type=F32, precision=HI) + b1[...]))
    h = jnp.sin(f * (jnp.dot(h, w2[...], preferred_element_type=F32, precision=HI) + b2[...]))
    h = jnp.dot(h, w3[...], preferred_element_type=F32, precision=HI)
    t = z[:, 0:1]
    h = h * (jnp.exp(-t * absd[...]) + HY_SHIFT)
    row = lax.broadcasted_iota(jnp.int32, (tl, 1), 0) + i * tl
    col = lax.broadcasted_iota(jnp.int32, (1, 2 * HY_WIDTH), 1)
    h = jnp.where(row == 0, jnp.where(col < HY_WIDTH, h + bias_ref[...], 0.0), h)
    o_ref[...] = h


def _pad2(a, r, c):
    return jnp.zeros((r, c), F32).at[:a.shape[0], :a.shape[1]].set(a.astype(F32))


def _hyena_filter_taps(L, fw1, fb1, fw2, fb2, fw3, freq, hy_bias):
    tl = min(L, 512)
    deltas = np.linspace(math.log(HY_TARGET) / HY_FAST, math.log(HY_TARGET) / HY_SLOW, HY_WIDTH)
    absd = jnp.asarray(np.abs(np.tile(deltas, 2))[None, :].astype(np.float32))
    z = jnp.asarray(_filter_tables(L))
    C2 = 2 * HY_WIDTH
    bias_full = jnp.concatenate([hy_bias.astype(F32), jnp.zeros((HY_WIDTH,), F32)]).reshape(1, C2)
    full = lambda shape: pl.BlockSpec(shape, lambda i: (0, 0))
    return pl.pallas_call(
        functools.partial(_filt_mlp_kernel, tl),
        grid=(L // tl,),
        in_specs=[pl.BlockSpec((tl, LANES), lambda i: (i, 0)),
                  full((LANES, LANES)), full((1, LANES)), full((LANES, LANES)), full((1, LANES)),
                  full((LANES, C2)), full((1, LANES)), full((1, C2)), full((1, C2))],
        out_specs=pl.BlockSpec((tl, C2), lambda i: (i, 0)),
        out_shape=jax.ShapeDtypeStruct((L, C2), F32),
        compiler_params=_params(("arbitrary",), 32),
        name="hyena_filter_mlp",
    )(z, _pad2(fw1, LANES, LANES), _pad2(fb1[None, :], 1, LANES), _pad2(fw2, LANES, LANES),
      _pad2(fb2[None, :], 1, LANES), _pad2(fw3, LANES, C2), _pad2(freq[None, :], 1, LANES), absd, bias_full)


def _dft_tables(L):
    n = 2 * L
    f = np.arange(L, dtype=np.int64)[:, None]
    s = np.arange(L, dtype=np.int64)[None, :]
    ang = (2.0 * math.pi / n) * ((f * s) % n).astype(np.float64)
    wc = np.cos(ang)
    ws = np.sin(ang)
    ws[0, :] = np.where(np.arange(L) % 2 == 0, 1.0, -1.0)
    w = np.concatenate([wc, ws], axis=0).astype(np.float32)
    return w, np.ascontiguousarray(w.T)


def _filt_dft_kernel(L, ft, wc_ref, ws_ref, h_ref, ka_ref, kb_ref, hb_s):
    f = pl.program_id(0)

    @pl.when(f == 0)
    def _():
        hb_s[...] = h_ref[...].astype(BF16)

    hb = hb_s[...]
    a = _bdot(wc_ref[...], hb)
    b = _bdot(ws_ref[...], hb)
    C = HY_WIDTH
    row0 = (lax.broadcasted_iota(jnp.int32, (ft, 1), 0) + f * ft) == 0
    sc = jnp.where(row0, 0.5 / L, 1.0 / L)
    ka_ref[...] = (a[:, :C] + a[:, C:]) * sc
    kb_ref[...] = jnp.where(row0, b[:, :C] + b[:, C:], b[:, :C] - b[:, C:]) * sc


def _hyena_filter_spectrum(L, ft, wf, taps):
    nf = L // ft
    C = HY_WIDTH
    return pl.pallas_call(
        functools.partial(_filt_dft_kernel, L, ft),
        grid=(nf,),
        in_specs=[pl.BlockSpec((ft, L), lambda f: (f, 0)),
                  pl.BlockSpec((ft, L), lambda f: (nf + f, 0)),
                  pl.BlockSpec((L, 2 * C), lambda f: (0, 0))],
        out_specs=[pl.BlockSpec((ft, C), lambda f: (f, 0)), pl.BlockSpec((ft, C), lambda f: (f, 0))],
        out_shape=[jax.ShapeDtypeStruct((L, C), F32), jax.ShapeDtypeStruct((L, C), F32)],
        scratch_shapes=[pltpu.VMEM((L, 2 * C), BF16)],
        compiler_params=_params(("arbitrary",), 48),
        name="hyena_filter_dft",
    )(wf, wf, taps)


def _conv3_rows(x, w, first, last):
    n = x.shape[0]
    xm = jnp.where(first, 0.0, pltpu.roll(x, 1, 0))
    xp = jnp.where(last, 0.0, pltpu.roll(x, n - 1, 0))
    return w[0:1, :] * xm + w[1:2, :] * x + w[2:3, :] * xp


def _hyena_kernel(L, ft, nf, x0_ref, x1_ref, v_ref, cw_ref, cb_ref, wc_ref, ws_ref, wct_ref, wst_ref,
                  ka_ref, kb_ref, o_ref, g_s, acc_s):
    f = pl.program_id(2)
    row = lax.broadcasted_iota(jnp.int32, (L, 1), 0)
    first = row == 0
    last = row == L - 1

    @pl.when(f == 0)
    def _():
        x1 = _conv3_rows(x1_ref[...].astype(F32), cw_ref[1], first, last) + cb_ref[1:2, :]
        v = _conv3_rows(v_ref[...].astype(F32), cw_ref[2], first, last) + cb_ref[2:3, :]
        g_s[...] = (v * x1).astype(BF16)
        acc_s[...] = jnp.zeros_like(acc_s)

    g = g_s[...]
    a = _bdot(wc_ref[...], g)
    b = _bdot(ws_ref[...], g)
    row0 = (lax.broadcasted_iota(jnp.int32, (ft, 1), 0) + f * ft) == 0
    ka = ka_ref[...]
    kb = kb_ref[...]
    kbm = jnp.where(row0, 0.0, kb)
    kam = jnp.where(row0, kb, ka)
    yr = (a * ka - b * kbm).astype(BF16)
    yb = (a * kbm + b * kam).astype(BF16)
    acc_s[...] += _bdot(wct_ref[...], yr) + _bdot(wst_ref[...], yb)

    @pl.when(f == nf - 1)
    def _():
        x0 = _conv3_rows(x0_ref[...].astype(F32), cw_ref[0], first, last) + cb_ref[0:1, :]
        o_ref[...] = (acc_s[...] * x0).astype(o_ref.dtype)


def _hyena_mix(x0, x1, v, cw, cb, wf, wft, ka, kb, L, row_block):
    B, T, C = x0.shape
    cbw = C
    ncb = C // cbw
    ft = min(L, 512)
    nf = L // ft
    in_specs = [pl.BlockSpec((None, L, cbw), lambda b, c, f: (b, row_block, c))] * 3 + [
        pl.BlockSpec((3, SHORT_W, cbw), lambda b, c, f: (0, 0, c)),
        pl.BlockSpec((3, cbw), lambda b, c, f: (0, c)),
        pl.BlockSpec((ft, L), lambda b, c, f: (f, 0)),
        pl.BlockSpec((ft, L), lambda b, c, f: (nf + f, 0)),
        pl.BlockSpec((L, ft), lambda b, c, f: (0, f)),
        pl.BlockSpec((L, ft), lambda b, c, f: (0, nf + f)),
        pl.BlockSpec((ft, cbw), lambda b, c, f: (f, c)),
        pl.BlockSpec((ft, cbw), lambda b, c, f: (f, c)),
    ]
    return pl.pallas_call(
        functools.partial(_hyena_kernel, L, ft, nf),
        grid=(B, ncb, nf),
        in_specs=in_specs,
        out_specs=pl.BlockSpec((None, L, cbw), lambda b, c, f: (b, 0, c)),
        out_shape=jax.ShapeDtypeStruct((B, L, C), BF16),
        scratch_shapes=[pltpu.VMEM((L, cbw), BF16), pltpu.VMEM((L, cbw), F32)],
        compiler_params=_params(("parallel", "parallel", "arbitrary"), 56),
        name="hyena_mix",
    )(x0, x1, v, cw, cb, wf, wf, wft, wft, ka, kb)


def _mlstm_kernel(S, T, q_ref, k_ref, v_ref, og_ref, gcol_ref, grow_ref, cwq_ref, cwk_ref, ng_ref, o_ref,
                  k_s, qt_s, vat_s, hf_s, hb_s, na_s, dc_s, ml_s):
    d = ML_HEAD_DIM
    lc = ML_BLOCK
    nc = T // lc
    ncl = S // lc
    row = lax.broadcasted_iota(jnp.int32, (T, 1), 0)
    first = (row == 0) | (row == S)
    last = (row == S - 1) | (row == T - 1)
    qc = _silu(_conv3_rows(q_ref[...].astype(F32), cwq_ref[...], first, last)) * (d ** -0.5)
    k_s[...] = _silu(_conv3_rows(k_ref[...].astype(F32), cwk_ref[...], first, last)).astype(BF16)
    ones_row = (lax.broadcasted_iota(jnp.int32, (d, lc), 0) == 0).astype(F32)
    for c in range(nc):
        qt_s[c] = qc[c * lc:(c + 1) * lc, :].T.astype(BF16)
        vt = v_ref[c * lc:(c + 1) * lc, :].astype(F32).T
        vat_s[c] = jnp.concatenate([vt, ones_row], axis=0).astype(BF16)

    si = lax.broadcasted_iota(jnp.int32, (lc, lc), 0)
    ti = lax.broadcasted_iota(jnp.int32, (lc, lc), 1)
    past = si <= ti
    future = si >= ti

    scans = ((0, past, lc - 1), (2, future, 0))

    b_tot = [[None] * nc for _ in scans]
    g_max = [[None] * nc for _ in scans]
    for c in range(nc):
        k = k_s[c * lc:(c + 1) * lc, :]
        vat = vat_s[c]
        st = _bdot(k, qt_s[c])
        vatf = vat.astype(F32)
        gc = gcol_ref[c]
        gr = grow_ref[c]
        for dn, (kind, mask, end) in enumerate(scans):
            r_col = gc[:, kind:kind + 1] - gc[:, kind + 1:kind + 2]
            b_row = gr[kind + 1:kind + 2, :]
            r_row = gr[kind:kind + 1, :] - b_row
            dlog = jnp.where(mask, r_col + b_row, -jnp.inf)
            m_loc = jnp.max(dlog, axis=0, keepdims=True)
            na_s[dn, c] = _bdot(vat, (st * jnp.exp(dlog - m_loc)).astype(BF16))
            ml_s[dn, c] = m_loc
            b_tot[dn][c] = b_row[:, end:end + 1]
            gs_row = b_tot[dn][c] + r_row
            g_max[dn][c] = jnp.max(gs_row, axis=1, keepdims=True)
            dc_s[dn, c] = _bdot((vatf * jnp.exp(gs_row - g_max[dn][c])).astype(BF16), k)

    def advance(dn, c, cs, m_prev):
        kind = scans[dn][0]
        m_loc = ml_s[dn, c]
        inter = grow_ref[c][kind + 1:kind + 2, :] + m_prev
        m_t = jnp.maximum(inter, m_loc)
        e_in = jnp.exp(m_loc - m_t)
        e_st = jnp.exp(inter - m_t)
        n_in = na_s[dn, c]
        n_st = _bdot(cs.astype(BF16), qt_s[c])
        den = e_in * n_in[d:d + 1, :] + e_st * n_st[d:d + 1, :]
        inv = 1.0 / jnp.maximum(jnp.abs(den), jnp.exp(-m_t))
        ht = (e_in * inv) * n_in[:d, :] + (e_st * inv) * n_st[:d, :]
        m_new = jnp.maximum(b_tot[dn][c] + m_prev, g_max[dn][c])
        cs = jnp.exp(b_tot[dn][c] + m_prev - m_new) * cs + jnp.exp(g_max[dn][c] - m_new) * dc_s[dn, c]
        return ht, cs, m_new

    c_f = c_b = jnp.zeros((2 * d, d), F32)
    m_f = m_b = jnp.zeros((1, 1), F32)
    for i in range(nc):
        cf, cb = (i + ncl) % nc, nc - 1 - i
        hf_s[cf], c_f, m_f = advance(0, cf, c_f, m_f)
        hb_s[cb], c_b, m_b = advance(1, cb, c_b, m_b)

    for c in range(nc):
        h = (hf_s[c] + hb_s[c]).T
        og = og_ref[c * lc:(c + 1) * lc, :].astype(F32)
        o_ref[c * lc:(c + 1) * lc, :] = ((_rms(h) * ng_ref[...]) * _sigmoid(og)).astype(o_ref.dtype)


def _mlstm_mix(q, k, v, og, gcol, grow, cwq, cwk, ng, S):
    B, T, W = q.shape
    H = W // ML_HEAD_DIM
    d = ML_HEAD_DIM
    nc = T // ML_BLOCK
    head = pl.BlockSpec((None, T, d), lambda b, h: (b, 0, h))
    return pl.pallas_call(
        functools.partial(_mlstm_kernel, S, T),
        grid=(B, H),
        in_specs=[head, head, head, head,
                  pl.BlockSpec((None, None, nc, ML_BLOCK, 4), lambda b, h: (b, h, 0, 0, 0)),
                  pl.BlockSpec((None, None, nc, 4, ML_BLOCK), lambda b, h: (b, h, 0, 0, 0)),
                  pl.BlockSpec((SHORT_W, d), lambda b, h: (0, h)),
                  pl.BlockSpec((SHORT_W, d), lambda b, h: (0, h)),
                  pl.BlockSpec((1, d), lambda b, h: (0, h))],
        out_specs=head,
        out_shape=jax.ShapeDtypeStruct((B, T, W), BF16),
        scratch_shapes=[pltpu.VMEM((T, d), BF16), pltpu.VMEM((nc, d, ML_BLOCK), BF16),
                        pltpu.VMEM((nc, 2 * d, ML_BLOCK), BF16),
                        pltpu.VMEM((nc, d, ML_BLOCK), F32), pltpu.VMEM((nc, d, ML_BLOCK), F32),
                        pltpu.VMEM((2, nc, 2 * d, ML_BLOCK), F32), pltpu.VMEM((2, nc, 2 * d, d), F32),
                        pltpu.VMEM((2, nc, 1, ML_BLOCK), F32)],
        compiler_params=_params(("parallel", "parallel"), 48),
        name="mlstm_mix",
    )(q, k, v, og, gcol, grow, cwq, cwk, ng)


FETCH = 16
HALO = 8
FFN_SPLIT = 2


def _halo_specs(a, n_lat_tiles):
    fb = TM // FETCH
    if isinstance(a, tuple):
        lat, cx = a
        assert cx.shape[1] == TM
        w, nfb = lat.shape[2], lat.shape[1] // FETCH
        li = lambda i: jnp.minimum(i, n_lat_tiles - 1)
        return ([pl.BlockSpec((None, TM, w), lambda b, i: (b, li(i), 0)),
                 pl.BlockSpec((None, FETCH, w), lambda b, i: (b, jnp.maximum(li(i) * fb - 1, 0), 0)),
                 pl.BlockSpec((None, FETCH, w), lambda b, i: (b, jnp.minimum((li(i) + 1) * fb, nfb - 1), 0)),
                 pl.BlockSpec((None, TM, w), lambda b, i: (b, 0, 0))], [lat, lat, lat, cx])
    w, nfb = a.shape[2], a.shape[1] // FETCH
    return ([pl.BlockSpec((None, TM, w), lambda b, i: (b, i, 0)),
             pl.BlockSpec((None, FETCH, w), lambda b, i: (b, jnp.maximum(i * fb - 1, 0), 0)),
             pl.BlockSpec((None, FETCH, w), lambda b, i: (b, jnp.minimum((i + 1) * fb, nfb - 1), 0))], [a, a, a])


def _load_halo(refs, n_lat_tiles):
    main = refs[0][...]
    if len(refs) == 4:
        main = jnp.where(pl.program_id(1) >= n_lat_tiles, refs[3][...], main)
    return jnp.concatenate([refs[1][...], main, refs[2][...]], axis=0)


def _tail_kernel(final, n_lat_tiles, counts, mod_ref, *refs):
    groups = []
    for n in counts:
        groups.append(refs[:n])
        refs = refs[n:]
    n_act = len(counts) - 1
    wo_refs = refs[:n_act]
    g_ref, wg_ref, wu_ref, cw_ref, cb_ref, wd_ref, fn_ref, o_ref = refs[n_act:]
    i = pl.program_id(1)
    nt = pl.num_programs(1)
    mod = mod_ref[...]
    y = None
    for a_refs, w_ref in zip(groups[1:], wo_refs):
        t = _bdot(_load_halo(a_refs, n_lat_tiles).astype(BF16), w_ref[...])
        y = t if y is None else y + t
    xe = _load_halo(groups[0], n_lat_tiles).astype(F32) + mod[2:3, :] * y
    x = xe[FETCH:FETCH + TM]
    hf = _modnorm(xe[FETCH - HALO:FETCH + TM + HALO], mod, g_ref[...], 3)
    he = hf.astype(BF16)
    h = hf[HALO:HALO + TM].astype(BF16)
    has_prev = jnp.logical_and(i != 0, i != n_lat_tiles)
    has_next = jnp.logical_and(i != n_lat_tiles - 1, i != nt - 1)
    row = lax.broadcasted_iota(jnp.int32, (TM + 2 * HALO, 1), 0)
    inside = jnp.logical_and(jnp.logical_or(row >= HALO, has_prev), jnp.logical_or(row < TM + HALO, has_next))
    fh = wg_ref.shape[1]
    fc = fh // FFN_SPLIT
    y = None
    for c in range(FFN_SPLIT):
        cols = slice(c * fc, (c + 1) * fc)
        ge = jnp.where(inside, _bdot(he, wg_ref[:, cols]), 0.0)
        gm = pltpu.roll(ge, 1, 0)[HALO:HALO + TM]
        gq = pltpu.roll(ge, TM + 2 * HALO - 1, 0)[HALO:HALO + TM]
        cw = cw_ref[:, cols]
        gc = cw[0:1, :] * gm + cw[1:2, :] * ge[HALO:HALO + TM] + cw[2:3, :] * gq + cb_ref[:, cols]
        a = (_silu(gc) * _bdot(h, wu_ref[:, cols])).astype(BF16)
        t = _bdot(a, wd_ref[cols, :])
        y = t if y is None else y + t
    o = x + mod[5:6, :] * y
    if final:
        o = _rms(o) * fn_ref[...]
    o_ref[...] = o


def _mixer_tail(x, acts, ws_out, mod, norm_g, w_up, cw, cb, w_down, fn, n_rows, n_lat_tiles, final):
    B, _, _, D = mod.shape
    Fh = w_up.shape[1] // 2
    wg = w_up[:, :Fh].astype(BF16)
    wu = w_up[:, Fh:].astype(BF16)
    nt = n_rows // TM
    once = lambda shape: pl.BlockSpec(shape, lambda b, i: (0, 0), pipeline_mode=pl.Buffered(1))
    in_specs = [pl.BlockSpec((None, None, 6, D), lambda b, i: (b, (i >= n_lat_tiles).astype(jnp.int32), 0, 0))]
    args = [mod]
    counts = []
    for a in [x] + list(acts):
        sp, ar = _halo_specs(a, n_lat_tiles)
        in_specs += sp
        args += ar
        counts.append(len(ar))
    in_specs += [once(w.shape) for w in ws_out]
    in_specs += [once((1, D)), once((D, Fh)), once((D, Fh)), once((SHORT_W, Fh)), once((1, Fh)), once((Fh, D)),
                 once((1, D))]
    return pl.pallas_call(
        functools.partial(_tail_kernel, final, n_lat_tiles, tuple(counts)),
        grid=(B, nt), in_specs=in_specs,
        out_specs=pl.BlockSpec((None, TM, D), lambda b, i: (b, i, 0)),
        out_shape=jax.ShapeDtypeStruct((B, n_rows, D), F32),
        compiler_params=_params(("parallel", "arbitrary"), 56),
        name="mixer_tail",
    )(*args, *ws_out, norm_g.reshape(1, D), wg, wu, cw.astype(F32), cb.reshape(1, Fh).astype(F32),
      w_down.astype(BF16), fn.reshape(1, D))


def _rope_tables(S, T):
    p = np.arange(S)
    inv = ROPE_BASE ** (-np.arange(0, ROPE_AXIS, 2, dtype=np.float64) / ROPE_AXIS)
    ar = (p // GRID_W)[:, None] * inv[None, :]
    ac = (p % GRID_W)[:, None] * inv[None, :]
    cos = np.concatenate([np.cos(ar), np.cos(ar), np.cos(ac), np.cos(ac)], axis=1)
    sin = np.concatenate([-np.sin(ar), np.sin(ar), -np.sin(ac), np.sin(ac)], axis=1)
    cos = np.concatenate([cos, np.ones((T - S, MLA_ROPE))], axis=0)
    sin = np.concatenate([sin, np.zeros((T - S, MLA_ROPE))], axis=0)
    return cos.astype(np.float32), sin.astype(np.float32)


def _mla_proj_kernel(x_ref, mod_ref, g_ref, wdn_ref, qn_ref, kvn_ref, wuq_ref, wukv_ref, cos_ref, sin_ref,
                     qo_ref, ko_ref, vo_ref):
    H, dn, dr = MLA_HEADS, MLA_NOPE, MLA_ROPE
    h = _modnorm(x_ref[...], mod_ref[...], g_ref[...], 0).astype(BF16)
    dnp = _bdot(h, wdn_ref[...])
    qa = dnp[:, :MLA_Q_RANK]
    kva = dnp[:, MLA_Q_RANK:MLA_Q_RANK + MLA_KV_RANK]
    kr2 = dnp[:, MLA_Q_RANK + MLA_KV_RANK:]
    q = _bdot((_rms(qa) * qn_ref[...]).astype(BF16), wuq_ref[...])
    kv = _bdot((_rms(kva) * kvn_ref[...]).astype(BF16), wukv_ref[...])
    cosr = cos_ref[...]
    sinr = sin_ref[...]
    kr = kr2[:, :dr] * cosr + kr2[:, dr:] * sinr
    zpad = jnp.zeros((kr.shape[0], MLA_DK_PAD - dn - dr), F32)
    ro = H * dn
    for hd in range(H):
        qr = q[:, ro + hd * dr:ro + (hd + 1) * dr] * cosr + q[:, ro + H * dr + hd * dr:ro + H * dr + (hd + 1) * dr] * sinr
        qh = jnp.concatenate([q[:, hd * dn:(hd + 1) * dn], qr, zpad], axis=1) * (MLA_SCALE * math.log2(math.e))
        qo_ref[hd] = qh.astype(BF16)
        ko_ref[hd] = jnp.concatenate([kv[:, hd * dn:(hd + 1) * dn], kr, zpad], axis=1).astype(BF16)
        vo_ref[hd] = kv[:, ro + hd * MLA_V:ro + (hd + 1) * MLA_V].astype(BF16)


def _mla_proj(x, mod, norm_g, w_down, q_norm, kv_norm, w_uq, w_ukv, S):
    B, T, D = x.shape
    H, dn, dr, dv = MLA_HEADS, MLA_NOPE, MLA_ROPE, MLA_V
    dk = dn + dr
    nt = T // TM
    n_lat_tiles = S // TM
    swap = np.arange(dr) ^ (ROPE_AXIS // 2)
    wkr = w_down[:, MLA_Q_RANK + MLA_KV_RANK:]
    wdn = jnp.concatenate([w_down, wkr[:, swap]], axis=1).astype(BF16)
    wq = w_uq.reshape(MLA_Q_RANK, H, dk)
    wq_n = wq[:, :, :dn].reshape(MLA_Q_RANK, H * dn)
    wq_r = wq[:, :, dn:]
    wuq = jnp.concatenate([wq_n, wq_r.reshape(MLA_Q_RANK, H * dr), wq_r[:, :, swap].reshape(MLA_Q_RANK, H * dr)],
                          axis=1).astype(BF16)
    wkv = w_ukv.reshape(MLA_KV_RANK, H, dn + dv)
    wukv = jnp.concatenate([wkv[:, :, :dn].reshape(MLA_KV_RANK, H * dn), wkv[:, :, dn:].reshape(MLA_KV_RANK, H * dv)],
                           axis=1).astype(BF16)
    cos, sin = _rope_tables(S, T)
    full = lambda a: pl.BlockSpec(a.shape, lambda b, i: (0,) * a.ndim)
    in_specs = [pl.BlockSpec((None, TM, D), lambda b, i: (b, i, 0)),
                pl.BlockSpec((None, None, 6, D), lambda b, i: (b, (i >= n_lat_tiles).astype(jnp.int32), 0, 0)),
                pl.BlockSpec((1, D), lambda b, i: (0, 0)),
                full(wdn), pl.BlockSpec((1, MLA_Q_RANK), lambda b, i: (0, 0)),
                pl.BlockSpec((1, MLA_KV_RANK), lambda b, i: (0, 0)), full(wuq), full(wukv),
                pl.BlockSpec((TM, dr), lambda b, i: (i, 0)), pl.BlockSpec((TM, dr), lambda b, i: (i, 0))]
    headed = lambda w: pl.BlockSpec((None, H, TM, w), lambda b, i: (b, 0, i, 0))
    return pl.pallas_call(
        _mla_proj_kernel,
        grid=(B, nt), in_specs=in_specs,
        out_specs=[headed(MLA_DK_PAD), headed(MLA_DK_PAD), headed(dv)],
        out_shape=[jax.ShapeDtypeStruct((B, H, T, MLA_DK_PAD), BF16), jax.ShapeDtypeStruct((B, H, T, MLA_DK_PAD), BF16),
                   jax.ShapeDtypeStruct((B, H, T, dv), BF16)],
        compiler_params=_params(("parallel", "arbitrary"), 40),
        name="mla_proj",
    )(x, mod, norm_g.reshape(1, D), wdn, q_norm.reshape(1, -1), kv_norm.reshape(1, -1), wuq, wukv,
      jnp.asarray(cos), jnp.asarray(sin))


ATT_HEADS_PER_STEP = 4
ATT_KEY_SLAB = 768


def _attn_kernel(q_ref, k_ref, v_ref, o_ref, va_s):
    hp, T, dv = v_ref.shape

    @pl.when(pl.program_id(2) == 0)
    def _():
        ones_col = (lax.broadcasted_iota(jnp.int32, (T, dv), 1) == 0).astype(BF16)
        for j in range(hp):
            va_s[j] = jnp.concatenate([v_ref[j], ones_col], axis=1)

    kw = max(w for w in range(LANES, ATT_KEY_SLAB + 1, LANES) if T % w == 0)
    slabs = [slice(c * kw, (c + 1) * kw) for c in range(T // kw)]
    nt = (((1,), (1,)), ((), ()))
    s = [[lax.dot_general(q_ref[j], k_ref[j, sl, :], nt, preferred_element_type=F32) for sl in slabs]
         for j in range(hp)]
    for j in range(hp):
        m = functools.reduce(jnp.maximum, [jnp.max(sc, axis=-1, keepdims=True) for sc in s[j]])
        na = None
        for sc, sl in zip(s[j], slabs):
            t = _bdot(jnp.exp2(sc - m).astype(BF16), va_s[j, sl, :])
            na = t if na is None else na + t
        o_ref[:, j * dv:(j + 1) * dv] = (na[:, :dv] * (1.0 / na[:, dv:dv + 1])).astype(o_ref.dtype)


def _attention(q, k, v, S):
    B, H, T, dk = k.shape
    dv = v.shape[3]
    tq = min(S, 512)
    hp = ATT_HEADS_PER_STEP
    return pl.pallas_call(
        _attn_kernel,
        grid=(B, H // hp, S // tq),
        in_specs=[pl.BlockSpec((None, hp, tq, dk), lambda b, h, i: (b, h, i, 0)),
                  pl.BlockSpec((None, hp, T, dk), lambda b, h, i: (b, h, 0, 0)),
                  pl.BlockSpec((None, hp, T, dv), lambda b, h, i: (b, h, 0, 0))],
        out_specs=pl.BlockSpec((None, tq, hp * dv), lambda b, h, i: (b, i, h)),
        out_shape=jax.ShapeDtypeStruct((B, S, H * dv), BF16),
        scratch_shapes=[pltpu.VMEM((hp, T, 2 * dv), BF16)],
        compiler_params=_params(("parallel", "parallel", "arbitrary"), 48),
        name="mla_attention",
    )(q, k, v)


def kernel(x, c, ctx, c_ctx, ada_w_0, ada_b_0, norm_mix_0, norm_ffn_0, w_in_0, hy_conv_w, hy_conv_b, hy_fw1, hy_fb1, hy_fw2, hy_fb2, hy_fw3, hy_freq, hy_bias, ml_conv_w, ml_gate_b, ml_norm_g, w_out_0, ffn_up_0, ffn_conv_w_0, ffn_conv_b_0, ffn_down_0, ada_w_1, ada_b_1, norm_mix_1, norm_ffn_1, mla_w_down, mla_q_norm, mla_kv_norm, mla_w_uq, mla_w_ukv, mla_w_o, ffn_up_1, ffn_conv_w_1, ffn_conv_b_1, ffn_down_1, final_norm):
    B, S, D = x.shape
    CL = ctx.shape[1]
    T = S + CL
    assert S % TM == 0 and CL % TM == 0 and S % CL == 0 and S % GRID_W == 0 and TM == ML_BLOCK
    nlt = S // TM
    C = HY_WIDTH
    W = ML_WIDTH
    H = ML_HEADS

    mod0 = _modulation(c, c_ctx, ada_w_0, ada_b_0)
    mod1 = _modulation(c, c_ctx, ada_w_1, ada_b_1)

    o = 3 * C
    wi = w_in_0.astype(BF16)
    w_gate = jnp.zeros((D, LANES), BF16).at[:, :4 * H].set(wi[:, o + 4 * W:])
    b_gate = jnp.zeros((1, LANES), F32).at[0, :4 * H].set(ml_gate_b.astype(F32))
    ws0 = [wi[:, 0:C], wi[:, C:2 * C], wi[:, 2 * C:3 * C], wi[:, o:o + W], wi[:, o + W:o + 2 * W],
           wi[:, o + 2 * W:o + 3 * W], wi[:, o + 3 * W:o + 4 * W], w_gate]
    hx0, hx1, hv, mq, mk, mv, mog, gates = _proj((x, ctx), mod0, norm_mix_0, ws0, 0, T, nlt, [BF16] * 7 + [F32],
                                                 bias=b_gate)

    filt = (hy_fw1, hy_fb1, hy_fw2, hy_fb2, hy_fw3, hy_freq, hy_bias)
    hcw = jnp.transpose(hy_conv_w.reshape(SHORT_W, 3, C), (1, 0, 2)).astype(F32)
    hcb = hy_conv_b.reshape(3, C).astype(F32)
    hy = []
    for L, blk in ((S, 0), (CL, S // CL)):
        wf_np, wft_np = _dft_tables(L)
        wf = jnp.asarray(wf_np).astype(BF16)
        wft = jnp.asarray(wft_np).astype(BF16)
        taps = _hyena_filter_taps(L, *filt)
        ka, kb = _hyena_filter_spectrum(L, min(L, 512), wf, taps)
        hy.append(_hyena_mix(hx0, hx1, hv, hcw, hcb, wf, wft, ka, kb, L, blk))

    nc = T // ML_BLOCK
    g4 = gates[:, :, :4 * H].reshape(B, nc, ML_BLOCK, 4, H)
    gcol = jnp.transpose(g4, (0, 4, 1, 2, 3))
    grow = jnp.transpose(g4, (0, 4, 1, 3, 2))
    mcw = ml_conv_w.astype(F32)
    ml = _mlstm_mix(mq, mk, mv, mog, gcol, grow, mcw[:, :W], mcw[:, W:], ml_norm_g.reshape(1, W).astype(F32), S)

    wo = w_out_0.astype(BF16)
    xs = _mixer_tail((x, ctx), [tuple(hy), ml], [wo[:C], wo[C:]], mod0, norm_ffn_0, ffn_up_0, ffn_conv_w_0,
                     ffn_conv_b_0, ffn_down_0, final_norm, T, nlt, False)

    q, k, v = _mla_proj(xs, mod1, norm_mix_1, mla_w_down, mla_q_norm, mla_kv_norm, mla_w_uq, mla_w_ukv, S)
    att = _attention(q, k, v, S)
    return _mixer_tail(xs, [att], [mla_w_o.astype(BF16)], mod1, norm_ffn_1, ffn_up_1, ffn_conv_w_1, ffn_conv_b_1,
                       ffn_down_1, final_norm, S, nlt, True)
```

```python
import functools
import math

import numpy as np
import jax
import jax.numpy as jnp
from jax import lax
from jax.experimental import pallas as pl
from jax.experimental.pallas import tpu as pltpu

F32 = jnp.float32
BF16 = jnp.bfloat16
HI = lax.Precision.HIGHEST

RMS_EPS = 1e-6
GRID_W = 64
SHORT_W = 3
HY_WIDTH = 512
HY_EMB = 33
HY_FAST = 0.3
HY_SLOW = 1.5
HY_TARGET = 1e-2
HY_SHIFT = 0.05
ML_HEADS = 4
ML_HEAD_DIM = 128
ML_WIDTH = ML_HEADS * ML_HEAD_DIM
ML_BLOCK = 256
MLA_HEADS = 8
MLA_NOPE = 128
MLA_ROPE = 64
MLA_V = 128
MLA_Q_RANK = 384
MLA_KV_RANK = 256
MLA_SCALE = (MLA_NOPE + MLA_ROPE) ** -0.5
MLA_DK_PAD = 256
ROPE_AXIS = MLA_ROPE // 2
ROPE_BASE = 10000.0

TM = 256
LANES = 128
MIB = 1024 * 1024


def _params(sem, vmem_mib):
    return pltpu.CompilerParams(dimension_semantics=sem, vmem_limit_bytes=vmem_mib * MIB)


def _sigmoid(x):
    return 0.5 * jnp.tanh(0.5 * x) + 0.5


def _silu(x):
    return x * _sigmoid(x)


def _log_sigmoid(x):
    return jnp.minimum(x, 0.0) - jnp.log(1.0 + jnp.exp(-jnp.abs(x)))


def _rms(x):
    return x * lax.rsqrt(jnp.mean(x * x, axis=-1, keepdims=True) + RMS_EPS)


def _bdot(a, b):
    return jnp.dot(a, b, preferred_element_type=F32)


def _ada_kernel(cv_ref, w_ref, b_ref, o_ref):
    s = _silu(cv_ref[...])
    o_ref[...] = jnp.dot(s, w_ref[...], preferred_element_type=F32, precision=HI) + b_ref[...]


def _ada(cv, w, b):
    R, D = cv.shape
    N = w.shape[1]
    tn = N // 4
    return pl.pallas_call(
        _ada_kernel,
        grid=(N // tn,),
        in_specs=[pl.BlockSpec((R, D), lambda j: (0, 0)),
                  pl.BlockSpec((D, tn), lambda j: (0, j)),
                  pl.BlockSpec((1, tn), lambda j: (0, j))],
        out_specs=pl.BlockSpec((R, tn), lambda j: (0, j)),
        out_shape=jax.ShapeDtypeStruct((R, N), F32),
        compiler_params=_params(("arbitrary",), 40),
        name="ada_mod",
    )(cv, w, b.reshape(1, N))


def _modulation(c, c_ctx, w, b):
    B, D = c.shape
    R = -(-(B + 1) // 8) * 8
    cv = jnp.concatenate([c, c_ctx[None, :], jnp.zeros((R - B - 1, D), F32)], axis=0)
    m = _ada(cv, w, b)
    lat = m[:B].reshape(B, 1, 6, D)
    cx = jnp.broadcast_to(m[B].reshape(1, 1, 6, D), (B, 1, 6, D))
    return jnp.concatenate([lat, cx], axis=1)


def _modnorm(x, mod, g, si):
    sh = mod[si:si + 1, :]
    sc = mod[si + 1:si + 2, :]
    return (_rms(x) * g) * (1.0 + sc) + sh


def _row_specs(a, n_lat_tiles):
    if isinstance(a, tuple):
        w = a[0].shape[2]
        return ([pl.BlockSpec((None, TM, w), lambda b, i: (b, jnp.minimum(i, n_lat_tiles - 1), 0)),
                 pl.BlockSpec((None, TM, w), lambda b, i: (b, jnp.maximum(i - n_lat_tiles, 0), 0))], list(a))
    return [pl.BlockSpec((None, TM, a.shape[2]), lambda b, i: (b, i, 0))], [a]


def _load_rows(refs, n_lat_tiles):
    if len(refs) == 1:
        return refs[0][...]
    return jnp.where(pl.program_id(1) >= n_lat_tiles, refs[1][...], refs[0][...])


def _proj_kernel(n_x, n_out, si, has_bias, n_lat_tiles, *refs):
    x_refs = refs[:n_x]
    mod_ref, g_ref = refs[n_x:n_x + 2]
    refs = refs[n_x + 2:]
    w_refs = refs[:n_out]
    nb = 1 if has_bias else 0
    o_refs = refs[n_out + nb:]
    h = _modnorm(_load_rows(x_refs, n_lat_tiles), mod_ref[...], g_ref[...], si).astype(BF16)
    for j in range(n_out):
        y = _bdot(h, w_refs[j][...])
        if has_bias and j == n_out - 1:
            y = _mlstm_gate_prep(y + refs[n_out][...])
        o_refs[j][...] = y.astype(o_refs[j].dtype)


def _mlstm_gate_prep(g):
    n = g.shape[0]
    H = ML_HEADS
    lf = _log_sigmoid(g)
    t = lax.broadcasted_iota(jnp.int32, (n, n), 0)
    s = lax.broadcasted_iota(jnp.int32, (n, n), 1)
    hi = lf.astype(BF16)
    r1 = lf - hi.astype(F32)
    mid = r1.astype(BF16)
    parts = jnp.concatenate([hi, mid, (r1 - mid.astype(F32)).astype(BF16)], axis=1)
    w = g.shape[1]
    pre3 = _bdot((s <= t).astype(BF16), parts)
    suf3 = _bdot((s >= t).astype(BF16), parts)
    pre = pre3[:, :w] + pre3[:, w:2 * w] + pre3[:, 2 * w:]
    suf = suf3[:, :w] + suf3[:, w:2 * w] + suf3[:, 2 * w:]
    col = lax.broadcasted_iota(jnp.int32, (1, g.shape[1]), 1)
    g = jnp.where((col >= H) & (col < 2 * H), pre, g)
    return jnp.where((col >= 3 * H) & (col < 4 * H), suf, g)


def _proj(x, mod, g, ws, si, n_rows, n_lat_tiles, out_dtypes, bias=None, vmem=48):
    B, _, _, D = mod.shape
    nt = n_rows // TM
    in_specs, args = _row_specs(x, n_lat_tiles)
    n_x = len(args)
    in_specs += [pl.BlockSpec((None, None, 6, D), lambda b, i: (b, (i >= n_lat_tiles).astype(jnp.int32), 0, 0)),
                 pl.BlockSpec((1, D), lambda b, i: (0, 0))]
    args += [mod, g.reshape(1, D)]
    for w in ws:
        in_specs.append(pl.BlockSpec(w.shape, lambda b, i: (0, 0)))
        args.append(w)
    if bias is not None:
        in_specs.append(pl.BlockSpec(bias.shape, lambda b, i: (0, 0)))
        args.append(bias)
    out_specs = [pl.BlockSpec((None, TM, w.shape[1]), lambda b, i: (b, i, 0)) for w in ws]
    out_shape = [jax.ShapeDtypeStruct((B, n_rows, w.shape[1]), dt) for w, dt in zip(ws, out_dtypes)]
    return pl.pallas_call(
        functools.partial(_proj_kernel, n_x, len(ws), si, bias is not None, n_lat_tiles),
        grid=(B, nt), in_specs=in_specs, out_specs=out_specs, out_shape=out_shape,
        compiler_params=_params(("parallel", "arbitrary"), vmem),
        name="modnorm_proj",
    )(*args)


def _filter_tables(L):
    pos = np.arange(L, dtype=np.float64)
    t = pos / (L - 1)
    bands = (HY_EMB - 1) // 2
    f = np.linspace(1e-4, bands - 1, bands)
    ang = (2.0 * math.pi / L) * pos[:, None] * f[None, :]
    z = np.concatenate([t[:, None], np.cos(ang), -np.sin(ang)], axis=-1)
    zp = np.zeros((L, LANES), np.float64)
    zp[:, :HY_EMB] = z
    return zp.astype(np.float32)


def _filt_mlp_kernel(tl, z_ref, w1, b1, w2, b2, w3, fq, absd, bias_ref, o_ref):
    i = pl.program_id(0)
    z = z_ref[...]
    f = fq[...]
    h = jnp.sin(f * (jnp.dot(z, w1[...], preferred_element_type=F32, precision=HI) + b1[...]))
    h = jnp.sin(f * (jnp.dot(h, w2[...], preferred_element_type=F32, precision=HI) + b2[...]))
    h = jnp.dot(h, w3[...], preferred_element_type=F32, precision=HI)
    t = z[:, 0:1]
    h = h * (jnp.exp(-t * absd[...]) + HY_SHIFT)
    row = lax.broadcasted_iota(jnp.int32, (tl, 1), 0) + i * tl
    col = lax.broadcasted_iota(jnp.int32, (1, 2 * HY_WIDTH), 1)
    h = jnp.where(row == 0, jnp.where(col < HY_WIDTH, h + bias_ref[...], 0.0), h)
    o_ref[...] = h


def _pad2(a, r, c):
    return jnp.zeros((r, c), F32).at[:a.shape[0], :a.shape[1]].set(a.astype(F32))


def _hyena_filter_taps(L, fw1, fb1, fw2, fb2, fw3, freq, hy_bias):
    tl = min(L, 512)
    deltas = np.linspace(math.log(HY_TARGET) / HY_FAST, math.log(HY_TARGET) / HY_SLOW, HY_WIDTH)
    absd = jnp.asarray(np.abs(np.tile(deltas, 2))[None, :].astype(np.float32))
    z = jnp.asarray(_filter_tables(L))
    C2 = 2 * HY_WIDTH
    bias_full = jnp.concatenate([hy_bias.astype(F32), jnp.zeros((HY_WIDTH,), F32)]).reshape(1, C2)
    full = lambda shape: pl.BlockSpec(shape, lambda i: (0, 0))
    return pl.pallas_call(
        functools.partial(_filt_mlp_kernel, tl),
        grid=(L // tl,),
        in_specs=[pl.BlockSpec((tl, LANES), lambda i: (i, 0)),
                  full((LANES, LANES)), full((1, LANES)), full((LANES, LANES)), full((1, LANES)),
                  full((LANES, C2)), full((1, LANES)), full((1, C2)), full((1, C2))],
        out_specs=pl.BlockSpec((tl, C2), lambda i: (i, 0)),
        out_shape=jax.ShapeDtypeStruct((L, C2), F32),
        compiler_params=_params(("arbitrary",), 32),
        name="hyena_filter_mlp",
    )(z, _pad2(fw1, LANES, LANES), _pad2(fb1[None, :], 1, LANES), _pad2(fw2, LANES, LANES),
      _pad2(fb2[None, :], 1, LANES), _pad2(fw3, LANES, C2), _pad2(freq[None, :], 1, LANES), absd, bias_full)


def _dft_tables(L):
    n = 2 * L
    f = np.arange(L, dtype=np.int64)[:, None]
    s = np.arange(L, dtype=np.int64)[None, :]
    ang = (2.0 * math.pi / n) * ((f * s) % n).astype(np.float64)
    wc = np.cos(ang)
    ws = np.sin(ang)
    ws[0, :] = np.where(np.arange(L) % 2 == 0, 1.0, -1.0)
    w = np.concatenate([wc, ws], axis=0).astype(np.float32)
    return w, np.ascontiguousarray(w.T)


def _hyena_radix(L):
    return max(r for r in (4, 2, 1) if L % r == 0 and L // r >= 256)


def _filt_dft_kernel(M, wf_ref, k1_ref, k0_ref, p_ref, q_ref, r_ref):
    row = lax.broadcasted_iota(jnp.int32, (M, 1), 0)
    row0 = row == 0
    wf = wf_ref[...]
    t1 = _bdot(wf, k1_ref[...].astype(BF16))
    t0 = _bdot(wf, jnp.where(row0, 0.0, k0_ref[...]).astype(BF16))
    sgn = jnp.where((row & 1) == 1, -1.0, 1.0)
    sc = jnp.where(row0, 0.5 / M, 1.0 / M)
    ka = (t1[:M] + sgn * t0[:M]) * sc
    kb = (t1[M:] + sgn * t0[M:]) * sc
    p_ref[...] = ka
    q_ref[...] = jnp.where(row0, 0.0, kb)
    r_ref[...] = jnp.where(row0, kb, ka)


def _hyena_filter_spectrum(L, R, wf, taps):
    C = HY_WIDTH
    M = L // R
    lags = jnp.concatenate([jnp.zeros((1, C), F32), jnp.flip(taps[1:, C:], axis=0), taps[:, :C]], axis=0)
    lags = lags.reshape(2 * R, M, C)
    nd = 2 * R - 1
    out = pl.BlockSpec((None, M, C), lambda d: (d, 0, 0))
    return pl.pallas_call(
        functools.partial(_filt_dft_kernel, M),
        grid=(nd,),
        in_specs=[pl.BlockSpec((2 * M, M), lambda d: (0, 0)),
                  pl.BlockSpec((None, M, C), lambda d: (d + 1, 0, 0)),
                  pl.BlockSpec((None, M, C), lambda d: (d, 0, 0))],
        out_specs=[out, out, out],
        out_shape=[jax.ShapeDtypeStruct((nd, M, C), F32)] * 3,
        compiler_params=_params(("arbitrary",), 32),
        name="hyena_filter_dft",
    )(wf, lags, lags)


def _conv3_rows(x, w, first, last):
    n = x.shape[0]
    xm = jnp.where(first, 0.0, pltpu.roll(x, 1, 0))
    xp = jnp.where(last, 0.0, pltpu.roll(x, n - 1, 0))
    return w[0:1, :] * xm + w[1:2, :] * x + w[2:3, :] * xp


def _hyena_kernel(L, R, x0_ref, x1_ref, v_ref, cw_ref, cb_ref, wf_ref, wft_ref, p_ref, q_ref, r_ref, o_ref):
    M = L // R
    row = lax.broadcasted_iota(jnp.int32, (L, 1), 0)
    first = row == 0
    last = row == L - 1
    x1 = _conv3_rows(x1_ref[...].astype(F32), cw_ref[1], first, last) + cb_ref[1:2, :]
    v = _conv3_rows(v_ref[...].astype(F32), cw_ref[2], first, last) + cb_ref[2:3, :]
    g = (v * x1).astype(BF16)
    x0 = _conv3_rows(x0_ref[...].astype(F32), cw_ref[0], first, last) + cb_ref[0:1, :]
    wf = wf_ref[...]
    spec = [_bdot(wf, g[j * M:(j + 1) * M]) for j in range(R)]
    for i in range(R):
        yr = yb = None
        for j in range(R):
            a, b = spec[j][:M], spec[j][M:]
            d = i - j + R - 1
            p, q, r = p_ref[d], q_ref[d], r_ref[d]
            tr = a * p - b * q
            tb = a * q + b * r
            yr = tr if yr is None else yr + tr
            yb = tb if yb is None else yb + tb
        y = _bdot(wft_ref[...], jnp.concatenate([yr, yb], axis=0).astype(BF16))
        o_ref[i * M:(i + 1) * M, :] = (y * x0[i * M:(i + 1) * M]).astype(o_ref.dtype)


def _hyena_mix(x0, x1, v, cw, cb, wf, wft, spectra, L, R, row_block):
    B, T, C = x0.shape
    M = L // R
    cbw = 256
    nd = 2 * R - 1
    in_specs = [pl.BlockSpec((None, L, cbw), lambda c, b: (b, row_block, c))] * 3 + [
        pl.BlockSpec((3, SHORT_W, cbw), lambda c, b: (0, 0, c)),
        pl.BlockSpec((3, cbw), lambda c, b: (0, c)),
        pl.BlockSpec((2 * M, M), lambda c, b: (0, 0)),
        pl.BlockSpec((M, 2 * M), lambda c, b: (0, 0)),
    ] + [pl.BlockSpec((nd, M, cbw), lambda c, b: (0, 0, c))] * 3
    return pl.pallas_call(
        functools.partial(_hyena_kernel, L, R),
        grid=(C // cbw, B),
        in_specs=in_specs,
        out_specs=pl.BlockSpec((None, L, cbw), lambda c, b: (b, 0, c)),
        out_shape=jax.ShapeDtypeStruct((B, L, C), BF16),
        compiler_params=_params(("parallel", "arbitrary"), 56),
        name="hyena_mix",
    )(x0, x1, v, cw, cb, wf, wft, *spectra)


def _mlstm_kernel(S, T, q_ref, k_ref, v_ref, og_ref, gcol_ref, grow_ref, cwq_ref, cwk_ref, ng_ref, o_ref,
                  k_s, qt_s, vat_s, hf_s, hb_s, na_s, dc_s, ml_s):
    d = ML_HEAD_DIM
    lc = ML_BLOCK
    nc = T // lc
    ncl = S // lc
    row = lax.broadcasted_iota(jnp.int32, (T, 1), 0)
    first = (row == 0) | (row == S)
    last = (row == S - 1) | (row == T - 1)
    qc = _silu(_conv3_rows(q_ref[...].astype(F32), cwq_ref[...], first, last)) * (d ** -0.5)
    k_s[...] = _silu(_conv3_rows(k_ref[...].astype(F32), cwk_ref[...], first, last)).astype(BF16)
    ones_row = (lax.broadcasted_iota(jnp.int32, (d, lc), 0) == 0).astype(F32)
    for c in range(nc):
        qt_s[c] = qc[c * lc:(c + 1) * lc, :].T.astype(BF16)
        vt = v_ref[c * lc:(c + 1) * lc, :].astype(F32).T
        vat_s[c] = jnp.concatenate([vt, ones_row], axis=0).astype(BF16)

    si = lax.broadcasted_iota(jnp.int32, (lc, lc), 0)
    ti = lax.broadcasted_iota(jnp.int32, (lc, lc), 1)
    past = si <= ti
    future = si >= ti

    scans = ((0, past, lc - 1), (2, future, 0))

    b_tot = [[None] * nc for _ in scans]
    g_max = [[None] * nc for _ in scans]
    for c in range(nc):
        k = k_s[c * lc:(c + 1) * lc, :]
        vat = vat_s[c]
        st = _bdot(k, qt_s[c])
        vatf = vat.astype(F32)
        gc = gcol_ref[c]
        gr = grow_ref[c]
        for dn, (kind, mask, end) in enumerate(scans):
            r_col = gc[:, kind:kind + 1] - gc[:, kind + 1:kind + 2]
            b_row = gr[kind + 1:kind + 2, :]
            r_row = gr[kind:kind + 1, :] - b_row
            dlog = jnp.where(mask, r_col + b_row, -jnp.inf)
            m_loc = jnp.max(dlog, axis=0, keepdims=True)
            na_s[dn, c] = _bdot(vat, (st * jnp.exp(dlog - m_loc)).astype(BF16))
            ml_s[dn, c] = m_loc
            b_tot[dn][c] = b_row[:, end:end + 1]
            gs_row = b_tot[dn][c] + r_row
            g_max[dn][c] = jnp.max(gs_row, axis=1, keepdims=True)
            dc_s[dn, c] = _bdot((vatf * jnp.exp(gs_row - g_max[dn][c])).astype(BF16), k)

    def advance(dn, c, cs, m_prev):
        kind = scans[dn][0]
        m_loc = ml_s[dn, c]
        inter = grow_ref[c][kind + 1:kind + 2, :] + m_prev
        m_t = jnp.maximum(inter, m_loc)
        e_in = jnp.exp(m_loc - m_t)
        e_st = jnp.exp(inter - m_t)
        n_in = na_s[dn, c]
        n_st = _bdot(cs.astype(BF16), qt_s[c])
        den = e_in * n_in[d:d + 1, :] + e_st * n_st[d:d + 1, :]
        inv = 1.0 / jnp.maximum(jnp.abs(den), jnp.exp(-m_t))
        ht = (e_in * inv) * n_in[:d, :] + (e_st * inv) * n_st[:d, :]
        m_new = jnp.maximum(b_tot[dn][c] + m_prev, g_max[dn][c])
        cs = jnp.exp(b_tot[dn][c] + m_prev - m_new) * cs + jnp.exp(g_max[dn][c] - m_new) * dc_s[dn, c]
        return ht, cs, m_new

    c_f = c_b = jnp.zeros((2 * d, d), F32)
    m_f = m_b = jnp.zeros((1, 1), F32)
    for i in range(nc):
        cf, cb = (i + ncl) % nc, nc - 1 - i
        hf_s[cf], c_f, m_f = advance(0, cf, c_f, m_f)
        hb_s[cb], c_b, m_b = advance(1, cb, c_b, m_b)

    for c in range(nc):
        h = (hf_s[c] + hb_s[c]).T
        og = og_ref[c * lc:(c + 1) * lc, :].astype(F32)
        o_ref[c * lc:(c + 1) * lc, :] = ((_rms(h) * ng_ref[...]) * _sigmoid(og)).astype(o_ref.dtype)


def _mlstm_mix(q, k, v, og, gcol, grow, cwq, cwk, ng, S):
    B, T, W = q.shape
    H = W // ML_HEAD_DIM
    d = ML_HEAD_DIM
    nc = T // ML_BLOCK
    head = pl.BlockSpec((None, T, d), lambda b, h: (b, 0, h))
    return pl.pallas_call(
        functools.partial(_mlstm_kernel, S, T),
        grid=(B, H),
        in_specs=[head, head, head, head,
                  pl.BlockSpec((None, None, nc, ML_BLOCK, 4), lambda b, h: (b, h, 0, 0, 0)),
                  pl.BlockSpec((None, None, nc, 4, ML_BLOCK), lambda b, h: (b, h, 0, 0, 0)),
                  pl.BlockSpec((SHORT_W, d), lambda b, h: (0, h)),
                  pl.BlockSpec((SHORT_W, d), lambda b, h: (0, h)),
                  pl.BlockSpec((1, d), lambda b, h: (0, h))],
        out_specs=head,
        out_shape=jax.ShapeDtypeStruct((B, T, W), BF16),
        scratch_shapes=[pltpu.VMEM((T, d), BF16), pltpu.VMEM((nc, d, ML_BLOCK), BF16),
                        pltpu.VMEM((nc, 2 * d, ML_BLOCK), BF16),
                        pltpu.VMEM((nc, d, ML_BLOCK), F32), pltpu.VMEM((nc, d, ML_BLOCK), F32),
                        pltpu.VMEM((2, nc, 2 * d, ML_BLOCK), F32), pltpu.VMEM((2, nc, 2 * d, d), F32),
                        pltpu.VMEM((2, nc, 1, ML_BLOCK), F32)],
        compiler_params=_params(("parallel", "parallel"), 48),
        name="mlstm_mix",
    )(q, k, v, og, gcol, grow, cwq, cwk, ng)


FETCH = 16
HALO = 8
FFN_SPLIT = 2


def _halo_specs(a, n_lat_tiles):
    fb = TM // FETCH
    if isinstance(a, tuple):
        lat, cx = a
        assert cx.shape[1] == TM
        w, nfb = lat.shape[2], lat.shape[1] // FETCH
        li = lambda i: jnp.minimum(i, n_lat_tiles - 1)
        return ([pl.BlockSpec((None, TM, w), lambda b, i: (b, li(i), 0)),
                 pl.BlockSpec((None, FETCH, w), lambda b, i: (b, jnp.maximum(li(i) * fb - 1, 0), 0)),
                 pl.BlockSpec((None, FETCH, w), lambda b, i: (b, jnp.minimum((li(i) + 1) * fb, nfb - 1), 0)),
                 pl.BlockSpec((None, TM, w), lambda b, i: (b, 0, 0))], [lat, lat, lat, cx])
    w, nfb = a.shape[2], a.shape[1] // FETCH
    return ([pl.BlockSpec((None, TM, w), lambda b, i: (b, i, 0)),
             pl.BlockSpec((None, FETCH, w), lambda b, i: (b, jnp.maximum(i * fb - 1, 0), 0)),
             pl.BlockSpec((None, FETCH, w), lambda b, i: (b, jnp.minimum((i + 1) * fb, nfb - 1), 0))], [a, a, a])


def _load_halo(refs, n_lat_tiles):
    main = refs[0][...]
    if len(refs) == 4:
        main = jnp.where(pl.program_id(1) >= n_lat_tiles, refs[3][...], main)
    return jnp.concatenate([refs[1][...], main, refs[2][...]], axis=0)


def _tail_kernel(final, n_lat_tiles, counts, mod_ref, *refs):
    groups = []
    for n in counts:
        groups.append(refs[:n])
        refs = refs[n:]
    n_act = len(counts) - 1
    wo_refs = refs[:n_act]
    g_ref, wg_ref, wu_ref, cw_ref, cb_ref, wd_ref, fn_ref, o_ref = refs[n_act:]
    i = pl.program_id(1)
    nt = pl.num_programs(1)
    mod = mod_ref[...]
    y = None
    for a_refs, w_ref in zip(groups[1:], wo_refs):
        t = _bdot(_load_halo(a_refs, n_lat_tiles).astype(BF16), w_ref[...])
        y = t if y is None else y + t
    xe = _load_halo(groups[0], n_lat_tiles).astype(F32) + mod[2:3, :] * y
    x = xe[FETCH:FETCH + TM]
    hf = _modnorm(xe[FETCH - HALO:FETCH + TM + HALO], mod, g_ref[...], 3)
    he = hf.astype(BF16)
    h = hf[HALO:HALO + TM].astype(BF16)
    has_prev = jnp.logical_and(i != 0, i != n_lat_tiles)
    has_next = jnp.logical_and(i != n_lat_tiles - 1, i != nt - 1)
    row = lax.broadcasted_iota(jnp.int32, (TM + 2 * HALO, 1), 0)
    inside = jnp.logical_and(jnp.logical_or(row >= HALO, has_prev), jnp.logical_or(row < TM + HALO, has_next))
    fh = wg_ref.shape[1]
    fc = fh // FFN_SPLIT
    y = None
    for c in range(FFN_SPLIT):
        cols = slice(c * fc, (c + 1) * fc)
        ge = jnp.where(inside, _bdot(he, wg_ref[:, cols]), 0.0)
        gm = pltpu.roll(ge, 1, 0)[HALO:HALO + TM]
        gq = pltpu.roll(ge, TM + 2 * HALO - 1, 0)[HALO:HALO + TM]
        cw = cw_ref[:, cols]
        gc = cw[0:1, :] * gm + cw[1:2, :] * ge[HALO:HALO + TM] + cw[2:3, :] * gq + cb_ref[:, cols]
        a = (_silu(gc) * _bdot(h, wu_ref[:, cols])).astype(BF16)
        t = _bdot(a, wd_ref[cols, :])
        y = t if y is None else y + t
    o = x + mod[5:6, :] * y
    if final:
        o = _rms(o) * fn_ref[...]
    o_ref[...] = o


def _mixer_tail(x, acts, ws_out, mod, norm_g, w_up, cw, cb, w_down, fn, n_rows, n_lat_tiles, final):
    B, _, _, D = mod.shape
    Fh = w_up.shape[1] // 2
    wg = w_up[:, :Fh].astype(BF16)
    wu = w_up[:, Fh:].astype(BF16)
    nt = n_rows // TM
    once = lambda shape: pl.BlockSpec(shape, lambda b, i: (0, 0), pipeline_mode=pl.Buffered(1))
    in_specs = [pl.BlockSpec((None, None, 6, D), lambda b, i: (b, (i >= n_lat_tiles).astype(jnp.int32), 0, 0))]
    args = [mod]
    counts = []
    for a in [x] + list(acts):
        sp, ar = _halo_specs(a, n_lat_tiles)
        in_specs += sp
        args += ar
        counts.append(len(ar))
    in_specs += [once(w.shape) for w in ws_out]
    in_specs += [once((1, D)), once((D, Fh)), once((D, Fh)), once((SHORT_W, Fh)), once((1, Fh)), once((Fh, D)),
                 once((1, D))]
    return pl.pallas_call(
        functools.partial(_tail_kernel, final, n_lat_tiles, tuple(counts)),
        grid=(B, nt), in_specs=in_specs,
        out_specs=pl.BlockSpec((None, TM, D), lambda b, i: (b, i, 0)),
        out_shape=jax.ShapeDtypeStruct((B, n_rows, D), F32),
        compiler_params=_params(("parallel", "arbitrary"), 56),
        name="mixer_tail",
    )(*args, *ws_out, norm_g.reshape(1, D), wg, wu, cw.astype(F32), cb.reshape(1, Fh).astype(F32),
      w_down.astype(BF16), fn.reshape(1, D))


def _rope_tables(S, T):
    p = np.arange(S)
    inv = ROPE_BASE ** (-np.arange(0, ROPE_AXIS, 2, dtype=np.float64) / ROPE_AXIS)
    ar = (p // GRID_W)[:, None] * inv[None, :]
    ac = (p % GRID_W)[:, None] * inv[None, :]
    cos = np.concatenate([np.cos(ar), np.cos(ar), np.cos(ac), np.cos(ac)], axis=1)
    sin = np.concatenate([-np.sin(ar), np.sin(ar), -np.sin(ac), np.sin(ac)], axis=1)
    cos = np.concatenate([cos, np.ones((T - S, MLA_ROPE))], axis=0)
    sin = np.concatenate([sin, np.zeros((T - S, MLA_ROPE))], axis=0)
    return cos.astype(np.float32), sin.astype(np.float32)


def _mla_proj_kernel(x_ref, mod_ref, g_ref, wdn_ref, qn_ref, kvn_ref, wuq_ref, wukv_ref, cos_ref, sin_ref,
                     qo_ref, ko_ref, vo_ref):
    H, dn, dr = MLA_HEADS, MLA_NOPE, MLA_ROPE
    h = _modnorm(x_ref[...], mod_ref[...], g_ref[...], 0).astype(BF16)
    dnp = _bdot(h, wdn_ref[...])
    qa = dnp[:, :MLA_Q_RANK]
    kva = dnp[:, MLA_Q_RANK:MLA_Q_RANK + MLA_KV_RANK]
    kr2 = dnp[:, MLA_Q_RANK + MLA_KV_RANK:]
    q = _bdot((_rms(qa) * qn_ref[...]).astype(BF16), wuq_ref[...])
    kv = _bdot((_rms(kva) * kvn_ref[...]).astype(BF16), wukv_ref[...])
    cosr = cos_ref[...]
    sinr = sin_ref[...]
    kr = kr2[:, :dr] * cosr + kr2[:, dr:] * sinr
    zpad = jnp.zeros((kr.shape[0], MLA_DK_PAD - dn - dr), F32)
    ro = H * dn
    for hd in range(H):
        qr = q[:, ro + hd * dr:ro + (hd + 1) * dr] * cosr + q[:, ro + H * dr + hd * dr:ro + H * dr + (hd + 1) * dr] * sinr
        qh = jnp.concatenate([q[:, hd * dn:(hd + 1) * dn], qr, zpad], axis=1) * (MLA_SCALE * math.log2(math.e))
        qo_ref[hd] = qh.astype(BF16)
        ko_ref[hd] = jnp.concatenate([kv[:, hd * dn:(hd + 1) * dn], kr, zpad], axis=1).astype(BF16)
        vo_ref[hd] = kv[:, ro + hd * MLA_V:ro + (hd + 1) * MLA_V].astype(BF16)


def _mla_proj(x, mod, norm_g, w_down, q_norm, kv_norm, w_uq, w_ukv, S):
    B, T, D = x.shape
    H, dn, dr, dv = MLA_HEADS, MLA_NOPE, MLA_ROPE, MLA_V
    dk = dn + dr
    nt = T // TM
    n_lat_tiles = S // TM
    swap = np.arange(dr) ^ (ROPE_AXIS // 2)
    wkr = w_down[:, MLA_Q_RANK + MLA_KV_RANK:]
    wdn = jnp.concatenate([w_down, wkr[:, swap]], axis=1).astype(BF16)
    wq = w_uq.reshape(MLA_Q_RANK, H, dk)
    wq_n = wq[:, :, :dn].reshape(MLA_Q_RANK, H * dn)
    wq_r = wq[:, :, dn:]
    wuq = jnp.concatenate([wq_n, wq_r.reshape(MLA_Q_RANK, H * dr), wq_r[:, :, swap].reshape(MLA_Q_RANK, H * dr)],
                          axis=1).astype(BF16)
    wkv = w_ukv.reshape(MLA_KV_RANK, H, dn + dv)
    wukv = jnp.concatenate([wkv[:, :, :dn].reshape(MLA_KV_RANK, H * dn), wkv[:, :, dn:].reshape(MLA_KV_RANK, H * dv)],
                           axis=1).astype(BF16)
    cos, sin = _rope_tables(S, T)
    full = lambda a: pl.BlockSpec(a.shape, lambda b, i: (0,) * a.ndim)
    in_specs = [pl.BlockSpec((None, TM, D), lambda b, i: (b, i, 0)),
                pl.BlockSpec((None, None, 6, D), lambda b, i: (b, (i >= n_lat_tiles).astype(jnp.int32), 0, 0)),
                pl.BlockSpec((1, D), lambda b, i: (0, 0)),
                full(wdn), pl.BlockSpec((1, MLA_Q_RANK), lambda b, i: (0, 0)),
                pl.BlockSpec((1, MLA_KV_RANK), lambda b, i: (0, 0)), full(wuq), full(wukv),
                pl.BlockSpec((TM, dr), lambda b, i: (i, 0)), pl.BlockSpec((TM, dr), lambda b, i: (i, 0))]
    headed = lambda w: pl.BlockSpec((None, H, TM, w), lambda b, i: (b, 0, i, 0))
    return pl.pallas_call(
        _mla_proj_kernel,
        grid=(B, nt), in_specs=in_specs,
        out_specs=[headed(MLA_DK_PAD), headed(MLA_DK_PAD), headed(dv)],
        out_shape=[jax.ShapeDtypeStruct((B, H, T, MLA_DK_PAD), BF16), jax.ShapeDtypeStruct((B, H, T, MLA_DK_PAD), BF16),
                   jax.ShapeDtypeStruct((B, H, T, dv), BF16)],
        compiler_params=_params(("parallel", "arbitrary"), 40),
        name="mla_proj",
    )(x, mod, norm_g.reshape(1, D), wdn, q_norm.reshape(1, -1), kv_norm.reshape(1, -1), wuq, wukv,
      jnp.asarray(cos), jnp.asarray(sin))


ATT_HEADS_PER_STEP = 4
ATT_KEY_SLAB = 768


def _attn_kernel(q_ref, k_ref, v_ref, o_ref, va_s):
    hp, T, dv = v_ref.shape

    @pl.when(pl.program_id(2) == 0)
    def _():
        ones_col = (lax.broadcasted_iota(jnp.int32, (T, dv), 1) == 0).astype(BF16)
        for j in range(hp):
            va_s[j] = jnp.concatenate([v_ref[j], ones_col], axis=1)

    kw = max(w for w in range(LANES, ATT_KEY_SLAB + 1, LANES) if T % w == 0)
    slabs = [slice(c * kw, (c + 1) * kw) for c in range(T // kw)]
    nt = (((1,), (1,)), ((), ()))
    s = [[lax.dot_general(q_ref[j], k_ref[j, sl, :], nt, preferred_element_type=F32) for sl in slabs]
         for j in range(hp)]
    for j in range(hp):
        m = functools.reduce(jnp.maximum, [jnp.max(sc, axis=-1, keepdims=True) for sc in s[j]])
        na = None
        for sc, sl in zip(s[j], slabs):
            t = _bdot(jnp.exp2(sc - m).astype(BF16), va_s[j, sl, :])
            na = t if na is None else na + t
        o_ref[:, j * dv:(j + 1) * dv] = (na[:, :dv] * (1.0 / na[:, dv:dv + 1])).astype(o_ref.dtype)


def _attention(q, k, v, S):
    B, H, T, dk = k.shape
    dv = v.shape[3]
    tq = min(S, 512)
    hp = ATT_HEADS_PER_STEP
    return pl.pallas_call(
        _attn_kernel,
        grid=(B, H // hp, S // tq),
        in_specs=[pl.BlockSpec((None, hp, tq, dk), lambda b, h, i: (b, h, i, 0)),
                  pl.BlockSpec((None, hp, T, dk), lambda b, h, i: (b, h, 0, 0)),
                  pl.BlockSpec((None, hp, T, dv), lambda b, h, i: (b, h, 0, 0))],
        out_specs=pl.BlockSpec((None, tq, hp * dv), lambda b, h, i: (b, i, h)),
        out_shape=jax.ShapeDtypeStruct((B, S, H * dv), BF16),
        scratch_shapes=[pltpu.VMEM((hp, T, 2 * dv), BF16)],
        compiler_params=_params(("parallel", "parallel", "arbitrary"), 48),
        name="mla_attention",
    )(q, k, v)


def kernel(x, c, ctx, c_ctx, ada_w_0, ada_b_0, norm_mix_0, norm_ffn_0, w_in_0, hy_conv_w, hy_conv_b, hy_fw1, hy_fb1, hy_fw2, hy_fb2, hy_fw3, hy_freq, hy_bias, ml_conv_w, ml_gate_b, ml_norm_g, w_out_0, ffn_up_0, ffn_conv_w_0, ffn_conv_b_0, ffn_down_0, ada_w_1, ada_b_1, norm_mix_1, norm_ffn_1, mla_w_down, mla_q_norm, mla_kv_norm, mla_w_uq, mla_w_ukv, mla_w_o, ffn_up_1, ffn_conv_w_1, ffn_conv_b_1, ffn_down_1, final_norm):
    B, S, D = x.shape
    CL = ctx.shape[1]
    T = S + CL
    assert S % TM == 0 and CL % TM == 0 and S % CL == 0 and S % GRID_W == 0 and TM == ML_BLOCK
    nlt = S // TM
    C = HY_WIDTH
    W = ML_WIDTH
    H = ML_HEADS

    mod0 = _modulation(c, c_ctx, ada_w_0, ada_b_0)
    mod1 = _modulation(c, c_ctx, ada_w_1, ada_b_1)

    o = 3 * C
    wi = w_in_0.astype(BF16)
    w_gate = jnp.zeros((D, LANES), BF16).at[:, :4 * H].set(wi[:, o + 4 * W:])
    b_gate = jnp.zeros((1, LANES), F32).at[0, :4 * H].set(ml_gate_b.astype(F32))
    ws0 = [wi[:, 0:C], wi[:, C:2 * C], wi[:, 2 * C:3 * C], wi[:, o:o + W], wi[:, o + W:o + 2 * W],
           wi[:, o + 2 * W:o + 3 * W], wi[:, o + 3 * W:o + 4 * W], w_gate]
    hx0, hx1, hv, mq, mk, mv, mog, gates = _proj((x, ctx), mod0, norm_mix_0, ws0, 0, T, nlt, [BF16] * 7 + [F32],
                                                 bias=b_gate)

    filt = (hy_fw1, hy_fb1, hy_fw2, hy_fb2, hy_fw3, hy_freq, hy_bias)
    hcw = jnp.transpose(hy_conv_w.reshape(SHORT_W, 3, C), (1, 0, 2)).astype(F32)
    hcb = hy_conv_b.reshape(3, C).astype(F32)
    hy = []
    for L, blk in ((S, 0), (CL, S // CL)):
        R = _hyena_radix(L)
        wf_np, wft_np = _dft_tables(L // R)
        wf = jnp.asarray(wf_np).astype(BF16)
        wft = jnp.asarray(wft_np).astype(BF16)
        taps = _hyena_filter_taps(L, *filt)
        spectra = _hyena_filter_spectrum(L, R, wf, taps)
        hy.append(_hyena_mix(hx0, hx1, hv, hcw, hcb, wf, wft, spectra, L, R, blk))

    nc = T // ML_BLOCK
    g4 = gates[:, :, :4 * H].reshape(B, nc, ML_BLOCK, 4, H)
    gcol = jnp.transpose(g4, (0, 4, 1, 2, 3))
    grow = jnp.transpose(g4, (0, 4, 1, 3, 2))
    mcw = ml_conv_w.astype(F32)
    ml = _mlstm_mix(mq, mk, mv, mog, gcol, grow, mcw[:, :W], mcw[:, W:], ml_norm_g.reshape(1, W).astype(F32), S)

    wo = w_out_0.astype(BF16)
    xs = _mixer_tail((x, ctx), [tuple(hy), ml], [wo[:C], wo[C:]], mod0, norm_ffn_0, ffn_up_0, ffn_conv_w_0,
                     ffn_conv_b_0, ffn_down_0, final_norm, T, nlt, False)

    q, k, v = _mla_proj(xs, mod1, norm_mix_1, mla_w_down, mla_q_norm, mla_kv_norm, mla_w_uq, mla_w_ukv, S)
    att = _attention(q, k, v, S)
    return _mixer_tail(xs, [att], [mla_w_o.astype(BF16)], mod1, norm_ffn_1, ffn_up_1, ffn_conv_w_1, ffn_conv_b_1,
                       ffn_down_1, final_norm, S, nlt, True)
```

```python
import functools
import math

import numpy as np
import jax
import jax.numpy as jnp
from jax import lax
from jax.experimental import pallas as pl
from jax.experimental.pallas import tpu as pltpu

F32 = jnp.float32
BF16 = jnp.bfloat16
HI = lax.Precision.HIGHEST

RMS_EPS = 1e-6
GRID_W = 64
SHORT_W = 3
HY_WIDTH = 512
HY_EMB = 33
HY_FAST = 0.3
HY_SLOW = 1.5
HY_TARGET = 1e-2
HY_SHIFT = 0.05
ML_HEADS = 4
ML_HEAD_DIM = 128
ML_WIDTH = ML_HEADS * ML_HEAD_DIM
ML_BLOCK = 256
MLA_HEADS = 8
MLA_NOPE = 128
MLA_ROPE = 64
MLA_V = 128
MLA_Q_RANK = 384
MLA_KV_RANK = 256
MLA_SCALE = (MLA_NOPE + MLA_ROPE) ** -0.5
MLA_DK_PAD = 256
ROPE_AXIS = MLA_ROPE // 2
ROPE_BASE = 10000.0

TM = 256
LANES = 128
MIB = 1024 * 1024


def _params(sem, vmem_mib):
    return pltpu.CompilerParams(dimension_semantics=sem, vmem_limit_bytes=vmem_mib * MIB)


def _sigmoid(x):
    return 0.5 * jnp.tanh(0.5 * x) + 0.5


def _silu(x):
    return x * _sigmoid(x)


def _log_sigmoid(x):
    return jnp.minimum(x, 0.0) - jnp.log(1.0 + jnp.exp(-jnp.abs(x)))


def _rms(x):
    return x * lax.rsqrt(jnp.mean(x * x, axis=-1, keepdims=True) + RMS_EPS)


def _bdot(a, b):
    return jnp.dot(a, b, preferred_element_type=F32)


def _ada_kernel(cv_ref, w_ref, b_ref, o_ref):
    s = _silu(cv_ref[...])
    o_ref[...] = jnp.dot(s, w_ref[...], preferred_element_type=F32, precision=HI) + b_ref[...]


def _ada(cv, w, b):
    R, D = cv.shape
    N = w.shape[1]
    tn = N // 4
    return pl.pallas_call(
        _ada_kernel,
        grid=(N // tn,),
        in_specs=[pl.BlockSpec((R, D), lambda j: (0, 0)),
                  pl.BlockSpec((D, tn), lambda j: (0, j)),
                  pl.BlockSpec((1, tn), lambda j: (0, j))],
        out_specs=pl.BlockSpec((R, tn), lambda j: (0, j)),
        out_shape=jax.ShapeDtypeStruct((R, N), F32),
        compiler_params=_params(("arbitrary",), 40),
        name="ada_mod",
    )(cv, w, b.reshape(1, N))


def _modulation(c, c_ctx, w, b):
    B, D = c.shape
    R = -(-(B + 1) // 8) * 8
    cv = jnp.concatenate([c, c_ctx[None, :], jnp.zeros((R - B - 1, D), F32)], axis=0)
    m = _ada(cv, w, b)
    lat = m[:B].reshape(B, 1, 6, D)
    cx = jnp.broadcast_to(m[B].reshape(1, 1, 6, D), (B, 1, 6, D))
    return jnp.concatenate([lat, cx], axis=1)


def _modnorm(x, mod, g, si):
    sh = mod[si:si + 1, :]
    sc = mod[si + 1:si + 2, :]
    return (_rms(x) * g) * (1.0 + sc) + sh


def _row_specs(a, n_lat_tiles):
    if isinstance(a, tuple):
        w = a[0].shape[2]
        return ([pl.BlockSpec((None, TM, w), lambda b, i: (b, jnp.minimum(i, n_lat_tiles - 1), 0)),
                 pl.BlockSpec((None, TM, w), lambda b, i: (b, jnp.maximum(i - n_lat_tiles, 0), 0))], list(a))
    return [pl.BlockSpec((None, TM, a.shape[2]), lambda b, i: (b, i, 0))], [a]


def _load_rows(refs, n_lat_tiles):
    if len(refs) == 1:
        return refs[0][...]
    return jnp.where(pl.program_id(1) >= n_lat_tiles, refs[1][...], refs[0][...])


def _proj_kernel(n_x, widths, si, n_lat_tiles, *refs):
    x_refs = refs[:n_x]
    mod_ref, g_ref, w_ref, gb_ref = refs[n_x:n_x + 4]
    o_refs = refs[n_x + 4:]
    h = _modnorm(_load_rows(x_refs, n_lat_tiles), mod_ref[...], g_ref[...], si).astype(BF16)
    off = 0
    for j, wd in enumerate(widths):
        y = _bdot(h, w_ref[:, off:off + wd])
        off += wd
        if j == len(widths) - 1:
            y = _mlstm_gate_prep(y + gb_ref[...])
        o_refs[j][...] = y.astype(o_refs[j].dtype)


def _mlstm_gate_prep(g):
    n = g.shape[0]
    H = ML_HEADS
    lf = _log_sigmoid(g)
    t = lax.broadcasted_iota(jnp.int32, (n, n), 0)
    s = lax.broadcasted_iota(jnp.int32, (n, n), 1)
    hi = lf.astype(BF16)
    r1 = lf - hi.astype(F32)
    mid = r1.astype(BF16)
    parts = jnp.concatenate([hi, mid, (r1 - mid.astype(F32)).astype(BF16)], axis=1)
    w = g.shape[1]
    pre3 = _bdot((s <= t).astype(BF16), parts)
    suf3 = _bdot((s >= t).astype(BF16), parts)
    pre = pre3[:, :w] + pre3[:, w:2 * w] + pre3[:, 2 * w:]
    suf = suf3[:, :w] + suf3[:, w:2 * w] + suf3[:, 2 * w:]
    col = lax.broadcasted_iota(jnp.int32, (1, g.shape[1]), 1)
    g = jnp.where((col >= H) & (col < 2 * H), pre, g)
    return jnp.where((col >= 3 * H) & (col < 4 * H), suf, g)


def _in_proj(x, mod, g, w, gate_b, widths, n_rows, n_lat_tiles):
    B, _, _, D = mod.shape
    nt = n_rows // TM
    in_specs, args = _row_specs(x, n_lat_tiles)
    n_x = len(args)
    in_specs += [pl.BlockSpec((None, None, 6, D), lambda b, i: (b, (i >= n_lat_tiles).astype(jnp.int32), 0, 0)),
                 pl.BlockSpec((1, D), lambda b, i: (0, 0)),
                 pl.BlockSpec(w.shape, lambda b, i: (0, 0)),
                 pl.BlockSpec(gate_b.shape, lambda b, i: (0, 0))]
    args += [mod, g.reshape(1, D), w, gate_b]
    dtypes = [BF16] * (len(widths) - 1) + [F32]
    out_specs = [pl.BlockSpec((None, TM, wd), lambda b, i: (b, i, 0)) for wd in widths]
    out_shape = [jax.ShapeDtypeStruct((B, n_rows, wd), dt) for wd, dt in zip(widths, dtypes)]
    return pl.pallas_call(
        functools.partial(_proj_kernel, n_x, tuple(widths), 0, n_lat_tiles),
        grid=(B, nt), in_specs=in_specs, out_specs=out_specs, out_shape=out_shape,
        compiler_params=_params(("parallel", "arbitrary"), 48),
        name="modnorm_proj",
    )(*args)


def _filter_tables(L):
    pos = np.arange(L, dtype=np.float64)
    t = pos / (L - 1)
    bands = (HY_EMB - 1) // 2
    f = np.linspace(1e-4, bands - 1, bands)
    ang = (2.0 * math.pi / L) * pos[:, None] * f[None, :]
    z = np.concatenate([t[:, None], np.cos(ang), -np.sin(ang)], axis=-1)
    zp = np.zeros((2 * L, LANES), np.float64)
    zp[L:, :HY_EMB] = z
    zp[1:L, :HY_EMB] = z[:0:-1]
    return zp.astype(np.float32)


def _filt_mlp_kernel(L, tl, z_ref, w1, b1, w2, b2, w3, fq, absd, bias_ref, o_ref):
    i = pl.program_id(0)
    z = z_ref[...]
    f = fq[...]
    h = jnp.sin(f * (jnp.dot(z, w1[...], preferred_element_type=F32, precision=HI) + b1[...]))
    h = jnp.sin(f * (jnp.dot(h, w2[...], preferred_element_type=F32, precision=HI) + b2[...]))
    h = jnp.dot(h, w3[...], preferred_element_type=F32, precision=HI)
    t = z[:, 0:1]
    h = h * (jnp.exp(-t * absd[...]) + HY_SHIFT)
    lag = lax.broadcasted_iota(jnp.int32, (tl, 1), 0) + (i * tl - L)
    o_ref[...] = jnp.where(lag == -L, 0.0, jnp.where(lag == 0, h + bias_ref[...], h))


def _pad2(a, r, c):
    return jnp.zeros((r, c), F32).at[:a.shape[0], :a.shape[1]].set(a.astype(F32))


def _hyena_filter_lags(L, fw1, fb1, fw2, fb2, fw3, freq, hy_bias):
    tl = min(L, 512)
    nl = L // tl
    C = HY_WIDTH
    deltas = np.linspace(math.log(HY_TARGET) / HY_FAST, math.log(HY_TARGET) / HY_SLOW, C)
    absd = jnp.asarray(np.abs(np.tile(deltas, 2))[None, :].astype(np.float32))
    z = jnp.asarray(_filter_tables(L))
    full = lambda shape: pl.BlockSpec(shape, lambda i: (0, 0))
    half = lambda rows: pl.BlockSpec((rows, C), lambda i: (0, (i < nl).astype(jnp.int32)))
    return pl.pallas_call(
        functools.partial(_filt_mlp_kernel, L, tl),
        grid=(2 * nl,),
        in_specs=[pl.BlockSpec((tl, LANES), lambda i: (i, 0)),
                  full((LANES, LANES)), full((1, LANES)), full((LANES, LANES)), full((1, LANES)),
                  half(LANES), full((1, LANES)), half(1), full((1, C))],
        out_specs=pl.BlockSpec((tl, C), lambda i: (i, 0)),
        out_shape=jax.ShapeDtypeStruct((2 * L, C), F32),
        compiler_params=_params(("arbitrary",), 32),
        name="hyena_filter_mlp",
    )(z, _pad2(fw1, LANES, LANES), _pad2(fb1[None, :], 1, LANES), _pad2(fw2, LANES, LANES),
      _pad2(fb2[None, :], 1, LANES), _pad2(fw3, LANES, 2 * C), _pad2(freq[None, :], 1, LANES), absd,
      hy_bias.reshape(1, C).astype(F32))


def _dft_tables(L):
    n = 2 * L
    f = np.arange(L, dtype=np.int64)[:, None]
    s = np.arange(L, dtype=np.int64)[None, :]
    ang = (2.0 * math.pi / n) * ((f * s) % n).astype(np.float64)
    wc = np.cos(ang)
    ws = np.sin(ang)
    ws[0, :] = np.where(np.arange(L) % 2 == 0, 1.0, -1.0)
    w = np.concatenate([wc, ws], axis=0).astype(np.float32)
    return w, np.ascontiguousarray(w.T)


def _hyena_radix(L):
    return max(r for r in (4, 2, 1) if L % r == 0 and L // r >= 256)


def _filt_dft_kernel(M, wf_ref, k1_ref, k0_ref, p_ref, q_ref, r_ref):
    row = lax.broadcasted_iota(jnp.int32, (M, 1), 0)
    row0 = row == 0
    wf = wf_ref[...]
    t1 = _bdot(wf, k1_ref[...].astype(BF16))
    t0 = _bdot(wf, jnp.where(row0, 0.0, k0_ref[...]).astype(BF16))
    sgn = jnp.where((row & 1) == 1, -1.0, 1.0)
    sc = jnp.where(row0, 0.5 / M, 1.0 / M)
    ka = (t1[:M] + sgn * t0[:M]) * sc
    kb = (t1[M:] + sgn * t0[M:]) * sc
    p_ref[...] = ka
    q_ref[...] = jnp.where(row0, 0.0, kb)
    r_ref[...] = jnp.where(row0, kb, ka)


def _hyena_filter_spectrum(L, R, wf, lags):
    C = HY_WIDTH
    M = L // R
    lags = lags.reshape(2 * R, M, C)
    nd = 2 * R - 1
    out = pl.BlockSpec((None, M, C), lambda d: (d, 0, 0))
    return pl.pallas_call(
        functools.partial(_filt_dft_kernel, M),
        grid=(nd,),
        in_specs=[pl.BlockSpec((2 * M, M), lambda d: (0, 0)),
                  pl.BlockSpec((None, M, C), lambda d: (d + 1, 0, 0)),
                  pl.BlockSpec((None, M, C), lambda d: (d, 0, 0))],
        out_specs=[out, out, out],
        out_shape=[jax.ShapeDtypeStruct((nd, M, C), F32)] * 3,
        compiler_params=_params(("arbitrary",), 32),
        name="hyena_filter_dft",
    )(wf, lags, lags)


def _conv3_rows(x, w, first, last):
    n = x.shape[0]
    xm = jnp.where(first, 0.0, pltpu.roll(x, 1, 0))
    xp = jnp.where(last, 0.0, pltpu.roll(x, n - 1, 0))
    return w[0:1, :] * xm + w[1:2, :] * x + w[2:3, :] * xp


def _hyena_kernel(L, R, x0_ref, x1_ref, v_ref, cw_ref, cb_ref, wf_ref, wft_ref, p_ref, q_ref, r_ref, o_ref):
    M = L // R
    row = lax.broadcasted_iota(jnp.int32, (L, 1), 0)
    first = row == 0
    last = row == L - 1
    x1 = _conv3_rows(x1_ref[...].astype(F32), cw_ref[1], first, last) + cb_ref[1:2, :]
    v = _conv3_rows(v_ref[...].astype(F32), cw_ref[2], first, last) + cb_ref[2:3, :]
    g = (v * x1).astype(BF16)
    x0 = _conv3_rows(x0_ref[...].astype(F32), cw_ref[0], first, last) + cb_ref[0:1, :]
    wf = wf_ref[...]
    spec = [_bdot(wf, g[j * M:(j + 1) * M]) for j in range(R)]
    for i in range(R):
        yr = yb = None
        for j in range(R):
            a, b = spec[j][:M], spec[j][M:]
            d = i - j + R - 1
            p, q, r = p_ref[d], q_ref[d], r_ref[d]
            tr = a * p - b * q
            tb = a * q + b * r
            yr = tr if yr is None else yr + tr
            yb = tb if yb is None else yb + tb
        y = _bdot(wft_ref[...], jnp.concatenate([yr, yb], axis=0).astype(BF16))
        o_ref[i * M:(i + 1) * M, :] = (y * x0[i * M:(i + 1) * M]).astype(o_ref.dtype)


def _hyena_mix(x0, x1, v, cw, cb, wf, wft, spectra, L, R, row_block):
    B, T, C = x0.shape
    M = L // R
    cbw = 256
    nd = 2 * R - 1
    in_specs = [pl.BlockSpec((None, L, cbw), lambda c, b: (b, row_block, c))] * 3 + [
        pl.BlockSpec((3, SHORT_W, cbw), lambda c, b: (0, 0, c)),
        pl.BlockSpec((3, cbw), lambda c, b: (0, c)),
        pl.BlockSpec((2 * M, M), lambda c, b: (0, 0)),
        pl.BlockSpec((M, 2 * M), lambda c, b: (0, 0)),
    ] + [pl.BlockSpec((nd, M, cbw), lambda c, b: (0, 0, c))] * 3
    return pl.pallas_call(
        functools.partial(_hyena_kernel, L, R),
        grid=(C // cbw, B),
        in_specs=in_specs,
        out_specs=pl.BlockSpec((None, L, cbw), lambda c, b: (b, 0, c)),
        out_shape=jax.ShapeDtypeStruct((B, L, C), BF16),
        compiler_params=_params(("parallel", "arbitrary"), 56),
        name="hyena_mix",
    )(x0, x1, v, cw, cb, wf, wft, *spectra)


def _mlstm_kernel(S, T, q_ref, k_ref, v_ref, og_ref, gcol_ref, grow_ref, cwq_ref, cwk_ref, ng_ref, o_ref,
                  k_s, qt_s, vat_s, hf_s, hb_s, na_s, dc_s, ml_s):
    d = ML_HEAD_DIM
    lc = ML_BLOCK
    nc = T // lc
    ncl = S // lc
    row = lax.broadcasted_iota(jnp.int32, (T, 1), 0)
    first = (row == 0) | (row == S)
    last = (row == S - 1) | (row == T - 1)
    qc = _silu(_conv3_rows(q_ref[...].astype(F32), cwq_ref[...], first, last)) * (d ** -0.5)
    k_s[...] = _silu(_conv3_rows(k_ref[...].astype(F32), cwk_ref[...], first, last)).astype(BF16)
    ones_row = (lax.broadcasted_iota(jnp.int32, (d, lc), 0) == 0).astype(F32)
    for c in range(nc):
        qt_s[c] = qc[c * lc:(c + 1) * lc, :].T.astype(BF16)
        vt = v_ref[c * lc:(c + 1) * lc, :].astype(F32).T
        vat_s[c] = jnp.concatenate([vt, ones_row], axis=0).astype(BF16)

    si = lax.broadcasted_iota(jnp.int32, (lc, lc), 0)
    ti = lax.broadcasted_iota(jnp.int32, (lc, lc), 1)
    past = si <= ti
    future = si >= ti

    scans = ((0, past, lc - 1), (2, future, 0))

    b_tot = [[None] * nc for _ in scans]
    g_max = [[None] * nc for _ in scans]
    for c in range(nc):
        k = k_s[c * lc:(c + 1) * lc, :]
        vat = vat_s[c]
        st = _bdot(k, qt_s[c])
        vatf = vat.astype(F32)
        gc = gcol_ref[:, 4 * c:4 * c + 4]
        gr = grow_ref[c]
        for dn, (kind, mask, end) in enumerate(scans):
            r_col = gc[:, kind:kind + 1] - gc[:, kind + 1:kind + 2]
            b_row = gr[kind + 1:kind + 2, :]
            r_row = gr[kind:kind + 1, :] - b_row
            dlog = jnp.where(mask, r_col + b_row, -jnp.inf)
            m_loc = jnp.max(dlog, axis=0, keepdims=True)
            na_s[dn, c] = _bdot(vat, (st * jnp.exp(dlog - m_loc)).astype(BF16))
            ml_s[dn, c] = m_loc
            b_tot[dn][c] = b_row[:, end:end + 1]
            gs_row = b_tot[dn][c] + r_row
            g_max[dn][c] = jnp.max(gs_row, axis=1, keepdims=True)
            dc_s[dn, c] = _bdot((vatf * jnp.exp(gs_row - g_max[dn][c])).astype(BF16), k)

    def advance(dn, c, cs, m_prev):
        kind = scans[dn][0]
        m_loc = ml_s[dn, c]
        inter = grow_ref[c][kind + 1:kind + 2, :] + m_prev
        m_t = jnp.maximum(inter, m_loc)
        e_in = jnp.exp(m_loc - m_t)
        e_st = jnp.exp(inter - m_t)
        n_in = na_s[dn, c]
        n_st = _bdot(cs.astype(BF16), qt_s[c])
        den = e_in * n_in[d:d + 1, :] + e_st * n_st[d:d + 1, :]
        inv = 1.0 / jnp.maximum(jnp.abs(den), jnp.exp(-m_t))
        ht = (e_in * inv) * n_in[:d, :] + (e_st * inv) * n_st[:d, :]
        m_new = jnp.maximum(b_tot[dn][c] + m_prev, g_max[dn][c])
        cs = jnp.exp(b_tot[dn][c] + m_prev - m_new) * cs + jnp.exp(g_max[dn][c] - m_new) * dc_s[dn, c]
        return ht, cs, m_new

    c_f = c_b = jnp.zeros((2 * d, d), F32)
    m_f = m_b = jnp.zeros((1, 1), F32)
    for i in range(nc):
        cf, cb = (i + ncl) % nc, nc - 1 - i
        hf_s[cf], c_f, m_f = advance(0, cf, c_f, m_f)
        hb_s[cb], c_b, m_b = advance(1, cb, c_b, m_b)

    for c in range(nc):
        h = (hf_s[c] + hb_s[c]).T
        og = og_ref[c * lc:(c + 1) * lc, :].astype(F32)
        o_ref[c * lc:(c + 1) * lc, :] = ((_rms(h) * ng_ref[...]) * _sigmoid(og)).astype(o_ref.dtype)


def _mlstm_mix(q, k, v, og, gcol, grow, cwq, cwk, ng, S):
    B, T, W = q.shape
    H = W // ML_HEAD_DIM
    d = ML_HEAD_DIM
    nc = T // ML_BLOCK
    head = pl.BlockSpec((None, T, d), lambda b, h: (b, 0, h))
    return pl.pallas_call(
        functools.partial(_mlstm_kernel, S, T),
        grid=(B, H),
        in_specs=[head, head, head, head,
                  pl.BlockSpec((None, None, ML_BLOCK, nc * 4), lambda b, h: (b, h, 0, 0)),
                  pl.BlockSpec((None, None, nc, 4, ML_BLOCK), lambda b, h: (b, h, 0, 0, 0)),
                  pl.BlockSpec((SHORT_W, d), lambda b, h: (0, h)),
                  pl.BlockSpec((SHORT_W, d), lambda b, h: (0, h)),
                  pl.BlockSpec((1, d), lambda b, h: (0, h))],
        out_specs=head,
        out_shape=jax.ShapeDtypeStruct((B, T, W), BF16),
        scratch_shapes=[pltpu.VMEM((T, d), BF16), pltpu.VMEM((nc, d, ML_BLOCK), BF16),
                        pltpu.VMEM((nc, 2 * d, ML_BLOCK), BF16),
                        pltpu.VMEM((nc, d, ML_BLOCK), F32), pltpu.VMEM((nc, d, ML_BLOCK), F32),
                        pltpu.VMEM((2, nc, 2 * d, ML_BLOCK), F32), pltpu.VMEM((2, nc, 2 * d, d), F32),
                        pltpu.VMEM((2, nc, 1, ML_BLOCK), F32)],
        compiler_params=_params(("parallel", "parallel"), 48),
        name="mlstm_mix",
    )(q, k, v, og, gcol, grow, cwq, cwk, ng)


FETCH = 16
HALO = 8
FFN_SPLIT = 2
TAIL_TM_LATENT = 512


def _halo_specs(a, n_lat_tiles, tm):
    fb = tm // FETCH
    if isinstance(a, tuple):
        lat, cx = a
        assert cx.shape[1] == tm
        w, nfb = lat.shape[2], lat.shape[1] // FETCH
        li = lambda i: jnp.minimum(i, n_lat_tiles - 1)
        return ([pl.BlockSpec((None, tm, w), lambda b, i: (b, li(i), 0)),
                 pl.BlockSpec((None, FETCH, w), lambda b, i: (b, jnp.maximum(li(i) * fb - 1, 0), 0)),
                 pl.BlockSpec((None, FETCH, w), lambda b, i: (b, jnp.minimum((li(i) + 1) * fb, nfb - 1), 0)),
                 pl.BlockSpec((None, tm, w), lambda b, i: (b, 0, 0))], [lat, lat, lat, cx])
    w, nfb = a.shape[2], a.shape[1] // FETCH
    return ([pl.BlockSpec((None, tm, w), lambda b, i: (b, i, 0)),
             pl.BlockSpec((None, FETCH, w), lambda b, i: (b, jnp.maximum(i * fb - 1, 0), 0)),
             pl.BlockSpec((None, FETCH, w), lambda b, i: (b, jnp.minimum((i + 1) * fb, nfb - 1), 0))], [a, a, a])


def _load_halo(refs, n_lat_tiles):
    main = refs[0][...]
    if len(refs) == 4:
        main = jnp.where(pl.program_id(1) >= n_lat_tiles, refs[3][...], main)
    return jnp.concatenate([refs[1][...], main, refs[2][...]], axis=0)


def _tail_kernel(final, n_lat_tiles, counts, mod_ref, *refs):
    groups = []
    for n in counts:
        groups.append(refs[:n])
        refs = refs[n:]
    n_act = len(counts) - 1
    wo_refs = refs[:n_act]
    g_ref, wup_ref, cw_ref, cb_ref, wd_ref, fn_ref, o_ref = refs[n_act:]
    i = pl.program_id(1)
    nt = pl.num_programs(1)
    tm = o_ref.shape[0]
    mod = mod_ref[...]
    y = None
    for a_refs, w_ref in zip(groups[1:], wo_refs):
        t = _bdot(_load_halo(a_refs, n_lat_tiles).astype(BF16), w_ref[...])
        y = t if y is None else y + t
    xe = _load_halo(groups[0], n_lat_tiles).astype(F32) + mod[2:3, :] * y
    x = xe[FETCH:FETCH + tm]
    hf = _modnorm(xe[FETCH - HALO:FETCH + tm + HALO], mod, g_ref[...], 3)
    he = hf.astype(BF16)
    h = hf[HALO:HALO + tm].astype(BF16)
    has_prev = jnp.logical_and(i != 0, i != n_lat_tiles)
    has_next = jnp.logical_and(i != n_lat_tiles - 1, i != nt - 1)
    row = lax.broadcasted_iota(jnp.int32, (tm + 2 * HALO, 1), 0)
    inside = jnp.logical_and(jnp.logical_or(row >= HALO, has_prev), jnp.logical_or(row < tm + HALO, has_next))
    fh = wd_ref.shape[0]
    fc = fh // FFN_SPLIT
    y = None
    for c in range(FFN_SPLIT):
        cols = slice(c * fc, (c + 1) * fc)
        ge = jnp.where(inside, _bdot(he, wup_ref[:, cols]), 0.0)
        gm = pltpu.roll(ge, 1, 0)[HALO:HALO + tm]
        gq = pltpu.roll(ge, tm + 2 * HALO - 1, 0)[HALO:HALO + tm]
        cw = cw_ref[:, cols]
        gc = cw[0:1, :] * gm + cw[1:2, :] * ge[HALO:HALO + tm] + cw[2:3, :] * gq + cb_ref[:, cols]
        a = (_silu(gc) * _bdot(h, wup_ref[:, fh + c * fc:fh + (c + 1) * fc])).astype(BF16)
        t = _bdot(a, wd_ref[cols, :])
        y = t if y is None else y + t
    o = x + mod[5:6, :] * y
    if final:
        o = _rms(o) * fn_ref[...]
    o_ref[...] = o


def _mixer_tail(x, acts, ws_out, mod, norm_g, w_up, cw, cb, w_down, fn, n_rows, n_lat_tiles, final, tm):
    B, _, _, D = mod.shape
    Fh = w_up.shape[1] // 2
    nt = n_rows // tm
    once = lambda shape: pl.BlockSpec(shape, lambda b, i: (0, 0), pipeline_mode=pl.Buffered(1))
    in_specs = [pl.BlockSpec((None, None, 6, D), lambda b, i: (b, (i >= n_lat_tiles).astype(jnp.int32), 0, 0))]
    args = [mod]
    counts = []
    for a in [x] + list(acts):
        sp, ar = _halo_specs(a, n_lat_tiles, tm)
        in_specs += sp
        args += ar
        counts.append(len(ar))
    in_specs += [once(w.shape) for w in ws_out]
    in_specs += [once((1, D)), once((D, 2 * Fh)), once((SHORT_W, Fh)), once((1, Fh)), once((Fh, D)), once((1, D))]
    return pl.pallas_call(
        functools.partial(_tail_kernel, final, n_lat_tiles, tuple(counts)),
        grid=(B, nt), in_specs=in_specs,
        out_specs=pl.BlockSpec((None, tm, D), lambda b, i: (b, i, 0)),
        out_shape=jax.ShapeDtypeStruct((B, n_rows, D), F32),
        compiler_params=_params(("parallel", "arbitrary"), 56),
        name="mixer_tail",
    )(*args, *ws_out, norm_g.reshape(1, D), w_up.astype(BF16), cw.astype(F32), cb.reshape(1, Fh).astype(F32),
      w_down.astype(BF16), fn.reshape(1, D))


def _rope_tables(S, T):
    p = np.arange(S)
    inv = ROPE_BASE ** (-np.arange(0, ROPE_AXIS, 2, dtype=np.float64) / ROPE_AXIS)
    ar = (p // GRID_W)[:, None] * inv[None, :]
    ac = (p % GRID_W)[:, None] * inv[None, :]
    cos = np.concatenate([np.cos(ar), np.cos(ar), np.cos(ac), np.cos(ac)], axis=1)
    sin = np.concatenate([-np.sin(ar), np.sin(ar), -np.sin(ac), np.sin(ac)], axis=1)
    cos = np.concatenate([cos, np.ones((T - S, MLA_ROPE))], axis=0)
    sin = np.concatenate([sin, np.zeros((T - S, MLA_ROPE))], axis=0)
    return cos.astype(np.float32), sin.astype(np.float32)


def _mla_proj_kernel(x_ref, mod_ref, g_ref, wdn_ref, qn_ref, kvn_ref, wuq_ref, wukv_ref, cos_ref, sin_ref,
                     qo_ref, ko_ref, vo_ref):
    H, dn, dr = MLA_HEADS, MLA_NOPE, MLA_ROPE
    h = _modnorm(x_ref[...], mod_ref[...], g_ref[...], 0).astype(BF16)
    dnp = _bdot(h, wdn_ref[...])
    qa = dnp[:, :MLA_Q_RANK]
    kva = dnp[:, MLA_Q_RANK:MLA_Q_RANK + MLA_KV_RANK]
    kr2 = dnp[:, MLA_Q_RANK + MLA_KV_RANK:]
    q = _bdot((_rms(qa) * qn_ref[...]).astype(BF16), wuq_ref[...])
    kv = _bdot((_rms(kva) * kvn_ref[...]).astype(BF16), wukv_ref[...])
    cosr = cos_ref[...]
    sinr = sin_ref[...]
    kr = kr2[:, :dr] * cosr + kr2[:, dr:] * sinr
    zpad = jnp.zeros((kr.shape[0], MLA_DK_PAD - dn - dr), F32)
    ro = H * dn
    for hd in range(H):
        qr = q[:, ro + hd * dr:ro + (hd + 1) * dr] * cosr + q[:, ro + H * dr + hd * dr:ro + H * dr + (hd + 1) * dr] * sinr
        qh = jnp.concatenate([q[:, hd * dn:(hd + 1) * dn], qr, zpad], axis=1) * (MLA_SCALE * math.log2(math.e))
        qo_ref[hd] = qh.astype(BF16)
        ko_ref[hd] = jnp.concatenate([kv[:, hd * dn:(hd + 1) * dn], kr, zpad], axis=1).astype(BF16)
        vo_ref[hd] = kv[:, ro + hd * MLA_V:ro + (hd + 1) * MLA_V].astype(BF16)


def _mla_proj(x, mod, norm_g, w_down, q_norm, kv_norm, w_uq, w_ukv, S):
    B, T, D = x.shape
    H, dn, dr, dv = MLA_HEADS, MLA_NOPE, MLA_ROPE, MLA_V
    dk = dn + dr
    nt = T // TM
    n_lat_tiles = S // TM
    swap = np.arange(dr) ^ (ROPE_AXIS // 2)
    wkr = w_down[:, MLA_Q_RANK + MLA_KV_RANK:]
    wdn = jnp.concatenate([w_down, wkr[:, swap]], axis=1).astype(BF16)
    wq = w_uq.reshape(MLA_Q_RANK, H, dk)
    wq_n = wq[:, :, :dn].reshape(MLA_Q_RANK, H * dn)
    wq_r = wq[:, :, dn:]
    wuq = jnp.concatenate([wq_n, wq_r.reshape(MLA_Q_RANK, H * dr), wq_r[:, :, swap].reshape(MLA_Q_RANK, H * dr)],
                          axis=1).astype(BF16)
    wkv = w_ukv.reshape(MLA_KV_RANK, H, dn + dv)
    wukv = jnp.concatenate([wkv[:, :, :dn].reshape(MLA_KV_RANK, H * dn), wkv[:, :, dn:].reshape(MLA_KV_RANK, H * dv)],
                           axis=1).astype(BF16)
    cos, sin = _rope_tables(S, T)
    full = lambda a: pl.BlockSpec(a.shape, lambda b, i: (0,) * a.ndim)
    in_specs = [pl.BlockSpec((None, TM, D), lambda b, i: (b, i, 0)),
                pl.BlockSpec((None, None, 6, D), lambda b, i: (b, (i >= n_lat_tiles).astype(jnp.int32), 0, 0)),
                pl.BlockSpec((1, D), lambda b, i: (0, 0)),
                full(wdn), pl.BlockSpec((1, MLA_Q_RANK), lambda b, i: (0, 0)),
                pl.BlockSpec((1, MLA_KV_RANK), lambda b, i: (0, 0)), full(wuq), full(wukv),
                pl.BlockSpec((TM, dr), lambda b, i: (i, 0)), pl.BlockSpec((TM, dr), lambda b, i: (i, 0))]
    headed = lambda w: pl.BlockSpec((None, H, TM, w), lambda b, i: (b, 0, i, 0))
    return pl.pallas_call(
        _mla_proj_kernel,
        grid=(B, nt), in_specs=in_specs,
        out_specs=[headed(MLA_DK_PAD), headed(MLA_DK_PAD), headed(dv)],
        out_shape=[jax.ShapeDtypeStruct((B, H, T, MLA_DK_PAD), BF16), jax.ShapeDtypeStruct((B, H, T, MLA_DK_PAD), BF16),
                   jax.ShapeDtypeStruct((B, H, T, dv), BF16)],
        compiler_params=_params(("parallel", "arbitrary"), 40),
        name="mla_proj",
    )(x, mod, norm_g.reshape(1, D), wdn, q_norm.reshape(1, -1), kv_norm.reshape(1, -1), wuq, wukv,
      jnp.asarray(cos), jnp.asarray(sin))


ATT_HEADS_PER_STEP = 4
ATT_KEY_SLAB = 768


def _attn_kernel(q_ref, k_ref, v_ref, o_ref, va_s):
    hp, T, dv = v_ref.shape

    @pl.when(pl.program_id(2) == 0)
    def _():
        ones_col = (lax.broadcasted_iota(jnp.int32, (T, dv), 1) == 0).astype(BF16)
        for j in range(hp):
            va_s[j] = jnp.concatenate([v_ref[j], ones_col], axis=1)

    kw = max(w for w in range(LANES, ATT_KEY_SLAB + 1, LANES) if T % w == 0)
    slabs = [slice(c * kw, (c + 1) * kw) for c in range(T // kw)]
    nt = (((1,), (1,)), ((), ()))
    s = [[lax.dot_general(q_ref[j], k_ref[j, sl, :], nt, preferred_element_type=F32) for sl in slabs]
         for j in range(hp)]
    for j in range(hp):
        m = functools.reduce(jnp.maximum, [jnp.max(sc, axis=-1, keepdims=True) for sc in s[j]])
        na = None
        for sc, sl in zip(s[j], slabs):
            t = _bdot(jnp.exp2(sc - m).astype(BF16), va_s[j, sl, :])
            na = t if na is None else na + t
        o_ref[:, j * dv:(j + 1) * dv] = (na[:, :dv] * (1.0 / na[:, dv:dv + 1])).astype(o_ref.dtype)


def _attention(q, k, v, S):
    B, H, T, dk = k.shape
    dv = v.shape[3]
    tq = min(S, 512)
    hp = ATT_HEADS_PER_STEP
    return pl.pallas_call(
        _attn_kernel,
        grid=(B, H // hp, S // tq),
        in_specs=[pl.BlockSpec((None, hp, tq, dk), lambda b, h, i: (b, h, i, 0)),
                  pl.BlockSpec((None, hp, T, dk), lambda b, h, i: (b, h, 0, 0)),
                  pl.BlockSpec((None, hp, T, dv), lambda b, h, i: (b, h, 0, 0))],
        out_specs=pl.BlockSpec((None, tq, hp * dv), lambda b, h, i: (b, i, h)),
        out_shape=jax.ShapeDtypeStruct((B, S, H * dv), BF16),
        scratch_shapes=[pltpu.VMEM((hp, T, 2 * dv), BF16)],
        compiler_params=_params(("parallel", "parallel", "arbitrary"), 48),
        name="mla_attention",
    )(q, k, v)


def kernel(x, c, ctx, c_ctx, ada_w_0, ada_b_0, norm_mix_0, norm_ffn_0, w_in_0, hy_conv_w, hy_conv_b, hy_fw1, hy_fb1, hy_fw2, hy_fb2, hy_fw3, hy_freq, hy_bias, ml_conv_w, ml_gate_b, ml_norm_g, w_out_0, ffn_up_0, ffn_conv_w_0, ffn_conv_b_0, ffn_down_0, ada_w_1, ada_b_1, norm_mix_1, norm_ffn_1, mla_w_down, mla_q_norm, mla_kv_norm, mla_w_uq, mla_w_ukv, mla_w_o, ffn_up_1, ffn_conv_w_1, ffn_conv_b_1, ffn_down_1, final_norm):
    B, S, D = x.shape
    CL = ctx.shape[1]
    T = S + CL
    assert S % TM == 0 and CL % TM == 0 and S % CL == 0 and S % GRID_W == 0 and TM == ML_BLOCK
    nlt = S // TM
    C = HY_WIDTH
    W = ML_WIDTH
    H = ML_HEADS

    mod0 = _modulation(c, c_ctx, ada_w_0, ada_b_0)
    mod1 = _modulation(c, c_ctx, ada_w_1, ada_b_1)

    wi = jnp.pad(w_in_0, ((0, 0), (0, LANES - 4 * H))).astype(BF16)
    b_gate = jnp.zeros((1, LANES), F32).at[0, :4 * H].set(ml_gate_b.astype(F32))
    hx0, hx1, hv, mq, mk, mv, mog, gates = _in_proj((x, ctx), mod0, norm_mix_0, wi, b_gate,
                                                    [C] * 3 + [W] * 4 + [LANES], T, nlt)

    filt = (hy_fw1, hy_fb1, hy_fw2, hy_fb2, hy_fw3, hy_freq, hy_bias)
    hcw = jnp.transpose(hy_conv_w.reshape(SHORT_W, 3, C), (1, 0, 2)).astype(F32)
    hcb = hy_conv_b.reshape(3, C).astype(F32)
    hy = []
    for L, blk in ((S, 0), (CL, S // CL)):
        R = _hyena_radix(L)
        wf_np, wft_np = _dft_tables(L // R)
        wf = jnp.asarray(wf_np).astype(BF16)
        wft = jnp.asarray(wft_np).astype(BF16)
        spectra = _hyena_filter_spectrum(L, R, wf, _hyena_filter_lags(L, *filt))
        hy.append(_hyena_mix(hx0, hx1, hv, hcw, hcb, wf, wft, spectra, L, R, blk))

    nc = T // ML_BLOCK
    g4 = gates[:, :, :4 * H].reshape(B, nc, ML_BLOCK, 4, H)
    gcol = jnp.transpose(g4, (0, 4, 2, 1, 3)).reshape(B, H, ML_BLOCK, nc * 4)
    grow = jnp.transpose(g4, (0, 4, 1, 3, 2))
    mcw = ml_conv_w.astype(F32)
    ml = _mlstm_mix(mq, mk, mv, mog, gcol, grow, mcw[:, :W], mcw[:, W:], ml_norm_g.reshape(1, W).astype(F32), S)

    wo = w_out_0.astype(BF16)
    xs = _mixer_tail((x, ctx), [tuple(hy), ml], [wo[:C], wo[C:]], mod0, norm_ffn_0, ffn_up_0, ffn_conv_w_0,
                     ffn_conv_b_0, ffn_down_0, final_norm, T, nlt, False, TM)

    q, k, v = _mla_proj(xs, mod1, norm_mix_1, mla_w_down, mla_q_norm, mla_kv_norm, mla_w_uq, mla_w_ukv, S)
    att = _attention(q, k, v, S)
    tm1 = TAIL_TM_LATENT if S % TAIL_TM_LATENT == 0 else TM
    return _mixer_tail(xs, [att], [mla_w_o.astype(BF16)], mod1, norm_ffn_1, ffn_up_1, ffn_conv_w_1, ffn_conv_b_1,
                       ffn_down_1, final_norm, S, S // tm1, True, tm1)
```

```python
import functools
import math

import numpy as np
import jax
import jax.numpy as jnp
from jax import lax
from jax.experimental import pallas as pl
from jax.experimental.pallas import tpu as pltpu

F32 = jnp.float32
BF16 = jnp.bfloat16
HI = lax.Precision.HIGHEST

RMS_EPS = 1e-6
GRID_W = 64
SHORT_W = 3
HY_WIDTH = 512
HY_EMB = 33
HY_FAST = 0.3
HY_SLOW = 1.5
HY_TARGET = 1e-2
HY_SHIFT = 0.05
ML_HEADS = 4
ML_HEAD_DIM = 128
ML_WIDTH = ML_HEADS * ML_HEAD_DIM
ML_BLOCK = 256
MLA_HEADS = 8
MLA_NOPE = 128
MLA_ROPE = 64
MLA_V = 128
MLA_Q_RANK = 384
MLA_KV_RANK = 256
MLA_SCALE = (MLA_NOPE + MLA_ROPE) ** -0.5
MLA_DK_PAD = 256
ROPE_AXIS = MLA_ROPE // 2
ROPE_BASE = 10000.0

TM = 256
LANES = 128
MIB = 1024 * 1024


def _params(sem, vmem_mib):
    return pltpu.CompilerParams(dimension_semantics=sem, vmem_limit_bytes=vmem_mib * MIB)


def _sigmoid(x):
    return 0.5 * jnp.tanh(0.5 * x) + 0.5


def _silu(x):
    return x * _sigmoid(x)


def _log_sigmoid(x):
    return jnp.minimum(x, 0.0) - jnp.log(1.0 + jnp.exp(-jnp.abs(x)))


def _rms(x):
    return x * lax.rsqrt(jnp.mean(x * x, axis=-1, keepdims=True) + RMS_EPS)


def _bdot(a, b):
    return jnp.dot(a, b, preferred_element_type=F32)


def _ada_kernel(cv_ref, w_ref, b_ref, o_ref):
    s = _silu(cv_ref[...])
    o_ref[...] = jnp.dot(s, w_ref[...], preferred_element_type=F32, precision=HI) + b_ref[...]


def _ada(cv, w, b):
    R, D = cv.shape
    N = w.shape[1]
    tn = N // 4
    return pl.pallas_call(
        _ada_kernel,
        grid=(N // tn,),
        in_specs=[pl.BlockSpec((R, D), lambda j: (0, 0)),
                  pl.BlockSpec((D, tn), lambda j: (0, j)),
                  pl.BlockSpec((1, tn), lambda j: (0, j))],
        out_specs=pl.BlockSpec((R, tn), lambda j: (0, j)),
        out_shape=jax.ShapeDtypeStruct((R, N), F32),
        compiler_params=_params(("arbitrary",), 40),
        name="ada_mod",
    )(cv, w, b.reshape(1, N))


def _modulation(c, c_ctx, w, b):
    B, D = c.shape
    R = -(-(B + 1) // 8) * 8
    cv = jnp.concatenate([c, c_ctx[None, :], jnp.zeros((R - B - 1, D), F32)], axis=0)
    m = _ada(cv, w, b)
    lat = m[:B].reshape(B, 1, 6, D)
    cx = jnp.broadcast_to(m[B].reshape(1, 1, 6, D), (B, 1, 6, D))
    return jnp.concatenate([lat, cx], axis=1)


def _modnorm(x, mod, g, si):
    sh = mod[si:si + 1, :]
    sc = mod[si + 1:si + 2, :]
    return (_rms(x) * g) * (1.0 + sc) + sh


def _row_specs(a, n_lat_tiles):
    if isinstance(a, tuple):
        w = a[0].shape[2]
        return ([pl.BlockSpec((None, TM, w), lambda b, i: (b, jnp.minimum(i, n_lat_tiles - 1), 0)),
                 pl.BlockSpec((None, TM, w), lambda b, i: (b, jnp.maximum(i - n_lat_tiles, 0), 0))], list(a))
    return [pl.BlockSpec((None, TM, a.shape[2]), lambda b, i: (b, i, 0))], [a]


def _load_rows(refs, n_lat_tiles):
    if len(refs) == 1:
        return refs[0][...]
    return jnp.where(pl.program_id(1) >= n_lat_tiles, refs[1][...], refs[0][...])


def _in_proj_kernel(n_x, n_lat_tiles, *refs):
    x_refs = refs[:n_x]
    mod_ref, g_ref, w_ref, gb_ref, cw_ref, cb_ref = refs[n_x:n_x + 6]
    g_o, x0_o, q_o, k_o, v_o, og_o, gate_o = refs[n_x + 6:]
    C, W = HY_WIDTH, ML_WIDTH
    i = pl.program_id(1)
    nt = pl.num_programs(1)
    xe = _load_halo(x_refs, n_lat_tiles).astype(F32)
    hf = _modnorm(xe[FETCH - HALO:FETCH + TM + HALO], mod_ref[...], g_ref[...], 0)
    has_prev = jnp.logical_and(i != 0, i != n_lat_tiles)
    has_next = jnp.logical_and(i != n_lat_tiles - 1, i != nt - 1)
    row = lax.broadcasted_iota(jnp.int32, (TM + 2 * HALO, 1), 0)
    inside = jnp.logical_and(jnp.logical_or(row >= HALO, has_prev), jnp.logical_or(row < TM + HALO, has_next))
    he = jnp.where(inside, hf, 0.0).astype(BF16)
    h = hf[HALO:HALO + TM].astype(BF16)

    def conv_slab(off, wd):
        ue = _bdot(he, w_ref[:, off:off + wd])
        um = pltpu.roll(ue, 1, 0)[HALO:HALO + TM]
        uq = pltpu.roll(ue, TM + 2 * HALO - 1, 0)[HALO:HALO + TM]
        cw = cw_ref[:, off:off + wd]
        return cw[0:1, :] * um + cw[1:2, :] * ue[HALO:HALO + TM] + cw[2:3, :] * uq + cb_ref[:, off:off + wd]

    x0_o[...] = conv_slab(0, C).astype(x0_o.dtype)
    g_o[...] = (conv_slab(2 * C, C) * conv_slab(C, C)).astype(g_o.dtype)
    q_o[...] = (_silu(conv_slab(3 * C, W)) * (ML_HEAD_DIM ** -0.5)).astype(q_o.dtype)
    k_o[...] = _silu(conv_slab(3 * C + W, W)).astype(k_o.dtype)
    off = 3 * C + 2 * W
    v_o[...] = _bdot(h, w_ref[:, off:off + W]).astype(v_o.dtype)
    og_o[...] = _bdot(h, w_ref[:, off + W:off + 2 * W]).astype(og_o.dtype)
    gate_o[...] = _mlstm_gate_prep(_bdot(h, w_ref[:, off + 2 * W:]) + gb_ref[...])


def _mlstm_gate_prep(g):
    n = g.shape[0]
    H = ML_HEADS
    lf = _log_sigmoid(g)
    t = lax.broadcasted_iota(jnp.int32, (n, n), 0)
    s = lax.broadcasted_iota(jnp.int32, (n, n), 1)
    hi = lf.astype(BF16)
    r1 = lf - hi.astype(F32)
    mid = r1.astype(BF16)
    parts = jnp.concatenate([hi, mid, (r1 - mid.astype(F32)).astype(BF16)], axis=1)
    w = g.shape[1]
    pre3 = _bdot((s <= t).astype(BF16), parts)
    suf3 = _bdot((s >= t).astype(BF16), parts)
    pre = pre3[:, :w] + pre3[:, w:2 * w] + pre3[:, 2 * w:]
    suf = suf3[:, :w] + suf3[:, w:2 * w] + suf3[:, 2 * w:]
    col = lax.broadcasted_iota(jnp.int32, (1, g.shape[1]), 1)
    g = jnp.where((col >= H) & (col < 2 * H), pre, g)
    return jnp.where((col >= 3 * H) & (col < 4 * H), suf, g)


def _in_proj(x, mod, g, w, gate_b, conv_w, conv_b, n_rows, n_lat_tiles):
    B, _, _, D = mod.shape
    C, W = HY_WIDTH, ML_WIDTH
    nt = n_rows // TM
    in_specs, args = _halo_specs(x, n_lat_tiles, TM)
    n_x = len(args)
    const = lambda a: pl.BlockSpec(a.shape, lambda b, i: (0, 0))
    small = [g.reshape(1, D), w, gate_b, conv_w, conv_b]
    in_specs += [pl.BlockSpec((None, None, 6, D), lambda b, i: (b, (i >= n_lat_tiles).astype(jnp.int32), 0, 0))]
    in_specs += [const(a) for a in small]
    args += [mod] + small
    widths = [C, C, W, W, W, W, LANES]
    dtypes = [BF16] * 6 + [F32]
    out_specs = [pl.BlockSpec((None, TM, wd), lambda b, i: (b, i, 0)) for wd in widths]
    out_shape = [jax.ShapeDtypeStruct((B, n_rows, wd), dt) for wd, dt in zip(widths, dtypes)]
    return pl.pallas_call(
        functools.partial(_in_proj_kernel, n_x, n_lat_tiles),
        grid=(B, nt), in_specs=in_specs, out_specs=out_specs, out_shape=out_shape,
        compiler_params=_params(("parallel", "arbitrary"), 48),
        name="in_proj",
    )(*args)


def _filter_tables(L):
    pos = np.arange(L, dtype=np.float64)
    t = pos / (L - 1)
    bands = (HY_EMB - 1) // 2
    f = np.linspace(1e-4, bands - 1, bands)
    ang = (2.0 * math.pi / L) * pos[:, None] * f[None, :]
    z = np.concatenate([t[:, None], np.cos(ang), -np.sin(ang)], axis=-1)
    zp = np.zeros((2 * L, LANES), np.float64)
    zp[L:, :HY_EMB] = z
    zp[1:L, :HY_EMB] = z[:0:-1]
    return zp.astype(np.float32)


def _filt_mlp_kernel(L, tl, z_ref, w1, b1, w2, b2, w3, fq, absd, bias_ref, o_ref):
    i = pl.program_id(0)
    z = z_ref[...]
    f = fq[...]
    h = jnp.sin(f * (jnp.dot(z, w1[...], preferred_element_type=F32, precision=HI) + b1[...]))
    h = jnp.sin(f * (jnp.dot(h, w2[...], preferred_element_type=F32, precision=HI) + b2[...]))
    h = jnp.dot(h, w3[...], preferred_element_type=F32, precision=HI)
    t = z[:, 0:1]
    h = h * (jnp.exp(-t * absd[...]) + HY_SHIFT)
    lag = lax.broadcasted_iota(jnp.int32, (tl, 1), 0) + (i * tl - L)
    o_ref[...] = jnp.where(lag == -L, 0.0, jnp.where(lag == 0, h + bias_ref[...], h))


def _pad2(a, r, c):
    return jnp.zeros((r, c), F32).at[:a.shape[0], :a.shape[1]].set(a.astype(F32))


def _hyena_filter_lags(L, fw1, fb1, fw2, fb2, fw3, freq, hy_bias):
    tl = min(L, 512)
    nl = L // tl
    C = HY_WIDTH
    deltas = np.linspace(math.log(HY_TARGET) / HY_FAST, math.log(HY_TARGET) / HY_SLOW, C)
    absd = jnp.asarray(np.abs(np.tile(deltas, 2))[None, :].astype(np.float32))
    z = jnp.asarray(_filter_tables(L))
    full = lambda shape: pl.BlockSpec(shape, lambda i: (0, 0))
    half = lambda rows: pl.BlockSpec((rows, C), lambda i: (0, (i < nl).astype(jnp.int32)))
    return pl.pallas_call(
        functools.partial(_filt_mlp_kernel, L, tl),
        grid=(2 * nl,),
        in_specs=[pl.BlockSpec((tl, LANES), lambda i: (i, 0)),
                  full((LANES, LANES)), full((1, LANES)), full((LANES, LANES)), full((1, LANES)),
                  half(LANES), full((1, LANES)), half(1), full((1, C))],
        out_specs=pl.BlockSpec((tl, C), lambda i: (i, 0)),
        out_shape=jax.ShapeDtypeStruct((2 * L, C), F32),
        compiler_params=_params(("arbitrary",), 32),
        name="hyena_filter_mlp",
    )(z, _pad2(fw1, LANES, LANES), _pad2(fb1[None, :], 1, LANES), _pad2(fw2, LANES, LANES),
      _pad2(fb2[None, :], 1, LANES), _pad2(fw3, LANES, 2 * C), _pad2(freq[None, :], 1, LANES), absd,
      hy_bias.reshape(1, C).astype(F32))


def _dft_tables(L):
    n = 2 * L
    f = np.arange(L, dtype=np.int64)[:, None]
    s = np.arange(L, dtype=np.int64)[None, :]
    ang = (2.0 * math.pi / n) * ((f * s) % n).astype(np.float64)
    wc = np.cos(ang)
    ws = np.sin(ang)
    ws[0, :] = np.where(np.arange(L) % 2 == 0, 1.0, -1.0)
    w = np.concatenate([wc, ws], axis=0).astype(np.float32)
    return w, np.ascontiguousarray(w.T)


def _hyena_radix(L):
    return max(r for r in (4, 2, 1) if L % r == 0 and L // r >= 256)


def _filt_dft_kernel(M, wf_ref, k1_ref, k0_ref, p_ref, q_ref, r_ref):
    row = lax.broadcasted_iota(jnp.int32, (M, 1), 0)
    row0 = row == 0
    wf = wf_ref[...]
    t1 = _bdot(wf, k1_ref[...].astype(BF16))
    t0 = _bdot(wf, jnp.where(row0, 0.0, k0_ref[...]).astype(BF16))
    sgn = jnp.where((row & 1) == 1, -1.0, 1.0)
    sc = jnp.where(row0, 0.5 / M, 1.0 / M)
    ka = (t1[:M] + sgn * t0[:M]) * sc
    kb = (t1[M:] + sgn * t0[M:]) * sc
    p_ref[...] = ka
    q_ref[...] = jnp.where(row0, 0.0, kb)
    r_ref[...] = jnp.where(row0, kb, ka)


def _hyena_filter_spectrum(L, R, wf, lags):
    C = HY_WIDTH
    M = L // R
    lags = lags.reshape(2 * R, M, C)
    nd = 2 * R - 1
    out = pl.BlockSpec((None, M, C), lambda d: (d, 0, 0))
    return pl.pallas_call(
        functools.partial(_filt_dft_kernel, M),
        grid=(nd,),
        in_specs=[pl.BlockSpec((2 * M, M), lambda d: (0, 0)),
                  pl.BlockSpec((None, M, C), lambda d: (d + 1, 0, 0)),
                  pl.BlockSpec((None, M, C), lambda d: (d, 0, 0))],
        out_specs=[out, out, out],
        out_shape=[jax.ShapeDtypeStruct((nd, M, C), F32)] * 3,
        compiler_params=_params(("arbitrary",), 32),
        name="hyena_filter_dft",
    )(wf, lags, lags)


def _conv3_rows(x, w, first, last):
    n = x.shape[0]
    xm = jnp.where(first, 0.0, pltpu.roll(x, 1, 0))
    xp = jnp.where(last, 0.0, pltpu.roll(x, n - 1, 0))
    return w[0:1, :] * xm + w[1:2, :] * x + w[2:3, :] * xp


def _hyena_kernel(L, R, g_ref, x0_ref, wf_ref, wft_ref, p_ref, q_ref, r_ref, o_ref):
    M = L // R
    g = g_ref[...]
    x0 = x0_ref[...].astype(F32)
    wf = wf_ref[...]
    spec = [_bdot(wf, g[j * M:(j + 1) * M]) for j in range(R)]
    for i in range(R):
        yr = yb = None
        for j in range(R):
            a, b = spec[j][:M], spec[j][M:]
            d = i - j + R - 1
            p, q, r = p_ref[d], q_ref[d], r_ref[d]
            tr = a * p - b * q
            tb = a * q + b * r
            yr = tr if yr is None else yr + tr
            yb = tb if yb is None else yb + tb
        y = _bdot(wft_ref[...], jnp.concatenate([yr, yb], axis=0).astype(BF16))
        o_ref[i * M:(i + 1) * M, :] = (y * x0[i * M:(i + 1) * M]).astype(o_ref.dtype)


def _hyena_mix(g, x0, wf, wft, spectra, L, R, row_block):
    B, T, C = x0.shape
    M = L // R
    cbw = 256
    nd = 2 * R - 1
    in_specs = [pl.BlockSpec((None, L, cbw), lambda c, b: (b, row_block, c))] * 2 + [
        pl.BlockSpec((2 * M, M), lambda c, b: (0, 0)),
        pl.BlockSpec((M, 2 * M), lambda c, b: (0, 0)),
    ] + [pl.BlockSpec((nd, M, cbw), lambda c, b: (0, 0, c))] * 3
    return pl.pallas_call(
        functools.partial(_hyena_kernel, L, R),
        grid=(C // cbw, B),
        in_specs=in_specs,
        out_specs=pl.BlockSpec((None, L, cbw), lambda c, b: (b, 0, c)),
        out_shape=jax.ShapeDtypeStruct((B, L, C), BF16),
        compiler_params=_params(("parallel", "arbitrary"), 56),
        name="hyena_mix",
    )(g, x0, wf, wft, *spectra)


def _mlstm_kernel(S, T, q_ref, k_s, v_ref, og_ref, gcol_ref, grow_ref, ng_ref, o_ref,
                  qt_s, vat_s, hf_s, hb_s, na_s, dc_s, ml_s):
    d = ML_HEAD_DIM
    lc = ML_BLOCK
    nc = T // lc
    ncl = S // lc
    ones_row = (lax.broadcasted_iota(jnp.int32, (d, lc), 0) == 0).astype(F32)
    for c in range(nc):
        qt_s[c] = q_ref[c * lc:(c + 1) * lc, :].astype(F32).T.astype(BF16)
        vt = v_ref[c * lc:(c + 1) * lc, :].astype(F32).T
        vat_s[c] = jnp.concatenate([vt, ones_row], axis=0).astype(BF16)

    si = lax.broadcasted_iota(jnp.int32, (lc, lc), 0)
    ti = lax.broadcasted_iota(jnp.int32, (lc, lc), 1)
    past = si <= ti
    future = si >= ti

    scans = ((0, past, lc - 1), (2, future, 0))

    b_tot = [[None] * nc for _ in scans]
    g_max = [[None] * nc for _ in scans]
    for c in range(nc):
        k = k_s[c * lc:(c + 1) * lc, :]
        vat = vat_s[c]
        st = _bdot(k, qt_s[c])
        vatf = vat.astype(F32)
        gc = gcol_ref[:, 4 * c:4 * c + 4]
        gr = grow_ref[c]
        for dn, (kind, mask, end) in enumerate(scans):
            r_col = gc[:, kind:kind + 1] - gc[:, kind + 1:kind + 2]
            b_row = gr[kind + 1:kind + 2, :]
            r_row = gr[kind:kind + 1, :] - b_row
            dlog = jnp.where(mask, r_col + b_row, -jnp.inf)
            m_loc = jnp.max(dlog, axis=0, keepdims=True)
            na_s[dn, c] = _bdot(vat, (st * jnp.exp(dlog - m_loc)).astype(BF16))
            ml_s[dn, c] = m_loc
            b_tot[dn][c] = b_row[:, end:end + 1]
            gs_row = b_tot[dn][c] + r_row
            g_max[dn][c] = jnp.max(gs_row, axis=1, keepdims=True)
            dc_s[dn, c] = _bdot((vatf * jnp.exp(gs_row - g_max[dn][c])).astype(BF16), k)

    def advance(dn, c, cs, m_prev):
        kind = scans[dn][0]
        m_loc = ml_s[dn, c]
        inter = grow_ref[c][kind + 1:kind + 2, :] + m_prev
        m_t = jnp.maximum(inter, m_loc)
        e_in = jnp.exp(m_loc - m_t)
        e_st = jnp.exp(inter - m_t)
        n_in = na_s[dn, c]
        n_st = _bdot(cs.astype(BF16), qt_s[c])
        den = e_in * n_in[d:d + 1, :] + e_st * n_st[d:d + 1, :]
        inv = 1.0 / jnp.maximum(jnp.abs(den), jnp.exp(-m_t))
        ht = (e_in * inv) * n_in[:d, :] + (e_st * inv) * n_st[:d, :]
        m_new = jnp.maximum(b_tot[dn][c] + m_prev, g_max[dn][c])
        cs = jnp.exp(b_tot[dn][c] + m_prev - m_new) * cs + jnp.exp(g_max[dn][c] - m_new) * dc_s[dn, c]
        return ht, cs, m_new

    c_f = c_b = jnp.zeros((2 * d, d), F32)
    m_f = m_b = jnp.zeros((1, 1), F32)
    for i in range(nc):
        cf, cb = (i + ncl) % nc, nc - 1 - i
        hf_s[cf], c_f, m_f = advance(0, cf, c_f, m_f)
        hb_s[cb], c_b, m_b = advance(1, cb, c_b, m_b)

    for c in range(nc):
        h = (hf_s[c] + hb_s[c]).T
        og = og_ref[c * lc:(c + 1) * lc, :].astype(F32)
        o_ref[c * lc:(c + 1) * lc, :] = ((_rms(h) * ng_ref[...]) * _sigmoid(og)).astype(o_ref.dtype)


def _mlstm_mix(q, k, v, og, gcol, grow, ng, S):
    B, T, W = q.shape
    H = W // ML_HEAD_DIM
    d = ML_HEAD_DIM
    nc = T // ML_BLOCK
    head = pl.BlockSpec((None, T, d), lambda b, h: (b, 0, h))
    return pl.pallas_call(
        functools.partial(_mlstm_kernel, S, T),
        grid=(B, H),
        in_specs=[head, head, head, head,
                  pl.BlockSpec((None, None, ML_BLOCK, nc * 4), lambda b, h: (b, h, 0, 0)),
                  pl.BlockSpec((None, None, nc, 4, ML_BLOCK), lambda b, h: (b, h, 0, 0, 0)),
                  pl.BlockSpec((1, d), lambda b, h: (0, h))],
        out_specs=head,
        out_shape=jax.ShapeDtypeStruct((B, T, W), BF16),
        scratch_shapes=[pltpu.VMEM((nc, d, ML_BLOCK), BF16),
                        pltpu.VMEM((nc, 2 * d, ML_BLOCK), BF16),
                        pltpu.VMEM((nc, d, ML_BLOCK), F32), pltpu.VMEM((nc, d, ML_BLOCK), F32),
                        pltpu.VMEM((2, nc, 2 * d, ML_BLOCK), F32), pltpu.VMEM((2, nc, 2 * d, d), F32),
                        pltpu.VMEM((2, nc, 1, ML_BLOCK), F32)],
        compiler_params=_params(("parallel", "parallel"), 48),
        name="mlstm_mix",
    )(q, k, v, og, gcol, grow, ng)


FETCH = 16
HALO = 8
FFN_SPLIT = 2
TAIL_TM_LATENT = 512


def _halo_specs(a, n_lat_tiles, tm):
    fb = tm // FETCH
    if isinstance(a, tuple):
        lat, cx = a
        assert cx.shape[1] == tm
        w, nfb = lat.shape[2], lat.shape[1] // FETCH
        li = lambda i: jnp.minimum(i, n_lat_tiles - 1)
        return ([pl.BlockSpec((None, tm, w), lambda b, i: (b, li(i), 0)),
                 pl.BlockSpec((None, FETCH, w), lambda b, i: (b, jnp.maximum(li(i) * fb - 1, 0), 0)),
                 pl.BlockSpec((None, FETCH, w), lambda b, i: (b, jnp.minimum((li(i) + 1) * fb, nfb - 1), 0)),
                 pl.BlockSpec((None, tm, w), lambda b, i: (b, 0, 0))], [lat, lat, lat, cx])
    w, nfb = a.shape[2], a.shape[1] // FETCH
    return ([pl.BlockSpec((None, tm, w), lambda b, i: (b, i, 0)),
             pl.BlockSpec((None, FETCH, w), lambda b, i: (b, jnp.maximum(i * fb - 1, 0), 0)),
             pl.BlockSpec((None, FETCH, w), lambda b, i: (b, jnp.minimum((i + 1) * fb, nfb - 1), 0))], [a, a, a])


def _load_halo(refs, n_lat_tiles):
    main = refs[0][...]
    if len(refs) == 4:
        main = jnp.where(pl.program_id(1) >= n_lat_tiles, refs[3][...], main)
    return jnp.concatenate([refs[1][...], main, refs[2][...]], axis=0)


def _tail_kernel(final, n_lat_tiles, counts, mod_ref, *refs):
    groups = []
    for n in counts:
        groups.append(refs[:n])
        refs = refs[n:]
    n_act = len(counts) - 1
    wo_refs = refs[:n_act]
    g_ref, wup_ref, cw_ref, cb_ref, wd_ref, fn_ref, o_ref = refs[n_act:]
    i = pl.program_id(1)
    nt = pl.num_programs(1)
    tm = o_ref.shape[0]
    mod = mod_ref[...]
    y = None
    for a_refs, w_ref in zip(groups[1:], wo_refs):
        t = _bdot(_load_halo(a_refs, n_lat_tiles).astype(BF16), w_ref[...])
        y = t if y is None else y + t
    xe = _load_halo(groups[0], n_lat_tiles).astype(F32) + mod[2:3, :] * y
    x = xe[FETCH:FETCH + tm]
    hf = _modnorm(xe[FETCH - HALO:FETCH + tm + HALO], mod, g_ref[...], 3)
    has_prev = jnp.logical_and(i != 0, i != n_lat_tiles)
    has_next = jnp.logical_and(i != n_lat_tiles - 1, i != nt - 1)
    row = lax.broadcasted_iota(jnp.int32, (tm + 2 * HALO, 1), 0)
    inside = jnp.logical_and(jnp.logical_or(row >= HALO, has_prev), jnp.logical_or(row < tm + HALO, has_next))
    he = jnp.where(inside, hf, 0.0).astype(BF16)
    h = hf[HALO:HALO + tm].astype(BF16)
    fh = wd_ref.shape[0]
    fc = fh // FFN_SPLIT
    y = None
    for c in range(FFN_SPLIT):
        cols = slice(c * fc, (c + 1) * fc)
        ge = _bdot(he, wup_ref[:, cols])
        gm = pltpu.roll(ge, 1, 0)[HALO:HALO + tm]
        gq = pltpu.roll(ge, tm + 2 * HALO - 1, 0)[HALO:HALO + tm]
        cw = cw_ref[:, cols]
        gc = cw[0:1, :] * gm + cw[1:2, :] * ge[HALO:HALO + tm] + cw[2:3, :] * gq + cb_ref[:, cols]
        a = (_silu(gc) * _bdot(h, wup_ref[:, fh + c * fc:fh + (c + 1) * fc])).astype(BF16)
        t = _bdot(a, wd_ref[cols, :])
        y = t if y is None else y + t
    o = x + mod[5:6, :] * y
    if final:
        o = _rms(o) * fn_ref[...]
    o_ref[...] = o


def _mixer_tail(x, acts, ws_out, mod, norm_g, w_up, cw, cb, w_down, fn, n_rows, n_lat_tiles, final, tm):
    B, _, _, D = mod.shape
    Fh = w_up.shape[1] // 2
    nt = n_rows // tm
    once = lambda shape: pl.BlockSpec(shape, lambda b, i: (0, 0), pipeline_mode=pl.Buffered(1))
    in_specs = [pl.BlockSpec((None, None, 6, D), lambda b, i: (b, (i >= n_lat_tiles).astype(jnp.int32), 0, 0))]
    args = [mod]
    counts = []
    for a in [x] + list(acts):
        sp, ar = _halo_specs(a, n_lat_tiles, tm)
        in_specs += sp
        args += ar
        counts.append(len(ar))
    in_specs += [once(w.shape) for w in ws_out]
    in_specs += [once((1, D)), once((D, 2 * Fh)), once((SHORT_W, Fh)), once((1, Fh)), once((Fh, D)), once((1, D))]
    return pl.pallas_call(
        functools.partial(_tail_kernel, final, n_lat_tiles, tuple(counts)),
        grid=(B, nt), in_specs=in_specs,
        out_specs=pl.BlockSpec((None, tm, D), lambda b, i: (b, i, 0)),
        out_shape=jax.ShapeDtypeStruct((B, n_rows, D), F32),
        compiler_params=_params(("parallel", "arbitrary"), 56),
        name="mixer_tail",
    )(*args, *ws_out, norm_g.reshape(1, D), w_up.astype(BF16), cw.astype(F32), cb.reshape(1, Fh).astype(F32),
      w_down.astype(BF16), fn.reshape(1, D))


def _rope_tables(S, T):
    p = np.arange(S)
    inv = ROPE_BASE ** (-np.arange(0, ROPE_AXIS, 2, dtype=np.float64) / ROPE_AXIS)
    ar = (p // GRID_W)[:, None] * inv[None, :]
    ac = (p % GRID_W)[:, None] * inv[None, :]
    cos = np.concatenate([np.cos(ar), np.cos(ar), np.cos(ac), np.cos(ac)], axis=1)
    sin = np.concatenate([-np.sin(ar), np.sin(ar), -np.sin(ac), np.sin(ac)], axis=1)
    cos = np.concatenate([cos, np.ones((T - S, MLA_ROPE))], axis=0)
    sin = np.concatenate([sin, np.zeros((T - S, MLA_ROPE))], axis=0)
    return cos.astype(np.float32), sin.astype(np.float32)


def _mla_proj_kernel(x_ref, mod_ref, g_ref, wdn_ref, qn_ref, kvn_ref, wuq_ref, wukv_ref, cos_ref, sin_ref,
                     qo_ref, ko_ref, vo_ref):
    H, dn, dr = MLA_HEADS, MLA_NOPE, MLA_ROPE
    h = _modnorm(x_ref[...], mod_ref[...], g_ref[...], 0).astype(BF16)
    dnp = _bdot(h, wdn_ref[...])
    qa = dnp[:, :MLA_Q_RANK]
    kva = dnp[:, MLA_Q_RANK:MLA_Q_RANK + MLA_KV_RANK]
    kr2 = dnp[:, MLA_Q_RANK + MLA_KV_RANK:]
    q = _bdot((_rms(qa) * qn_ref[...]).astype(BF16), wuq_ref[...])
    kv = _bdot((_rms(kva) * kvn_ref[...]).astype(BF16), wukv_ref[...])
    cosr = cos_ref[...]
    sinr = sin_ref[...]
    kr = kr2[:, :dr] * cosr + kr2[:, dr:] * sinr
    zpad = jnp.zeros((kr.shape[0], MLA_DK_PAD - dn - dr), F32)
    ro = H * dn
    for hd in range(H):
        qr = q[:, ro + hd * dr:ro + (hd + 1) * dr] * cosr + q[:, ro + H * dr + hd * dr:ro + H * dr + (hd + 1) * dr] * sinr
        qh = jnp.concatenate([q[:, hd * dn:(hd + 1) * dn], qr, zpad], axis=1) * (MLA_SCALE * math.log2(math.e))
        qo_ref[hd] = qh.astype(BF16)
        ko_ref[hd] = jnp.concatenate([kv[:, hd * dn:(hd + 1) * dn], kr, zpad], axis=1).astype(BF16)
        vo_ref[hd] = kv[:, ro + hd * MLA_V:ro + (hd + 1) * MLA_V].astype(BF16)


def _mla_proj(x, mod, norm_g, w_down, q_norm, kv_norm, w_uq, w_ukv, S):
    B, T, D = x.shape
    H, dn, dr, dv = MLA_HEADS, MLA_NOPE, MLA_ROPE, MLA_V
    dk = dn + dr
    nt = T // TM
    n_lat_tiles = S // TM
    swap = np.arange(dr) ^ (ROPE_AXIS // 2)
    wkr = w_down[:, MLA_Q_RANK + MLA_KV_RANK:]
    wdn = jnp.concatenate([w_down, wkr[:, swap]], axis=1).astype(BF16)
    wq = w_uq.reshape(MLA_Q_RANK, H, dk)
    wq_n = wq[:, :, :dn].reshape(MLA_Q_RANK, H * dn)
    wq_r = wq[:, :, dn:]
    wuq = jnp.concatenate([wq_n, wq_r.reshape(MLA_Q_RANK, H * dr), wq_r[:, :, swap].reshape(MLA_Q_RANK, H * dr)],
                          axis=1).astype(BF16)
    wkv = w_ukv.reshape(MLA_KV_RANK, H, dn + dv)
    wukv = jnp.concatenate([wkv[:, :, :dn].reshape(MLA_KV_RANK, H * dn), wkv[:, :, dn:].reshape(MLA_KV_RANK, H * dv)],
                           axis=1).astype(BF16)
    cos, sin = _rope_tables(S, T)
    full = lambda a: pl.BlockSpec(a.shape, lambda b, i: (0,) * a.ndim)
    in_specs = [pl.BlockSpec((None, TM, D), lambda b, i: (b, i, 0)),
                pl.BlockSpec((None, None, 6, D), lambda b, i: (b, (i >= n_lat_tiles).astype(jnp.int32), 0, 0)),
                pl.BlockSpec((1, D), lambda b, i: (0, 0)),
                full(wdn), pl.BlockSpec((1, MLA_Q_RANK), lambda b, i: (0, 0)),
                pl.BlockSpec((1, MLA_KV_RANK), lambda b, i: (0, 0)), full(wuq), full(wukv),
                pl.BlockSpec((TM, dr), lambda b, i: (i, 0)), pl.BlockSpec((TM, dr), lambda b, i: (i, 0))]
    headed = lambda w: pl.BlockSpec((None, H, TM, w), lambda b, i: (b, 0, i, 0))
    return pl.pallas_call(
        _mla_proj_kernel,
        grid=(B, nt), in_specs=in_specs,
        out_specs=[headed(MLA_DK_PAD), headed(MLA_DK_PAD), headed(dv)],
        out_shape=[jax.ShapeDtypeStruct((B, H, T, MLA_DK_PAD), BF16), jax.ShapeDtypeStruct((B, H, T, MLA_DK_PAD), BF16),
                   jax.ShapeDtypeStruct((B, H, T, dv), BF16)],
        compiler_params=_params(("parallel", "arbitrary"), 40),
        name="mla_proj",
    )(x, mod, norm_g.reshape(1, D), wdn, q_norm.reshape(1, -1), kv_norm.reshape(1, -1), wuq, wukv,
      jnp.asarray(cos), jnp.asarray(sin))


ATT_HEADS_PER_STEP = 4
ATT_KEY_SLAB = 768


def _attn_kernel(q_ref, k_ref, v_ref, o_ref, va_s):
    hp, T, dv = v_ref.shape

    @pl.when(pl.program_id(2) == 0)
    def _():
        ones_col = (lax.broadcasted_iota(jnp.int32, (T, dv), 1) == 0).astype(BF16)
        for j in range(hp):
            va_s[j] = jnp.concatenate([v_ref[j], ones_col], axis=1)

    kw = max(w for w in range(LANES, ATT_KEY_SLAB + 1, LANES) if T % w == 0)
    slabs = [slice(c * kw, (c + 1) * kw) for c in range(T // kw)]
    nt = (((1,), (1,)), ((), ()))
    s = [[lax.dot_general(q_ref[j], k_ref[j, sl, :], nt, preferred_element_type=F32) for sl in slabs]
         for j in range(hp)]
    for j in range(hp):
        m = functools.reduce(jnp.maximum, [jnp.max(sc, axis=-1, keepdims=True) for sc in s[j]])
        na = None
        for sc, sl in zip(s[j], slabs):
            t = _bdot(jnp.exp2(sc - m).astype(BF16), va_s[j, sl, :])
            na = t if na is None else na + t
        o_ref[:, j * dv:(j + 1) * dv] = (na[:, :dv] * (1.0 / na[:, dv:dv + 1])).astype(o_ref.dtype)


def _attention(q, k, v, S):
    B, H, T, dk = k.shape
    dv = v.shape[3]
    tq = min(S, 512)
    hp = ATT_HEADS_PER_STEP
    return pl.pallas_call(
        _attn_kernel,
        grid=(B, H // hp, S // tq),
        in_specs=[pl.BlockSpec((None, hp, tq, dk), lambda b, h, i: (b, h, i, 0)),
                  pl.BlockSpec((None, hp, T, dk), lambda b, h, i: (b, h, 0, 0)),
                  pl.BlockSpec((None, hp, T, dv), lambda b, h, i: (b, h, 0, 0))],
        out_specs=pl.BlockSpec((None, tq, hp * dv), lambda b, h, i: (b, i, h)),
        out_shape=jax.ShapeDtypeStruct((B, S, H * dv), BF16),
        scratch_shapes=[pltpu.VMEM((hp, T, 2 * dv), BF16)],
        compiler_params=_params(("parallel", "parallel", "arbitrary"), 48),
        name="mla_attention",
    )(q, k, v)


def kernel(x, c, ctx, c_ctx, ada_w_0, ada_b_0, norm_mix_0, norm_ffn_0, w_in_0, hy_conv_w, hy_conv_b, hy_fw1, hy_fb1, hy_fw2, hy_fb2, hy_fw3, hy_freq, hy_bias, ml_conv_w, ml_gate_b, ml_norm_g, w_out_0, ffn_up_0, ffn_conv_w_0, ffn_conv_b_0, ffn_down_0, ada_w_1, ada_b_1, norm_mix_1, norm_ffn_1, mla_w_down, mla_q_norm, mla_kv_norm, mla_w_uq, mla_w_ukv, mla_w_o, ffn_up_1, ffn_conv_w_1, ffn_conv_b_1, ffn_down_1, final_norm):
    B, S, D = x.shape
    CL = ctx.shape[1]
    T = S + CL
    assert S % TM == 0 and CL % TM == 0 and S % CL == 0 and S % GRID_W == 0 and TM == ML_BLOCK
    nlt = S // TM
    C = HY_WIDTH
    W = ML_WIDTH
    H = ML_HEADS

    mod0 = _modulation(c, c_ctx, ada_w_0, ada_b_0)
    mod1 = _modulation(c, c_ctx, ada_w_1, ada_b_1)

    wi = jnp.pad(w_in_0, ((0, 0), (0, LANES - 4 * H))).astype(BF16)
    b_gate = jnp.zeros((1, LANES), F32).at[0, :4 * H].set(ml_gate_b.astype(F32))
    conv_w = jnp.concatenate([hy_conv_w, ml_conv_w], axis=1).astype(F32)
    conv_b = jnp.concatenate([hy_conv_b, jnp.zeros((2 * W,), hy_conv_b.dtype)]).reshape(1, -1).astype(F32)
    hg, hx0, mq, mk, mv, mog, gates = _in_proj((x, ctx), mod0, norm_mix_0, wi, b_gate, conv_w, conv_b, T, nlt)

    filt = (hy_fw1, hy_fb1, hy_fw2, hy_fb2, hy_fw3, hy_freq, hy_bias)
    hy = []
    for L, blk in ((S, 0), (CL, S // CL)):
        R = _hyena_radix(L)
        wf_np, wft_np = _dft_tables(L // R)
        wf = jnp.asarray(wf_np).astype(BF16)
        wft = jnp.asarray(wft_np).astype(BF16)
        spectra = _hyena_filter_spectrum(L, R, wf, _hyena_filter_lags(L, *filt))
        hy.append(_hyena_mix(hg, hx0, wf, wft, spectra, L, R, blk))

    nc = T // ML_BLOCK
    g4 = gates[:, :, :4 * H].reshape(B, nc, ML_BLOCK, 4, H)
    gcol = jnp.transpose(g4, (0, 4, 2, 1, 3)).reshape(B, H, ML_BLOCK, nc * 4)
    grow = jnp.transpose(g4, (0, 4, 1, 3, 2))
    ml = _mlstm_mix(mq, mk, mv, mog, gcol, grow, ml_norm_g.reshape(1, W).astype(F32), S)

    wo = w_out_0.astype(BF16)
    xs = _mixer_tail((x, ctx), [tuple(hy), ml], [wo[:C], wo[C:]], mod0, norm_ffn_0, ffn_up_0, ffn_conv_w_0,
                     ffn_conv_b_0, ffn_down_0, final_norm, T, nlt, False, TM)

    q, k, v = _mla_proj(xs, mod1, norm_mix_1, mla_w_down, mla_q_norm, mla_kv_norm, mla_w_uq, mla_w_ukv, S)
    att = _attention(q, k, v, S)
    tm1 = TAIL_TM_LATENT if S % TAIL_TM_LATENT == 0 else TM
    return _mixer_tail(xs, [att], [mla_w_o.astype(BF16)], mod1, norm_ffn_1, ffn_up_1, ffn_conv_w_1, ffn_conv_b_1,
                       ffn_down_1, final_norm, S, S // tm1, True, tm1)
```

```python
import functools
import math

import numpy as np
import jax
import jax.numpy as jnp
from jax import lax
from jax.experimental import pallas as pl
from jax.experimental.pallas import tpu as pltpu

F32 = jnp.float32
BF16 = jnp.bfloat16
HI = lax.Precision.HIGHEST

RMS_EPS = 1e-6
GRID_W = 64
SHORT_W = 3
HY_WIDTH = 512
HY_EMB = 33
HY_FAST = 0.3
HY_SLOW = 1.5
HY_TARGET = 1e-2
HY_SHIFT = 0.05
ML_HEADS = 4
ML_HEAD_DIM = 128
ML_WIDTH = ML_HEADS * ML_HEAD_DIM
ML_BLOCK = 256
MLA_HEADS = 8
MLA_NOPE = 128
MLA_ROPE = 64
MLA_V = 128
MLA_Q_RANK = 384
MLA_KV_RANK = 256
MLA_SCALE = (MLA_NOPE + MLA_ROPE) ** -0.5
MLA_DK_PAD = 256
ROPE_AXIS = MLA_ROPE // 2
ROPE_BASE = 10000.0

TM = 256
LANES = 128
MIB = 1024 * 1024


def _params(sem, vmem_mib):
    return pltpu.CompilerParams(dimension_semantics=sem, vmem_limit_bytes=vmem_mib * MIB)


def _sigmoid(x):
    return 0.5 * jnp.tanh(0.5 * x) + 0.5


def _silu(x):
    return x * _sigmoid(x)


def _log_sigmoid(x):
    return jnp.minimum(x, 0.0) - jnp.log(1.0 + jnp.exp(-jnp.abs(x)))


def _rms(x):
    return x * lax.rsqrt(jnp.mean(x * x, axis=-1, keepdims=True) + RMS_EPS)


def _bdot(a, b):
    return jnp.dot(a, b, preferred_element_type=F32)


def _ada_kernel(cv_ref, w_ref, b_ref, o_ref):
    s = _silu(cv_ref[...])
    o_ref[...] = jnp.dot(s, w_ref[...], preferred_element_type=F32, precision=HI) + b_ref[...]


def _ada(cv, w, b):
    R, D = cv.shape
    N = w.shape[1]
    tn = N // 4
    return pl.pallas_call(
        _ada_kernel,
        grid=(N // tn,),
        in_specs=[pl.BlockSpec((R, D), lambda j: (0, 0)),
                  pl.BlockSpec((D, tn), lambda j: (0, j)),
                  pl.BlockSpec((1, tn), lambda j: (0, j))],
        out_specs=pl.BlockSpec((R, tn), lambda j: (0, j)),
        out_shape=jax.ShapeDtypeStruct((R, N), F32),
        compiler_params=_params(("arbitrary",), 40),
        name="ada_mod",
    )(cv, w, b.reshape(1, N))


def _modulation(c, c_ctx, w, b):
    B, D = c.shape
    R = -(-(B + 1) // 8) * 8
    cv = jnp.concatenate([c, c_ctx[None, :], jnp.zeros((R - B - 1, D), F32)], axis=0)
    m = _ada(cv, w, b)
    lat = m[:B].reshape(B, 1, 6, D)
    cx = jnp.broadcast_to(m[B].reshape(1, 1, 6, D), (B, 1, 6, D))
    return jnp.concatenate([lat, cx], axis=1)


def _modnorm(x, mod, g, si):
    sh = mod[si:si + 1, :]
    sc = mod[si + 1:si + 2, :]
    return (_rms(x) * g) * (1.0 + sc) + sh


def _row_specs(a, n_lat_tiles):
    if isinstance(a, tuple):
        w = a[0].shape[2]
        return ([pl.BlockSpec((None, TM, w), lambda b, i: (b, jnp.minimum(i, n_lat_tiles - 1), 0)),
                 pl.BlockSpec((None, TM, w), lambda b, i: (b, jnp.maximum(i - n_lat_tiles, 0), 0))], list(a))
    return [pl.BlockSpec((None, TM, a.shape[2]), lambda b, i: (b, i, 0))], [a]


def _load_rows(refs, n_lat_tiles):
    if len(refs) == 1:
        return refs[0][...]
    return jnp.where(pl.program_id(1) >= n_lat_tiles, refs[1][...], refs[0][...])


def _proj_kernel(n_x, widths, si, n_lat_tiles, *refs):
    x_refs = refs[:n_x]
    mod_ref, g_ref, w_ref, gb_ref = refs[n_x:n_x + 4]
    o_refs = refs[n_x + 4:]
    h = _modnorm(_load_rows(x_refs, n_lat_tiles), mod_ref[...], g_ref[...], si).astype(BF16)
    off = 0
    for j, wd in enumerate(widths):
        y = _bdot(h, w_ref[:, off:off + wd])
        off += wd
        if j == len(widths) - 1:
            y = _mlstm_gate_prep(y + gb_ref[...])
        o_refs[j][...] = y.astype(o_refs[j].dtype)


def _mlstm_gate_prep(g):
    n = g.shape[0]
    H = ML_HEADS
    lf = _log_sigmoid(g)
    t = lax.broadcasted_iota(jnp.int32, (n, n), 0)
    s = lax.broadcasted_iota(jnp.int32, (n, n), 1)
    hi = lf.astype(BF16)
    r1 = lf - hi.astype(F32)
    mid = r1.astype(BF16)
    parts = jnp.concatenate([hi, mid, (r1 - mid.astype(F32)).astype(BF16)], axis=1)
    w = g.shape[1]
    pre3 = _bdot((s <= t).astype(BF16), parts)
    suf3 = _bdot((s >= t).astype(BF16), parts)
    pre = pre3[:, :w] + pre3[:, w:2 * w] + pre3[:, 2 * w:]
    suf = suf3[:, :w] + suf3[:, w:2 * w] + suf3[:, 2 * w:]
    col = lax.broadcasted_iota(jnp.int32, (1, g.shape[1]), 1)
    g = jnp.where((col >= H) & (col < 2 * H), pre, g)
    return jnp.where((col >= 3 * H) & (col < 4 * H), suf, g)


def _in_proj(x, mod, g, w, gate_b, widths, n_rows, n_lat_tiles):
    B, _, _, D = mod.shape
    nt = n_rows // TM
    in_specs, args = _row_specs(x, n_lat_tiles)
    n_x = len(args)
    in_specs += [pl.BlockSpec((None, None, 6, D), lambda b, i: (b, (i >= n_lat_tiles).astype(jnp.int32), 0, 0)),
                 pl.BlockSpec((1, D), lambda b, i: (0, 0)),
                 pl.BlockSpec(w.shape, lambda b, i: (0, 0)),
                 pl.BlockSpec(gate_b.shape, lambda b, i: (0, 0))]
    args += [mod, g.reshape(1, D), w, gate_b]
    dtypes = [BF16] * (len(widths) - 1) + [F32]
    out_specs = [pl.BlockSpec((None, TM, wd), lambda b, i: (b, i, 0)) for wd in widths]
    out_shape = [jax.ShapeDtypeStruct((B, n_rows, wd), dt) for wd, dt in zip(widths, dtypes)]
    return pl.pallas_call(
        functools.partial(_proj_kernel, n_x, tuple(widths), 0, n_lat_tiles),
        grid=(B, nt), in_specs=in_specs, out_specs=out_specs, out_shape=out_shape,
        compiler_params=_params(("parallel", "arbitrary"), 48),
        name="modnorm_proj",
    )(*args)


def _filter_tables(L):
    pos = np.arange(L, dtype=np.float64)
    t = pos / (L - 1)
    bands = (HY_EMB - 1) // 2
    f = np.linspace(1e-4, bands - 1, bands)
    ang = (2.0 * math.pi / L) * pos[:, None] * f[None, :]
    z = np.concatenate([t[:, None], np.cos(ang), -np.sin(ang)], axis=-1)
    zp = np.zeros((2 * L, LANES), np.float64)
    zp[L:, :HY_EMB] = z
    zp[1:L, :HY_EMB] = z[:0:-1]
    return zp.astype(np.float32)


def _filt_mlp_kernel(L, tl, z_ref, w1, b1, w2, b2, w3, fq, absd, bias_ref, o_ref):
    i = pl.program_id(0)
    z = z_ref[...]
    f = fq[...]
    h = jnp.sin(f * (jnp.dot(z, w1[...], preferred_element_type=F32, precision=HI) + b1[...]))
    h = jnp.sin(f * (jnp.dot(h, w2[...], preferred_element_type=F32, precision=HI) + b2[...]))
    h = jnp.dot(h, w3[...], preferred_element_type=F32, precision=HI)
    t = z[:, 0:1]
    h = h * (jnp.exp(-t * absd[...]) + HY_SHIFT)
    lag = lax.broadcasted_iota(jnp.int32, (tl, 1), 0) + (i * tl - L)
    o_ref[...] = jnp.where(lag == -L, 0.0, jnp.where(lag == 0, h + bias_ref[...], h))


def _pad2(a, r, c):
    return jnp.zeros((r, c), F32).at[:a.shape[0], :a.shape[1]].set(a.astype(F32))


def _hyena_filter_lags(L, fw1, fb1, fw2, fb2, fw3, freq, hy_bias):
    tl = min(L, 512)
    nl = L // tl
    C = HY_WIDTH
    deltas = np.linspace(math.log(HY_TARGET) / HY_FAST, math.log(HY_TARGET) / HY_SLOW, C)
    absd = jnp.asarray(np.abs(np.tile(deltas, 2))[None, :].astype(np.float32))
    z = jnp.asarray(_filter_tables(L))
    full = lambda shape: pl.BlockSpec(shape, lambda i: (0, 0))
    half = lambda rows: pl.BlockSpec((rows, C), lambda i: (0, (i < nl).astype(jnp.int32)))
    return pl.pallas_call(
        functools.partial(_filt_mlp_kernel, L, tl),
        grid=(2 * nl,),
        in_specs=[pl.BlockSpec((tl, LANES), lambda i: (i, 0)),
                  full((LANES, LANES)), full((1, LANES)), full((LANES, LANES)), full((1, LANES)),
                  half(LANES), full((1, LANES)), half(1), full((1, C))],
        out_specs=pl.BlockSpec((tl, C), lambda i: (i, 0)),
        out_shape=jax.ShapeDtypeStruct((2 * L, C), F32),
        compiler_params=_params(("arbitrary",), 32),
        name="hyena_filter_mlp",
    )(z, _pad2(fw1, LANES, LANES), _pad2(fb1[None, :], 1, LANES), _pad2(fw2, LANES, LANES),
      _pad2(fb2[None, :], 1, LANES), _pad2(fw3, LANES, 2 * C), _pad2(freq[None, :], 1, LANES), absd,
      hy_bias.reshape(1, C).astype(F32))


def _dft_tables(L):
    n = 2 * L
    f = np.arange(L, dtype=np.int64)[:, None]
    s = np.arange(L, dtype=np.int64)[None, :]
    ang = (2.0 * math.pi / n) * ((f * s) % n).astype(np.float64)
    wc = np.cos(ang)
    ws = np.sin(ang)
    ws[0, :] = np.where(np.arange(L) % 2 == 0, 1.0, -1.0)
    w = np.concatenate([wc, ws], axis=0).astype(np.float32)
    return w, np.ascontiguousarray(w.T)


def _hyena_radix(L):
    return max(r for r in (4, 2, 1) if L % r == 0 and L // r >= 256)


def _filt_dft_kernel(M, wf_ref, k1_ref, k0_ref, p_ref, q_ref, r_ref):
    row = lax.broadcasted_iota(jnp.int32, (M, 1), 0)
    row0 = row == 0
    wf = wf_ref[...]
    t1 = _bdot(wf, k1_ref[...].astype(BF16))
    t0 = _bdot(wf, jnp.where(row0, 0.0, k0_ref[...]).astype(BF16))
    sgn = jnp.where((row & 1) == 1, -1.0, 1.0)
    sc = jnp.where(row0, 0.5 / M, 1.0 / M)
    ka = (t1[:M] + sgn * t0[:M]) * sc
    kb = (t1[M:] + sgn * t0[M:]) * sc
    p_ref[...] = ka
    q_ref[...] = jnp.where(row0, 0.0, kb)
    r_ref[...] = jnp.where(row0, kb, ka)


def _hyena_filter_spectrum(L, R, wf, lags):
    C = HY_WIDTH
    M = L // R
    lags = lags.reshape(2 * R, M, C)
    nd = 2 * R - 1
    out = pl.BlockSpec((None, M, C), lambda d: (d, 0, 0))
    return pl.pallas_call(
        functools.partial(_filt_dft_kernel, M),
        grid=(nd,),
        in_specs=[pl.BlockSpec((2 * M, M), lambda d: (0, 0)),
                  pl.BlockSpec((None, M, C), lambda d: (d + 1, 0, 0)),
                  pl.BlockSpec((None, M, C), lambda d: (d, 0, 0))],
        out_specs=[out, out, out],
        out_shape=[jax.ShapeDtypeStruct((nd, M, C), F32)] * 3,
        compiler_params=_params(("arbitrary",), 32),
        name="hyena_filter_dft",
    )(wf, lags, lags)


def _conv3_rows(x, w, first, last):
    n = x.shape[0]
    xm = jnp.where(first, 0.0, pltpu.roll(x, 1, 0))
    xp = jnp.where(last, 0.0, pltpu.roll(x, n - 1, 0))
    return w[0:1, :] * xm + w[1:2, :] * x + w[2:3, :] * xp


def _hyena_kernel(L, R, x0_ref, x1_ref, v_ref, cw_ref, cb_ref, wf_ref, wft_ref, p_ref, q_ref, r_ref, o_ref):
    M = L // R
    row = lax.broadcasted_iota(jnp.int32, (L, 1), 0)
    first = row == 0
    last = row == L - 1
    x1 = _conv3_rows(x1_ref[...].astype(F32), cw_ref[1], first, last) + cb_ref[1:2, :]
    v = _conv3_rows(v_ref[...].astype(F32), cw_ref[2], first, last) + cb_ref[2:3, :]
    g = (v * x1).astype(BF16)
    x0 = _conv3_rows(x0_ref[...].astype(F32), cw_ref[0], first, last) + cb_ref[0:1, :]
    wf = wf_ref[...]
    spec = [_bdot(wf, g[j * M:(j + 1) * M]) for j in range(R)]
    for i in range(R):
        yr = yb = None
        for j in range(R):
            a, b = spec[j][:M], spec[j][M:]
            d = i - j + R - 1
            p, q, r = p_ref[d], q_ref[d], r_ref[d]
            tr = a * p - b * q
            tb = a * q + b * r
            yr = tr if yr is None else yr + tr
            yb = tb if yb is None else yb + tb
        y = _bdot(wft_ref[...], jnp.concatenate([yr, yb], axis=0).astype(BF16))
        o_ref[i * M:(i + 1) * M, :] = (y * x0[i * M:(i + 1) * M]).astype(o_ref.dtype)


def _hyena_mix(x0, x1, v, cw, cb, wf, wft, spectra, L, R, row_block):
    B, T, C = x0.shape
    M = L // R
    cbw = 256
    nd = 2 * R - 1
    in_specs = [pl.BlockSpec((None, L, cbw), lambda c, b: (b, row_block, c))] * 3 + [
        pl.BlockSpec((3, SHORT_W, cbw), lambda c, b: (0, 0, c)),
        pl.BlockSpec((3, cbw), lambda c, b: (0, c)),
        pl.BlockSpec((2 * M, M), lambda c, b: (0, 0)),
        pl.BlockSpec((M, 2 * M), lambda c, b: (0, 0)),
    ] + [pl.BlockSpec((nd, M, cbw), lambda c, b: (0, 0, c))] * 3
    return pl.pallas_call(
        functools.partial(_hyena_kernel, L, R),
        grid=(C // cbw, B),
        in_specs=in_specs,
        out_specs=pl.BlockSpec((None, L, cbw), lambda c, b: (b, 0, c)),
        out_shape=jax.ShapeDtypeStruct((B, L, C), BF16),
        compiler_params=_params(("parallel", "arbitrary"), 56),
        name="hyena_mix",
    )(x0, x1, v, cw, cb, wf, wft, *spectra)


def _mlstm_kernel(S, T, q_ref, k_ref, v_ref, og_ref, gcol_ref, grow_ref, cwq_ref, cwk_ref, ng_ref, o_ref,
                  k_s, qt_s, vat_s, hf_s, hb_s, na_s, dc_s, ml_s):
    d = ML_HEAD_DIM
    lc = ML_BLOCK
    nc = T // lc
    ncl = S // lc
    row = lax.broadcasted_iota(jnp.int32, (T, 1), 0)
    first = (row == 0) | (row == S)
    last = (row == S - 1) | (row == T - 1)
    qc = _silu(_conv3_rows(q_ref[...].astype(F32), cwq_ref[...], first, last)) * (d ** -0.5)
    k_s[...] = _silu(_conv3_rows(k_ref[...].astype(F32), cwk_ref[...], first, last)).astype(BF16)
    ones_row = (lax.broadcasted_iota(jnp.int32, (d, lc), 0) == 0).astype(F32)
    for c in range(nc):
        qt_s[c] = qc[c * lc:(c + 1) * lc, :].T.astype(BF16)
        vt = v_ref[c * lc:(c + 1) * lc, :].astype(F32).T
        vat_s[c] = jnp.concatenate([vt, ones_row], axis=0).astype(BF16)

    si = lax.broadcasted_iota(jnp.int32, (lc, lc), 0)
    ti = lax.broadcasted_iota(jnp.int32, (lc, lc), 1)
    past = si <= ti
    future = si >= ti

    scans = ((0, past, lc - 1), (2, future, 0))

    b_tot = [[None] * nc for _ in scans]
    g_max = [[None] * nc for _ in scans]
    for c in range(nc):
        k = k_s[c * lc:(c + 1) * lc, :]
        vat = vat_s[c]
        st = _bdot(k, qt_s[c])
        vatf = vat.astype(F32)
        gc = gcol_ref[:, 4 * c:4 * c + 4]
        gr = grow_ref[c]
        for dn, (kind, mask, end) in enumerate(scans):
            r_col = gc[:, kind:kind + 1] - gc[:, kind + 1:kind + 2]
            b_row = gr[kind + 1:kind + 2, :]
            r_row = gr[kind:kind + 1, :] - b_row
            dlog = jnp.where(mask, r_col + b_row, -jnp.inf)
            m_loc = jnp.max(dlog, axis=0, keepdims=True)
            na_s[dn, c] = _bdot(vat, (st * jnp.exp(dlog - m_loc)).astype(BF16))
            ml_s[dn, c] = m_loc
            b_tot[dn][c] = b_row[:, end:end + 1]
            gs_row = b_tot[dn][c] + r_row
            g_max[dn][c] = jnp.max(gs_row, axis=1, keepdims=True)
            dc_s[dn, c] = _bdot((vatf * jnp.exp(gs_row - g_max[dn][c])).astype(BF16), k)

    def advance(dn, c, cs, m_prev):
        kind = scans[dn][0]
        m_loc = ml_s[dn, c]
        inter = grow_ref[c][kind + 1:kind + 2, :] + m_prev
        m_t = jnp.maximum(inter, m_loc)
        e_in = jnp.exp(m_loc - m_t)
        e_st = jnp.exp(inter - m_t)
        n_in = na_s[dn, c]
        n_st = _bdot(cs.astype(BF16), qt_s[c])
        den = e_in * n_in[d:d + 1, :] + e_st * n_st[d:d + 1, :]
        inv = 1.0 / jnp.maximum(jnp.abs(den), jnp.exp(-m_t))
        ht = (e_in * inv) * n_in[:d, :] + (e_st * inv) * n_st[:d, :]
        m_new = jnp.maximum(b_tot[dn][c] + m_prev, g_max[dn][c])
        cs = jnp.exp(b_tot[dn][c] + m_prev - m_new) * cs + jnp.exp(g_max[dn][c] - m_new) * dc_s[dn, c]
        return ht, cs, m_new

    c_f = c_b = jnp.zeros((2 * d, d), F32)
    m_f = m_b = jnp.zeros((1, 1), F32)
    for i in range(nc):
        cf, cb = (i + ncl) % nc, nc - 1 - i
        hf_s[cf], c_f, m_f = advance(0, cf, c_f, m_f)
        hb_s[cb], c_b, m_b = advance(1, cb, c_b, m_b)

    for c in range(nc):
        h = (hf_s[c] + hb_s[c]).T
        og = og_ref[c * lc:(c + 1) * lc, :].astype(F32)
        o_ref[c * lc:(c + 1) * lc, :] = ((_rms(h) * ng_ref[...]) * _sigmoid(og)).astype(o_ref.dtype)


def _mlstm_mix(q, k, v, og, gcol, grow, cwq, cwk, ng, S):
    B, T, W = q.shape
    H = W // ML_HEAD_DIM
    d = ML_HEAD_DIM
    nc = T // ML_BLOCK
    head = pl.BlockSpec((None, T, d), lambda b, h: (b, 0, h))
    return pl.pallas_call(
        functools.partial(_mlstm_kernel, S, T),
        grid=(B, H),
        in_specs=[head, head, head, head,
                  pl.BlockSpec((None, None, ML_BLOCK, nc * 4), lambda b, h: (b, h, 0, 0)),
                  pl.BlockSpec((None, None, nc, 4, ML_BLOCK), lambda b, h: (b, h, 0, 0, 0)),
                  pl.BlockSpec((SHORT_W, d), lambda b, h: (0, h)),
                  pl.BlockSpec((SHORT_W, d), lambda b, h: (0, h)),
                  pl.BlockSpec((1, d), lambda b, h: (0, h))],
        out_specs=head,
        out_shape=jax.ShapeDtypeStruct((B, T, W), BF16),
        scratch_shapes=[pltpu.VMEM((T, d), BF16), pltpu.VMEM((nc, d, ML_BLOCK), BF16),
                        pltpu.VMEM((nc, 2 * d, ML_BLOCK), BF16),
                        pltpu.VMEM((nc, d, ML_BLOCK), F32), pltpu.VMEM((nc, d, ML_BLOCK), F32),
                        pltpu.VMEM((2, nc, 2 * d, ML_BLOCK), F32), pltpu.VMEM((2, nc, 2 * d, d), F32),
                        pltpu.VMEM((2, nc, 1, ML_BLOCK), F32)],
        compiler_params=_params(("parallel", "parallel"), 48),
        name="mlstm_mix",
    )(q, k, v, og, gcol, grow, cwq, cwk, ng)


FETCH = 16
HALO = 8
FFN_SPLIT = 2
TAIL_TM_LATENT = 512


def _halo_specs(a, n_lat_tiles, tm):
    fb = tm // FETCH
    if isinstance(a, tuple):
        lat, cx = a
        assert cx.shape[1] == tm
        w, nfb = lat.shape[2], lat.shape[1] // FETCH
        li = lambda i: jnp.minimum(i, n_lat_tiles - 1)
        return ([pl.BlockSpec((None, tm, w), lambda b, i: (b, li(i), 0)),
                 pl.BlockSpec((None, FETCH, w), lambda b, i: (b, jnp.maximum(li(i) * fb - 1, 0), 0)),
                 pl.BlockSpec((None, FETCH, w), lambda b, i: (b, jnp.minimum((li(i) + 1) * fb, nfb - 1), 0)),
                 pl.BlockSpec((None, tm, w), lambda b, i: (b, 0, 0))], [lat, lat, lat, cx])
    w, nfb = a.shape[2], a.shape[1] // FETCH
    return ([pl.BlockSpec((None, tm, w), lambda b, i: (b, i, 0)),
             pl.BlockSpec((None, FETCH, w), lambda b, i: (b, jnp.maximum(i * fb - 1, 0), 0)),
             pl.BlockSpec((None, FETCH, w), lambda b, i: (b, jnp.minimum((i + 1) * fb, nfb - 1), 0))], [a, a, a])


def _load_halo(refs, n_lat_tiles):
    main = refs[0][...]
    if len(refs) == 4:
        main = jnp.where(pl.program_id(1) >= n_lat_tiles, refs[3][...], main)
    return jnp.concatenate([refs[1][...], main, refs[2][...]], axis=0)


def _tail_kernel(final, n_lat_tiles, counts, mod_ref, *refs):
    groups = []
    for n in counts:
        groups.append(refs[:n])
        refs = refs[n:]
    n_act = len(counts) - 1
    wo_refs = refs[:n_act]
    g_ref, wup_ref, cw_ref, cb_ref, wd_ref, fn_ref, o_ref = refs[n_act:]
    i = pl.program_id(1)
    nt = pl.num_programs(1)
    tm = o_ref.shape[0]
    mod = mod_ref[...]
    y = None
    for a_refs, w_ref in zip(groups[1:], wo_refs):
        t = _bdot(_load_halo(a_refs, n_lat_tiles).astype(BF16), w_ref[...])
        y = t if y is None else y + t
    xe = _load_halo(groups[0], n_lat_tiles).astype(F32) + mod[2:3, :] * y
    x = xe[FETCH:FETCH + tm]
    hf = _modnorm(xe[FETCH - HALO:FETCH + tm + HALO], mod, g_ref[...], 3)
    he = hf.astype(BF16)
    h = hf[HALO:HALO + tm].astype(BF16)
    has_prev = jnp.logical_and(i != 0, i != n_lat_tiles)
    has_next = jnp.logical_and(i != n_lat_tiles - 1, i != nt - 1)
    row = lax.broadcasted_iota(jnp.int32, (tm + 2 * HALO, 1), 0)
    inside = jnp.logical_and(jnp.logical_or(row >= HALO, has_prev), jnp.logical_or(row < tm + HALO, has_next))
    fh = wd_ref.shape[0]
    fc = fh // FFN_SPLIT
    y = None
    for c in range(FFN_SPLIT):
        cols = slice(c * fc, (c + 1) * fc)
        ge = jnp.where(inside, _bdot(he, wup_ref[:, cols]), 0.0)
        gm = pltpu.roll(ge, 1, 0)[HALO:HALO + tm]
        gq = pltpu.roll(ge, tm + 2 * HALO - 1, 0)[HALO:HALO + tm]
        cw = cw_ref[:, cols]
        gc = cw[0:1, :] * gm + cw[1:2, :] * ge[HALO:HALO + tm] + cw[2:3, :] * gq + cb_ref[:, cols]
        a = (_silu(gc) * _bdot(h, wup_ref[:, fh + c * fc:fh + (c + 1) * fc])).astype(BF16)
        t = _bdot(a, wd_ref[cols, :])
        y = t if y is None else y + t
    o = x + mod[5:6, :] * y
    if final:
        o = _rms(o) * fn_ref[...]
    o_ref[...] = o


def _mixer_tail(x, acts, ws_out, mod, norm_g, w_up, cw, cb, w_down, fn, n_rows, n_lat_tiles, final, tm):
    B, _, _, D = mod.shape
    Fh = w_up.shape[1] // 2
    nt = n_rows // tm
    once = lambda shape: pl.BlockSpec(shape, lambda b, i: (0, 0), pipeline_mode=pl.Buffered(1))
    in_specs = [pl.BlockSpec((None, None, 6, D), lambda b, i: (b, (i >= n_lat_tiles).astype(jnp.int32), 0, 0))]
    args = [mod]
    counts = []
    for a in [x] + list(acts):
        sp, ar = _halo_specs(a, n_lat_tiles, tm)
        in_specs += sp
        args += ar
        counts.append(len(ar))
    in_specs += [once(w.shape) for w in ws_out]
    in_specs += [once((1, D)), once((D, 2 * Fh)), once((SHORT_W, Fh)), once((1, Fh)), once((Fh, D)), once((1, D))]
    return pl.pallas_call(
        functools.partial(_tail_kernel, final, n_lat_tiles, tuple(counts)),
        grid=(B, nt), in_specs=in_specs,
        out_specs=pl.BlockSpec((None, tm, D), lambda b, i: (b, i, 0)),
        out_shape=jax.ShapeDtypeStruct((B, n_rows, D), F32),
        compiler_params=_params(("parallel", "arbitrary"), 56),
        name="mixer_tail",
    )(*args, *ws_out, norm_g.reshape(1, D), w_up.astype(BF16), cw.astype(F32), cb.reshape(1, Fh).astype(F32),
      w_down.astype(BF16), fn.reshape(1, D))


def _rope_tables(S, T):
    p = np.arange(S)
    inv = ROPE_BASE ** (-np.arange(0, ROPE_AXIS, 2, dtype=np.float64) / ROPE_AXIS)
    ar = (p // GRID_W)[:, None] * inv[None, :]
    ac = (p % GRID_W)[:, None] * inv[None, :]
    cos = np.concatenate([np.cos(ar), np.cos(ar), np.cos(ac), np.cos(ac)], axis=1)
    sin = np.concatenate([-np.sin(ar), np.sin(ar), -np.sin(ac), np.sin(ac)], axis=1)
    cos = np.concatenate([cos, np.ones((T - S, MLA_ROPE))], axis=0)
    sin = np.concatenate([sin, np.zeros((T - S, MLA_ROPE))], axis=0)
    return cos.astype(np.float32), sin.astype(np.float32)


def _mla_proj_kernel(n_x, n_lat_tiles, *refs):
    x_refs = refs[:n_x]
    mod_ref, g_ref, wdn_ref, qn_ref, kvn_ref, wuq_ref, wukv_ref, cos_ref, sin_ref, qo_ref, ko_ref, vo_ref = refs[n_x:]
    H, dn, dr = MLA_HEADS, MLA_NOPE, MLA_ROPE
    h = _modnorm(_load_rows(x_refs, n_lat_tiles), mod_ref[...], g_ref[...], 0).astype(BF16)
    dnp = _bdot(h, wdn_ref[...])
    qa = dnp[:, :MLA_Q_RANK]
    kva = dnp[:, MLA_Q_RANK:MLA_Q_RANK + MLA_KV_RANK]
    kr2 = dnp[:, MLA_Q_RANK + MLA_KV_RANK:]
    q = _bdot((_rms(qa) * qn_ref[...]).astype(BF16), wuq_ref[...])
    kv = _bdot((_rms(kva) * kvn_ref[...]).astype(BF16), wukv_ref[...])
    cosr = cos_ref[...]
    sinr = sin_ref[...]
    kr = kr2[:, :dr] * cosr + kr2[:, dr:] * sinr
    zpad = jnp.zeros((kr.shape[0], MLA_DK_PAD - dn - dr), F32)
    ro = H * dn
    for hd in range(H):
        qr = q[:, ro + hd * dr:ro + (hd + 1) * dr] * cosr + q[:, ro + H * dr + hd * dr:ro + H * dr + (hd + 1) * dr] * sinr
        qh = jnp.concatenate([q[:, hd * dn:(hd + 1) * dn], qr, zpad], axis=1) * (MLA_SCALE * math.log2(math.e))
        qo_ref[hd] = qh.astype(BF16)
        ko_ref[hd] = jnp.concatenate([kv[:, hd * dn:(hd + 1) * dn], kr, zpad], axis=1).astype(BF16)
        vo_ref[hd] = kv[:, ro + hd * MLA_V:ro + (hd + 1) * MLA_V].astype(BF16)


def _mla_proj(x, mod, norm_g, w_down, q_norm, kv_norm, w_uq, w_ukv, S, T):
    B, _, _, D = mod.shape
    H, dn, dr, dv = MLA_HEADS, MLA_NOPE, MLA_ROPE, MLA_V
    dk = dn + dr
    nt = T // TM
    n_lat_tiles = S // TM
    swap = np.arange(dr) ^ (ROPE_AXIS // 2)
    wkr = w_down[:, MLA_Q_RANK + MLA_KV_RANK:]
    wdn = jnp.concatenate([w_down, wkr[:, swap]], axis=1).astype(BF16)
    wq = w_uq.reshape(MLA_Q_RANK, H, dk)
    wq_n = wq[:, :, :dn].reshape(MLA_Q_RANK, H * dn)
    wq_r = wq[:, :, dn:]
    wuq = jnp.concatenate([wq_n, wq_r.reshape(MLA_Q_RANK, H * dr), wq_r[:, :, swap].reshape(MLA_Q_RANK, H * dr)],
                          axis=1).astype(BF16)
    wkv = w_ukv.reshape(MLA_KV_RANK, H, dn + dv)
    wukv = jnp.concatenate([wkv[:, :, :dn].reshape(MLA_KV_RANK, H * dn), wkv[:, :, dn:].reshape(MLA_KV_RANK, H * dv)],
                           axis=1).astype(BF16)
    cos, sin = _rope_tables(S, T)
    full = lambda a: pl.BlockSpec(a.shape, lambda b, i: (0,) * a.ndim)
    in_specs, args = _row_specs(x, n_lat_tiles)
    n_x = len(args)
    in_specs += [pl.BlockSpec((None, None, 6, D), lambda b, i: (b, (i >= n_lat_tiles).astype(jnp.int32), 0, 0)),
                 pl.BlockSpec((1, D), lambda b, i: (0, 0)),
                 full(wdn), pl.BlockSpec((1, MLA_Q_RANK), lambda b, i: (0, 0)),
                 pl.BlockSpec((1, MLA_KV_RANK), lambda b, i: (0, 0)), full(wuq), full(wukv),
                 pl.BlockSpec((TM, dr), lambda b, i: (i, 0)), pl.BlockSpec((TM, dr), lambda b, i: (i, 0))]
    headed = lambda w: pl.BlockSpec((None, H, TM, w), lambda b, i: (b, 0, i, 0))
    return pl.pallas_call(
        functools.partial(_mla_proj_kernel, n_x, n_lat_tiles),
        grid=(B, nt), in_specs=in_specs,
        out_specs=[headed(MLA_DK_PAD), headed(MLA_DK_PAD), headed(dv)],
        out_shape=[jax.ShapeDtypeStruct((B, H, T, MLA_DK_PAD), BF16), jax.ShapeDtypeStruct((B, H, T, MLA_DK_PAD), BF16),
                   jax.ShapeDtypeStruct((B, H, T, dv), BF16)],
        compiler_params=_params(("parallel", "arbitrary"), 40),
        name="mla_proj",
    )(*args, mod, norm_g.reshape(1, D), wdn, q_norm.reshape(1, -1), kv_norm.reshape(1, -1), wuq, wukv,
      jnp.asarray(cos), jnp.asarray(sin))


ATT_HEADS_PER_STEP = 4
ATT_KEY_SLAB = 768


def _attn_kernel(q_ref, k_ref, v_ref, o_ref, va_s):
    hp, T, dv = v_ref.shape

    @pl.when(pl.program_id(2) == 0)
    def _():
        ones_col = (lax.broadcasted_iota(jnp.int32, (T, dv), 1) == 0).astype(BF16)
        for j in range(hp):
            va_s[j] = jnp.concatenate([v_ref[j], ones_col], axis=1)

    kw = max(w for w in range(LANES, ATT_KEY_SLAB + 1, LANES) if T % w == 0)
    slabs = [slice(c * kw, (c + 1) * kw) for c in range(T // kw)]
    nt = (((1,), (1,)), ((), ()))
    s = [[lax.dot_general(q_ref[j], k_ref[j, sl, :], nt, preferred_element_type=F32) for sl in slabs]
         for j in range(hp)]
    for j in range(hp):
        m = functools.reduce(jnp.maximum, [jnp.max(sc, axis=-1, keepdims=True) for sc in s[j]])
        na = None
        for sc, sl in zip(s[j], slabs):
            t = _bdot(jnp.exp2(sc - m).astype(BF16), va_s[j, sl, :])
            na = t if na is None else na + t
        o_ref[:, j * dv:(j + 1) * dv] = (na[:, :dv] * (1.0 / na[:, dv:dv + 1])).astype(o_ref.dtype)


def _attention(q, k, v, S):
    B, H, T, dk = k.shape
    dv = v.shape[3]
    tq = min(S, 512)
    hp = ATT_HEADS_PER_STEP
    return pl.pallas_call(
        _attn_kernel,
        grid=(B, H // hp, S // tq),
        in_specs=[pl.BlockSpec((None, hp, tq, dk), lambda b, h, i: (b, h, i, 0)),
                  pl.BlockSpec((None, hp, T, dk), lambda b, h, i: (b, h, 0, 0)),
                  pl.BlockSpec((None, hp, T, dv), lambda b, h, i: (b, h, 0, 0))],
        out_specs=pl.BlockSpec((None, tq, hp * dv), lambda b, h, i: (b, i, h)),
        out_shape=jax.ShapeDtypeStruct((B, S, H * dv), BF16),
        scratch_shapes=[pltpu.VMEM((hp, T, 2 * dv), BF16)],
        compiler_params=_params(("parallel", "parallel", "arbitrary"), 48),
        name="mla_attention",
    )(q, k, v)


def kernel(x, c, ctx, c_ctx, ada_w_0, ada_b_0, norm_mix_0, norm_ffn_0, w_in_0, hy_conv_w, hy_conv_b, hy_fw1, hy_fb1, hy_fw2, hy_fb2, hy_fw3, hy_freq, hy_bias, ml_conv_w, ml_gate_b, ml_norm_g, w_out_0, ffn_up_0, ffn_conv_w_0, ffn_conv_b_0, ffn_down_0, ada_w_1, ada_b_1, norm_mix_1, norm_ffn_1, mla_w_down, mla_q_norm, mla_kv_norm, mla_w_uq, mla_w_ukv, mla_w_o, ffn_up_1, ffn_conv_w_1, ffn_conv_b_1, ffn_down_1, final_norm):
    B, S, D = x.shape
    CL = ctx.shape[1]
    T = S + CL
    assert S % TM == 0 and CL % TM == 0 and S % CL == 0 and S % GRID_W == 0 and TM == ML_BLOCK
    nlt = S // TM
    C = HY_WIDTH
    W = ML_WIDTH
    H = ML_HEADS

    mod0 = _modulation(c, c_ctx, ada_w_0, ada_b_0)
    mod1 = _modulation(c, c_ctx, ada_w_1, ada_b_1)

    wi = jnp.pad(w_in_0, ((0, 0), (0, LANES - 4 * H))).astype(BF16)
    b_gate = jnp.zeros((1, LANES), F32).at[0, :4 * H].set(ml_gate_b.astype(F32))
    hx0, hx1, hv, mq, mk, mv, mog, gates = _in_proj((x, ctx), mod0, norm_mix_0, wi, b_gate,
                                                    [C] * 3 + [W] * 4 + [LANES], T, nlt)

    filt = (hy_fw1, hy_fb1, hy_fw2, hy_fb2, hy_fw3, hy_freq, hy_bias)
    hcw = jnp.transpose(hy_conv_w.reshape(SHORT_W, 3, C), (1, 0, 2)).astype(F32)
    hcb = hy_conv_b.reshape(3, C).astype(F32)
    hy = []
    for L, blk in ((S, 0), (CL, S // CL)):
        R = _hyena_radix(L)
        wf_np, wft_np = _dft_tables(L // R)
        wf = jnp.asarray(wf_np).astype(BF16)
        wft = jnp.asarray(wft_np).astype(BF16)
        spectra = _hyena_filter_spectrum(L, R, wf, _hyena_filter_lags(L, *filt))
        hy.append(_hyena_mix(hx0, hx1, hv, hcw, hcb, wf, wft, spectra, L, R, blk))

    nc = T // ML_BLOCK
    g4 = gates[:, :, :4 * H].reshape(B, nc, ML_BLOCK, 4, H)
    gcol = jnp.transpose(g4, (0, 4, 2, 1, 3)).reshape(B, H, ML_BLOCK, nc * 4)
    grow = jnp.transpose(g4, (0, 4, 1, 3, 2))
    mcw = ml_conv_w.astype(F32)
    ml = _mlstm_mix(mq, mk, mv, mog, gcol, grow, mcw[:, :W], mcw[:, W:], ml_norm_g.reshape(1, W).astype(F32), S)

    tml = TAIL_TM_LATENT if S % TAIL_TM_LATENT == 0 else TM
    wo = [w_out_0[:C].astype(BF16), w_out_0[C:].astype(BF16)]
    ffn0 = (mod0, norm_ffn_0, ffn_up_0, ffn_conv_w_0, ffn_conv_b_0, ffn_down_0, final_norm)
    xs_lat = _mixer_tail(x, [hy[0], ml], wo, *ffn0, S, S // tml, False, tml)
    xs_ctx = _mixer_tail(ctx, [hy[1], ml[:, S:]], wo, *ffn0, CL, 0, False, TM)

    q, k, v = _mla_proj((xs_lat, xs_ctx), mod1, norm_mix_1, mla_w_down, mla_q_norm, mla_kv_norm, mla_w_uq,
                        mla_w_ukv, S, T)
    att = _attention(q, k, v, S)
    return _mixer_tail(xs_lat, [att], [mla_w_o.astype(BF16)], mod1, norm_ffn_1, ffn_up_1, ffn_conv_w_1,
                       ffn_conv_b_1, ffn_down_1, final_norm, S, S // tml, True, tml)
```

```python
import functools
import math

import numpy as np
import jax
import jax.numpy as jnp
from jax import lax
from jax.experimental import pallas as pl
from jax.experimental.pallas import tpu as pltpu

F32 = jnp.float32
BF16 = jnp.bfloat16
HI = lax.Precision.HIGHEST

RMS_EPS = 1e-6
GRID_W = 64
SHORT_W = 3
HY_WIDTH = 512
HY_EMB = 33
HY_FAST = 0.3
HY_SLOW = 1.5
HY_TARGET = 1e-2
HY_SHIFT = 0.05
ML_HEADS = 4
ML_HEAD_DIM = 128
ML_WIDTH = ML_HEADS * ML_HEAD_DIM
ML_BLOCK = 256
MLA_HEADS = 8
MLA_NOPE = 128
MLA_ROPE = 64
MLA_V = 128
MLA_Q_RANK = 384
MLA_KV_RANK = 256
MLA_SCALE = (MLA_NOPE + MLA_ROPE) ** -0.5
MLA_DK_PAD = 256
ROPE_AXIS = MLA_ROPE // 2
ROPE_BASE = 10000.0

TM = 256
LANES = 128
MIB = 1024 * 1024


def _params(sem, vmem_mib):
    return pltpu.CompilerParams(dimension_semantics=sem, vmem_limit_bytes=vmem_mib * MIB)


def _sigmoid(x):
    return 0.5 * jnp.tanh(0.5 * x) + 0.5


def _silu(x):
    return x * _sigmoid(x)


def _log_sigmoid(x):
    return jnp.minimum(x, 0.0) - jnp.log(1.0 + jnp.exp(-jnp.abs(x)))


def _rms(x):
    return x * lax.rsqrt(jnp.mean(x * x, axis=-1, keepdims=True) + RMS_EPS)


def _bdot(a, b):
    return jnp.dot(a, b, preferred_element_type=F32)


def _ada_kernel(cv_ref, w_ref, b_ref, o_ref):
    s = _silu(cv_ref[...])
    o_ref[...] = jnp.dot(s, w_ref[...], preferred_element_type=F32, precision=HI) + b_ref[...]


def _ada(cv, w, b):
    R, D = cv.shape
    N = w.shape[1]
    tn = N // 4
    return pl.pallas_call(
        _ada_kernel,
        grid=(N // tn,),
        in_specs=[pl.BlockSpec((R, D), lambda j: (0, 0)),
                  pl.BlockSpec((D, tn), lambda j: (0, j)),
                  pl.BlockSpec((1, tn), lambda j: (0, j))],
        out_specs=pl.BlockSpec((R, tn), lambda j: (0, j)),
        out_shape=jax.ShapeDtypeStruct((R, N), F32),
        compiler_params=_params(("arbitrary",), 40),
        name="ada_mod",
    )(cv, w, b.reshape(1, N))


def _modulation(c, c_ctx, w, b):
    B, D = c.shape
    R = -(-(B + 1) // 8) * 8
    cv = jnp.concatenate([c, c_ctx[None, :], jnp.zeros((R - B - 1, D), F32)], axis=0)
    m = _ada(cv, w, b)
    lat = m[:B].reshape(B, 1, 6, D)
    cx = jnp.broadcast_to(m[B].reshape(1, 1, 6, D), (B, 1, 6, D))
    return jnp.concatenate([lat, cx], axis=1)


def _modnorm(x, mod, g, si):
    sh = mod[si:si + 1, :]
    sc = mod[si + 1:si + 2, :]
    return (_rms(x) * g) * (1.0 + sc) + sh


def _row_specs(a, n_lat_tiles):
    if isinstance(a, tuple):
        w = a[0].shape[2]
        return ([pl.BlockSpec((None, TM, w), lambda b, i: (b, jnp.minimum(i, n_lat_tiles - 1), 0)),
                 pl.BlockSpec((None, TM, w), lambda b, i: (b, jnp.maximum(i - n_lat_tiles, 0), 0))], list(a))
    return [pl.BlockSpec((None, TM, a.shape[2]), lambda b, i: (b, i, 0))], [a]


def _load_rows(refs, n_lat_tiles):
    if len(refs) == 1:
        return refs[0][...]
    return jnp.where(pl.program_id(1) >= n_lat_tiles, refs[1][...], refs[0][...])


def _proj_kernel(n_x, widths, si, n_lat_tiles, *refs):
    x_refs = refs[:n_x]
    mod_ref, g_ref, w_ref, gb_ref = refs[n_x:n_x + 4]
    o_refs = refs[n_x + 4:]
    h = _modnorm(_load_rows(x_refs, n_lat_tiles), mod_ref[...], g_ref[...], si).astype(BF16)
    off = 0
    for j, wd in enumerate(widths):
        y = _bdot(h, w_ref[:, off:off + wd])
        off += wd
        if j == len(widths) - 1:
            y = _mlstm_gate_prep(y + gb_ref[...])
        o_refs[j][...] = y.astype(o_refs[j].dtype)


def _mlstm_gate_prep(g):
    n = g.shape[0]
    H = ML_HEADS
    lf = _log_sigmoid(g)
    t = lax.broadcasted_iota(jnp.int32, (n, n), 0)
    s = lax.broadcasted_iota(jnp.int32, (n, n), 1)
    hi = lf.astype(BF16)
    r1 = lf - hi.astype(F32)
    mid = r1.astype(BF16)
    parts = jnp.concatenate([hi, mid, (r1 - mid.astype(F32)).astype(BF16)], axis=1)
    w = g.shape[1]
    pre3 = _bdot((s <= t).astype(BF16), parts)
    suf3 = _bdot((s >= t).astype(BF16), parts)
    pre = pre3[:, :w] + pre3[:, w:2 * w] + pre3[:, 2 * w:]
    suf = suf3[:, :w] + suf3[:, w:2 * w] + suf3[:, 2 * w:]
    col = lax.broadcasted_iota(jnp.int32, (1, g.shape[1]), 1)
    g = jnp.where((col >= H) & (col < 2 * H), pre, g)
    return jnp.where((col >= 3 * H) & (col < 4 * H), suf, g)


def _in_proj(x, mod, g, w, gate_b, widths, n_rows, n_lat_tiles):
    B, _, _, D = mod.shape
    nt = n_rows // TM
    in_specs, args = _row_specs(x, n_lat_tiles)
    n_x = len(args)
    in_specs += [pl.BlockSpec((None, None, 6, D), lambda b, i: (b, (i >= n_lat_tiles).astype(jnp.int32), 0, 0)),
                 pl.BlockSpec((1, D), lambda b, i: (0, 0)),
                 pl.BlockSpec(w.shape, lambda b, i: (0, 0)),
                 pl.BlockSpec(gate_b.shape, lambda b, i: (0, 0))]
    args += [mod, g.reshape(1, D), w, gate_b]
    dtypes = [BF16] * (len(widths) - 1) + [F32]
    out_specs = [pl.BlockSpec((None, TM, wd), lambda b, i: (b, i, 0)) for wd in widths]
    out_shape = [jax.ShapeDtypeStruct((B, n_rows, wd), dt) for wd, dt in zip(widths, dtypes)]
    return pl.pallas_call(
        functools.partial(_proj_kernel, n_x, tuple(widths), 0, n_lat_tiles),
        grid=(B, nt), in_specs=in_specs, out_specs=out_specs, out_shape=out_shape,
        compiler_params=_params(("parallel", "arbitrary"), 48),
        name="modnorm_proj",
    )(*args)


def _filter_tables(L):
    pos = np.arange(L, dtype=np.float64)
    t = pos / (L - 1)
    bands = (HY_EMB - 1) // 2
    f = np.linspace(1e-4, bands - 1, bands)
    ang = (2.0 * math.pi / L) * pos[:, None] * f[None, :]
    z = np.concatenate([t[:, None], np.cos(ang), -np.sin(ang)], axis=-1)
    zp = np.zeros((2 * L, LANES), np.float64)
    zp[L:, :HY_EMB] = z
    zp[1:L, :HY_EMB] = z[:0:-1]
    return zp.astype(np.float32)


def _filt_mlp_kernel(L, tl, z_ref, w1, b1, w2, b2, w3, fq, absd, bias_ref, o_ref):
    i = pl.program_id(0)
    z = z_ref[...]
    f = fq[...]
    h = jnp.sin(f * (jnp.dot(z, w1[...], preferred_element_type=F32, precision=HI) + b1[...]))
    h = jnp.sin(f * (jnp.dot(h, w2[...], preferred_element_type=F32, precision=HI) + b2[...]))
    h = jnp.dot(h, w3[...], preferred_element_type=F32, precision=HI)
    t = z[:, 0:1]
    h = h * (jnp.exp(-t * absd[...]) + HY_SHIFT)
    lag = lax.broadcasted_iota(jnp.int32, (tl, 1), 0) + (i * tl - L)
    o_ref[...] = jnp.where(lag == -L, 0.0, jnp.where(lag == 0, h + bias_ref[...], h))


def _pad2(a, r, c):
    return jnp.zeros((r, c), F32).at[:a.shape[0], :a.shape[1]].set(a.astype(F32))


def _hyena_filter_lags(L, fw1, fb1, fw2, fb2, fw3, freq, hy_bias):
    tl = min(L, 512)
    nl = L // tl
    C = HY_WIDTH
    deltas = np.linspace(math.log(HY_TARGET) / HY_FAST, math.log(HY_TARGET) / HY_SLOW, C)
    absd = jnp.asarray(np.abs(np.tile(deltas, 2))[None, :].astype(np.float32))
    z = jnp.asarray(_filter_tables(L))
    full = lambda shape: pl.BlockSpec(shape, lambda i: (0, 0))
    half = lambda rows: pl.BlockSpec((rows, C), lambda i: (0, (i < nl).astype(jnp.int32)))
    return pl.pallas_call(
        functools.partial(_filt_mlp_kernel, L, tl),
        grid=(2 * nl,),
        in_specs=[pl.BlockSpec((tl, LANES), lambda i: (i, 0)),
                  full((LANES, LANES)), full((1, LANES)), full((LANES, LANES)), full((1, LANES)),
                  half(LANES), full((1, LANES)), half(1), full((1, C))],
        out_specs=pl.BlockSpec((tl, C), lambda i: (i, 0)),
        out_shape=jax.ShapeDtypeStruct((2 * L, C), F32),
        compiler_params=_params(("arbitrary",), 32),
        name="hyena_filter_mlp",
    )(z, _pad2(fw1, LANES, LANES), _pad2(fb1[None, :], 1, LANES), _pad2(fw2, LANES, LANES),
      _pad2(fb2[None, :], 1, LANES), _pad2(fw3, LANES, 2 * C), _pad2(freq[None, :], 1, LANES), absd,
      hy_bias.reshape(1, C).astype(F32))


def _dft_tables(L):
    n = 2 * L
    f = np.arange(L, dtype=np.int64)[:, None]
    s = np.arange(L, dtype=np.int64)[None, :]
    ang = (2.0 * math.pi / n) * ((f * s) % n).astype(np.float64)
    wc = np.cos(ang)
    ws = np.sin(ang)
    ws[0, :] = np.where(np.arange(L) % 2 == 0, 1.0, -1.0)
    w = np.concatenate([wc, ws], axis=0).astype(np.float32)
    return w, np.ascontiguousarray(w.T)


def _hyena_radix(L):
    return max(r for r in (4, 2, 1) if L % r == 0 and L // r >= 256)


def _filt_dft_kernel(M, wf_ref, k1_ref, k0_ref, p_ref, q_ref, r_ref):
    row = lax.broadcasted_iota(jnp.int32, (M, 1), 0)
    row0 = row == 0
    wf = wf_ref[...]
    t1 = _bdot(wf, k1_ref[...].astype(BF16))
    t0 = _bdot(wf, jnp.where(row0, 0.0, k0_ref[...]).astype(BF16))
    sgn = jnp.where((row & 1) == 1, -1.0, 1.0)
    sc = jnp.where(row0, 0.5 / M, 1.0 / M)
    ka = (t1[:M] + sgn * t0[:M]) * sc
    kb = (t1[M:] + sgn * t0[M:]) * sc
    p_ref[...] = ka
    q_ref[...] = jnp.where(row0, 0.0, kb)
    r_ref[...] = jnp.where(row0, kb, ka)


def _hyena_filter_spectrum(L, R, wf, lags):
    C = HY_WIDTH
    M = L // R
    lags = lags.reshape(2 * R, M, C)
    nd = 2 * R - 1
    out = pl.BlockSpec((None, M, C), lambda d: (d, 0, 0))
    return pl.pallas_call(
        functools.partial(_filt_dft_kernel, M),
        grid=(nd,),
        in_specs=[pl.BlockSpec((2 * M, M), lambda d: (0, 0)),
                  pl.BlockSpec((None, M, C), lambda d: (d + 1, 0, 0)),
                  pl.BlockSpec((None, M, C), lambda d: (d, 0, 0))],
        out_specs=[out, out, out],
        out_shape=[jax.ShapeDtypeStruct((nd, M, C), F32)] * 3,
        compiler_params=_params(("arbitrary",), 32),
        name="hyena_filter_dft",
    )(wf, lags, lags)


def _conv3_rows(x, w, first, last):
    n = x.shape[0]
    xm = jnp.where(first, 0.0, pltpu.roll(x, 1, 0))
    xp = jnp.where(last, 0.0, pltpu.roll(x, n - 1, 0))
    return w[0:1, :] * xm + w[1:2, :] * x + w[2:3, :] * xp


def _hyena_kernel(L, R, x0_ref, x1_ref, v_ref, cw_ref, cb_ref, wf_ref, wft_ref, p_ref, q_ref, r_ref, o_ref):
    M = L // R
    row = lax.broadcasted_iota(jnp.int32, (L, 1), 0)
    first = row == 0
    last = row == L - 1
    x1 = _conv3_rows(x1_ref[...].astype(F32), cw_ref[1], first, last) + cb_ref[1:2, :]
    v = _conv3_rows(v_ref[...].astype(F32), cw_ref[2], first, last) + cb_ref[2:3, :]
    g = (v * x1).astype(BF16)
    x0 = _conv3_rows(x0_ref[...].astype(F32), cw_ref[0], first, last) + cb_ref[0:1, :]
    wf = wf_ref[...]
    spec = [_bdot(wf, g[j * M:(j + 1) * M]) for j in range(R)]
    for i in range(R):
        yr = yb = None
        for j in range(R):
            a, b = spec[j][:M], spec[j][M:]
            d = i - j + R - 1
            p, q, r = p_ref[d], q_ref[d], r_ref[d]
            tr = a * p - b * q
            tb = a * q + b * r
            yr = tr if yr is None else yr + tr
            yb = tb if yb is None else yb + tb
        y = _bdot(wft_ref[...], jnp.concatenate([yr, yb], axis=0).astype(BF16))
        o_ref[i * M:(i + 1) * M, :] = (y * x0[i * M:(i + 1) * M]).astype(o_ref.dtype)


def _hyena_mix(x0, x1, v, cw, cb, wf, wft, spectra, L, R, row_block):
    B, T, C = x0.shape
    M = L // R
    cbw = 256
    nd = 2 * R - 1
    in_specs = [pl.BlockSpec((None, L, cbw), lambda c, b: (b, row_block, c))] * 3 + [
        pl.BlockSpec((3, SHORT_W, cbw), lambda c, b: (0, 0, c)),
        pl.BlockSpec((3, cbw), lambda c, b: (0, c)),
        pl.BlockSpec((2 * M, M), lambda c, b: (0, 0)),
        pl.BlockSpec((M, 2 * M), lambda c, b: (0, 0)),
    ] + [pl.BlockSpec((nd, M, cbw), lambda c, b: (0, 0, c))] * 3
    return pl.pallas_call(
        functools.partial(_hyena_kernel, L, R),
        grid=(C // cbw, B),
        in_specs=in_specs,
        out_specs=pl.BlockSpec((None, L, cbw), lambda c, b: (b, 0, c)),
        out_shape=jax.ShapeDtypeStruct((B, L, C), BF16),
        compiler_params=_params(("parallel", "arbitrary"), 56),
        name="hyena_mix",
    )(x0, x1, v, cw, cb, wf, wft, *spectra)


def _mlstm_kernel(S, T, q_ref, k_ref, v_ref, og_ref, gcol_ref, grow_ref, cwq_ref, cwk_ref, ng_ref, o_ref,
                  k_s, qt_s, vat_s, hf_s, hb_s, na_s, dc_s, ml_s):
    d = ML_HEAD_DIM
    lc = ML_BLOCK
    nc = T // lc
    ncl = S // lc
    row = lax.broadcasted_iota(jnp.int32, (T, 1), 0)
    first = (row == 0) | (row == S)
    last = (row == S - 1) | (row == T - 1)
    qc = _silu(_conv3_rows(q_ref[...].astype(F32), cwq_ref[...], first, last)) * (d ** -0.5)
    k_s[...] = _silu(_conv3_rows(k_ref[...].astype(F32), cwk_ref[...], first, last)).astype(BF16)
    ones_row = (lax.broadcasted_iota(jnp.int32, (d, lc), 0) == 0).astype(F32)
    for c in range(nc):
        qt_s[c] = qc[c * lc:(c + 1) * lc, :].T.astype(BF16)
        vt = v_ref[c * lc:(c + 1) * lc, :].astype(F32).T
        vat_s[c] = jnp.concatenate([vt, ones_row], axis=0).astype(BF16)

    si = lax.broadcasted_iota(jnp.int32, (lc, lc), 0)
    ti = lax.broadcasted_iota(jnp.int32, (lc, lc), 1)
    past = si <= ti
    future = si >= ti

    scans = ((0, past, lc - 1), (2, future, 0))

    b_tot = [[None] * nc for _ in scans]
    g_max = [[None] * nc for _ in scans]
    for c in range(nc):
        k = k_s[c * lc:(c + 1) * lc, :]
        vat = vat_s[c]
        st = _bdot(k, qt_s[c])
        vatf = vat.astype(F32)
        gc = gcol_ref[:, 4 * c:4 * c + 4]
        gr = grow_ref[c]
        for dn, (kind, mask, end) in enumerate(scans):
            r_col = gc[:, kind:kind + 1] - gc[:, kind + 1:kind + 2]
            b_row = gr[kind + 1:kind + 2, :]
            r_row = gr[kind:kind + 1, :] - b_row
            dlog = jnp.where(mask, r_col + b_row, -jnp.inf)
            m_loc = jnp.max(dlog, axis=0, keepdims=True)
            na_s[dn, c] = _bdot(vat, (st * jnp.exp(dlog - m_loc)).astype(BF16))
            ml_s[dn, c] = m_loc
            b_tot[dn][c] = b_row[:, end:end + 1]
            gs_row = b_tot[dn][c] + r_row
            g_max[dn][c] = jnp.max(gs_row, axis=1, keepdims=True)
            dc_s[dn, c] = _bdot((vatf * jnp.exp(gs_row - g_max[dn][c])).astype(BF16), k)

    def advance(dn, c, cs, m_prev):
        kind = scans[dn][0]
        m_loc = ml_s[dn, c]
        inter = grow_ref[c][kind + 1:kind + 2, :] + m_prev
        m_t = jnp.maximum(inter, m_loc)
        e_in = jnp.exp(m_loc - m_t)
        e_st = jnp.exp(inter - m_t)
        n_in = na_s[dn, c]
        n_st = _bdot(cs.astype(BF16), qt_s[c])
        den = e_in * n_in[d:d + 1, :] + e_st * n_st[d:d + 1, :]
        inv = 1.0 / jnp.maximum(jnp.abs(den), jnp.exp(-m_t))
        ht = (e_in * inv) * n_in[:d, :] + (e_st * inv) * n_st[:d, :]
        m_new = jnp.maximum(b_tot[dn][c] + m_prev, g_max[dn][c])
        cs = jnp.exp(b_tot[dn][c] + m_prev - m_new) * cs + jnp.exp(g_max[dn][c] - m_new) * dc_s[dn, c]
        return ht, cs, m_new

    c_f = c_b = jnp.zeros((2 * d, d), F32)
    m_f = m_b = jnp.zeros((1, 1), F32)
    for i in range(nc):
        cf, cb = (i + ncl) % nc, nc - 1 - i
        hf_s[cf], c_f, m_f = advance(0, cf, c_f, m_f)
        hb_s[cb], c_b, m_b = advance(1, cb, c_b, m_b)

    for c in range(nc):
        h = (hf_s[c] + hb_s[c]).T
        og = og_ref[c * lc:(c + 1) * lc, :].astype(F32)
        o_ref[c * lc:(c + 1) * lc, :] = ((_rms(h) * ng_ref[...]) * _sigmoid(og)).astype(o_ref.dtype)


def _mlstm_mix(q, k, v, og, gcol, grow, cwq, cwk, ng, S):
    B, T, W = q.shape
    H = W // ML_HEAD_DIM
    d = ML_HEAD_DIM
    nc = T // ML_BLOCK
    head = pl.BlockSpec((None, T, d), lambda b, h: (b, 0, h))
    return pl.pallas_call(
        functools.partial(_mlstm_kernel, S, T),
        grid=(B, H),
        in_specs=[head, head, head, head,
                  pl.BlockSpec((None, None, ML_BLOCK, nc * 4), lambda b, h: (b, h, 0, 0)),
                  pl.BlockSpec((None, None, nc, 4, ML_BLOCK), lambda b, h: (b, h, 0, 0, 0)),
                  pl.BlockSpec((SHORT_W, d), lambda b, h: (0, h)),
                  pl.BlockSpec((SHORT_W, d), lambda b, h: (0, h)),
                  pl.BlockSpec((1, d), lambda b, h: (0, h))],
        out_specs=head,
        out_shape=jax.ShapeDtypeStruct((B, T, W), BF16),
        scratch_shapes=[pltpu.VMEM((T, d), BF16), pltpu.VMEM((nc, d, ML_BLOCK), BF16),
                        pltpu.VMEM((nc, 2 * d, ML_BLOCK), BF16),
                        pltpu.VMEM((nc, d, ML_BLOCK), F32), pltpu.VMEM((nc, d, ML_BLOCK), F32),
                        pltpu.VMEM((2, nc, 2 * d, ML_BLOCK), F32), pltpu.VMEM((2, nc, 2 * d, d), F32),
                        pltpu.VMEM((2, nc, 1, ML_BLOCK), F32)],
        compiler_params=_params(("parallel", "parallel"), 48),
        name="mlstm_mix",
    )(q, k, v, og, gcol, grow, cwq, cwk, ng)


FETCH = 16
HALO = 8
FFN_SPLIT = 1
TAIL_TM_LATENT = 512


def _halo_specs(a, n_lat_tiles, tm):
    fb = tm // FETCH
    if isinstance(a, tuple):
        lat, cx = a
        assert cx.shape[1] == tm
        w, nfb = lat.shape[2], lat.shape[1] // FETCH
        li = lambda i: jnp.minimum(i, n_lat_tiles - 1)
        return ([pl.BlockSpec((None, tm, w), lambda b, i: (b, li(i), 0)),
                 pl.BlockSpec((None, FETCH, w), lambda b, i: (b, jnp.maximum(li(i) * fb - 1, 0), 0)),
                 pl.BlockSpec((None, FETCH, w), lambda b, i: (b, jnp.minimum((li(i) + 1) * fb, nfb - 1), 0)),
                 pl.BlockSpec((None, tm, w), lambda b, i: (b, 0, 0))], [lat, lat, lat, cx])
    w, nfb = a.shape[2], a.shape[1] // FETCH
    return ([pl.BlockSpec((None, tm, w), lambda b, i: (b, i, 0)),
             pl.BlockSpec((None, FETCH, w), lambda b, i: (b, jnp.maximum(i * fb - 1, 0), 0)),
             pl.BlockSpec((None, FETCH, w), lambda b, i: (b, jnp.minimum((i + 1) * fb, nfb - 1), 0))], [a, a, a])


def _load_halo(refs, n_lat_tiles):
    main = refs[0][...]
    if len(refs) == 4:
        main = jnp.where(pl.program_id(1) >= n_lat_tiles, refs[3][...], main)
    return jnp.concatenate([refs[1][...], main, refs[2][...]], axis=0)


def _tail_kernel(final, n_lat_tiles, counts, mod_ref, *refs):
    groups = []
    for n in counts:
        groups.append(refs[:n])
        refs = refs[n:]
    n_act = len(counts) - 1
    wo_refs = refs[:n_act]
    g_ref, wup_ref, cw_ref, cb_ref, wd_ref, fn_ref, o_ref = refs[n_act:]
    i = pl.program_id(1)
    nt = pl.num_programs(1)
    tm = o_ref.shape[0]
    mod = mod_ref[...]
    y = None
    for a_refs, w_ref in zip(groups[1:], wo_refs):
        t = _bdot(_load_halo(a_refs, n_lat_tiles).astype(BF16), w_ref[...])
        y = t if y is None else y + t
    xe = _load_halo(groups[0], n_lat_tiles).astype(F32) + mod[2:3, :] * y
    x = xe[FETCH:FETCH + tm]
    hf = _modnorm(xe[FETCH - HALO:FETCH + tm + HALO], mod, g_ref[...], 3)
    he = hf.astype(BF16)
    h = hf[HALO:HALO + tm].astype(BF16)
    has_prev = jnp.logical_and(i != 0, i != n_lat_tiles)
    has_next = jnp.logical_and(i != n_lat_tiles - 1, i != nt - 1)
    row = lax.broadcasted_iota(jnp.int32, (tm + 2 * HALO, 1), 0)
    inside = jnp.logical_and(jnp.logical_or(row >= HALO, has_prev), jnp.logical_or(row < tm + HALO, has_next))
    fh = wd_ref.shape[0]
    fc = fh // FFN_SPLIT
    y = None
    for c in range(FFN_SPLIT):
        cols = slice(c * fc, (c + 1) * fc)
        ge = jnp.where(inside, _bdot(he, wup_ref[:, cols]), 0.0)
        gm = pltpu.roll(ge, 1, 0)[HALO:HALO + tm]
        gq = pltpu.roll(ge, tm + 2 * HALO - 1, 0)[HALO:HALO + tm]
        cw = cw_ref[:, cols]
        gc = cw[0:1, :] * gm + cw[1:2, :] * ge[HALO:HALO + tm] + cw[2:3, :] * gq + cb_ref[:, cols]
        a = (_silu(gc) * _bdot(h, wup_ref[:, fh + c * fc:fh + (c + 1) * fc])).astype(BF16)
        t = _bdot(a, wd_ref[cols, :])
        y = t if y is None else y + t
    o = x + mod[5:6, :] * y
    if final:
        o = _rms(o) * fn_ref[...]
    o_ref[...] = o


def _mixer_tail(x, acts, ws_out, mod, norm_g, w_up, cw, cb, w_down, fn, n_rows, n_lat_tiles, final, tm):
    B, _, _, D = mod.shape
    Fh = w_up.shape[1] // 2
    nt = n_rows // tm
    once = lambda shape: pl.BlockSpec(shape, lambda b, i: (0, 0), pipeline_mode=pl.Buffered(1))
    in_specs = [pl.BlockSpec((None, None, 6, D), lambda b, i: (b, (i >= n_lat_tiles).astype(jnp.int32), 0, 0))]
    args = [mod]
    counts = []
    for a in [x] + list(acts):
        sp, ar = _halo_specs(a, n_lat_tiles, tm)
        in_specs += sp
        args += ar
        counts.append(len(ar))
    in_specs += [once(w.shape) for w in ws_out]
    in_specs += [once((1, D)), once((D, 2 * Fh)), once((SHORT_W, Fh)), once((1, Fh)), once((Fh, D)), once((1, D))]
    return pl.pallas_call(
        functools.partial(_tail_kernel, final, n_lat_tiles, tuple(counts)),
        grid=(B, nt), in_specs=in_specs,
        out_specs=pl.BlockSpec((None, tm, D), lambda b, i: (b, i, 0)),
        out_shape=jax.ShapeDtypeStruct((B, n_rows, D), F32),
        compiler_params=_params(("parallel", "arbitrary"), 56),
        name="mixer_tail",
    )(*args, *ws_out, norm_g.reshape(1, D), w_up.astype(BF16), cw.astype(F32), cb.reshape(1, Fh).astype(F32),
      w_down.astype(BF16), fn.reshape(1, D))


def _rope_tables(S, T):
    p = np.arange(S)
    inv = ROPE_BASE ** (-np.arange(0, ROPE_AXIS, 2, dtype=np.float64) / ROPE_AXIS)
    ar = (p // GRID_W)[:, None] * inv[None, :]
    ac = (p % GRID_W)[:, None] * inv[None, :]
    cos = np.concatenate([np.cos(ar), np.cos(ar), np.cos(ac), np.cos(ac)], axis=1)
    sin = np.concatenate([-np.sin(ar), np.sin(ar), -np.sin(ac), np.sin(ac)], axis=1)
    cos = np.concatenate([cos, np.ones((T - S, MLA_ROPE))], axis=0)
    sin = np.concatenate([sin, np.zeros((T - S, MLA_ROPE))], axis=0)
    return cos.astype(np.float32), sin.astype(np.float32)


def _mla_proj_kernel(n_x, n_lat_tiles, *refs):
    x_refs = refs[:n_x]
    mod_ref, g_ref, wdn_ref, qn_ref, kvn_ref, wuq_ref, wukv_ref, cos_ref, sin_ref, qo_ref, ko_ref, vo_ref = refs[n_x:]
    H, dn, dr = MLA_HEADS, MLA_NOPE, MLA_ROPE
    h = _modnorm(_load_rows(x_refs, n_lat_tiles), mod_ref[...], g_ref[...], 0).astype(BF16)
    dnp = _bdot(h, wdn_ref[...])
    qa = dnp[:, :MLA_Q_RANK]
    kva = dnp[:, MLA_Q_RANK:MLA_Q_RANK + MLA_KV_RANK]
    kr2 = dnp[:, MLA_Q_RANK + MLA_KV_RANK:]
    q = _bdot((_rms(qa) * qn_ref[...]).astype(BF16), wuq_ref[...])
    kv = _bdot((_rms(kva) * kvn_ref[...]).astype(BF16), wukv_ref[...])
    cosr = cos_ref[...]
    sinr = sin_ref[...]
    kr = kr2[:, :dr] * cosr + kr2[:, dr:] * sinr
    zpad = jnp.zeros((kr.shape[0], MLA_DK_PAD - dn - dr), F32)
    ro = H * dn
    for hd in range(H):
        qr = q[:, ro + hd * dr:ro + (hd + 1) * dr] * cosr + q[:, ro + H * dr + hd * dr:ro + H * dr + (hd + 1) * dr] * sinr
        qh = jnp.concatenate([q[:, hd * dn:(hd + 1) * dn], qr, zpad], axis=1) * (MLA_SCALE * math.log2(math.e))
        qo_ref[hd] = qh.astype(BF16)
        ko_ref[hd] = jnp.concatenate([kv[:, hd * dn:(hd + 1) * dn], kr, zpad], axis=1).astype(BF16)
        vo_ref[hd] = kv[:, ro + hd * MLA_V:ro + (hd + 1) * MLA_V].astype(BF16)


def _mla_proj(x, mod, norm_g, w_down, q_norm, kv_norm, w_uq, w_ukv, S, T):
    B, _, _, D = mod.shape
    H, dn, dr, dv = MLA_HEADS, MLA_NOPE, MLA_ROPE, MLA_V
    dk = dn + dr
    nt = T // TM
    n_lat_tiles = S // TM
    swap = np.arange(dr) ^ (ROPE_AXIS // 2)
    wkr = w_down[:, MLA_Q_RANK + MLA_KV_RANK:]
    wdn = jnp.concatenate([w_down, wkr[:, swap]], axis=1).astype(BF16)
    wq = w_uq.reshape(MLA_Q_RANK, H, dk)
    wq_n = wq[:, :, :dn].reshape(MLA_Q_RANK, H * dn)
    wq_r = wq[:, :, dn:]
    wuq = jnp.concatenate([wq_n, wq_r.reshape(MLA_Q_RANK, H * dr), wq_r[:, :, swap].reshape(MLA_Q_RANK, H * dr)],
                          axis=1).astype(BF16)
    wkv = w_ukv.reshape(MLA_KV_RANK, H, dn + dv)
    wukv = jnp.concatenate([wkv[:, :, :dn].reshape(MLA_KV_RANK, H * dn), wkv[:, :, dn:].reshape(MLA_KV_RANK, H * dv)],
                           axis=1).astype(BF16)
    cos, sin = _rope_tables(S, T)
    full = lambda a: pl.BlockSpec(a.shape, lambda b, i: (0,) * a.ndim)
    in_specs, args = _row_specs(x, n_lat_tiles)
    n_x = len(args)
    in_specs += [pl.BlockSpec((None, None, 6, D), lambda b, i: (b, (i >= n_lat_tiles).astype(jnp.int32), 0, 0)),
                 pl.BlockSpec((1, D), lambda b, i: (0, 0)),
                 full(wdn), pl.BlockSpec((1, MLA_Q_RANK), lambda b, i: (0, 0)),
                 pl.BlockSpec((1, MLA_KV_RANK), lambda b, i: (0, 0)), full(wuq), full(wukv),
                 pl.BlockSpec((TM, dr), lambda b, i: (i, 0)), pl.BlockSpec((TM, dr), lambda b, i: (i, 0))]
    headed = lambda w: pl.BlockSpec((None, H, TM, w), lambda b, i: (b, 0, i, 0))
    return pl.pallas_call(
        functools.partial(_mla_proj_kernel, n_x, n_lat_tiles),
        grid=(B, nt), in_specs=in_specs,
        out_specs=[headed(MLA_DK_PAD), headed(MLA_DK_PAD), headed(dv)],
        out_shape=[jax.ShapeDtypeStruct((B, H, T, MLA_DK_PAD), BF16), jax.ShapeDtypeStruct((B, H, T, MLA_DK_PAD), BF16),
                   jax.ShapeDtypeStruct((B, H, T, dv), BF16)],
        compiler_params=_params(("parallel", "arbitrary"), 40),
        name="mla_proj",
    )(*args, mod, norm_g.reshape(1, D), wdn, q_norm.reshape(1, -1), kv_norm.reshape(1, -1), wuq, wukv,
      jnp.asarray(cos), jnp.asarray(sin))


ATT_HEADS_PER_STEP = 4
ATT_KEY_SLAB = 768


def _attn_kernel(q_ref, k_ref, v_ref, o_ref, va_s):
    hp, T, dv = v_ref.shape

    @pl.when(pl.program_id(2) == 0)
    def _():
        ones_col = (lax.broadcasted_iota(jnp.int32, (T, dv), 1) == 0).astype(BF16)
        for j in range(hp):
            va_s[j] = jnp.concatenate([v_ref[j], ones_col], axis=1)

    kw = max(w for w in range(LANES, ATT_KEY_SLAB + 1, LANES) if T % w == 0)
    slabs = [slice(c * kw, (c + 1) * kw) for c in range(T // kw)]
    nt = (((1,), (1,)), ((), ()))
    s = [[lax.dot_general(q_ref[j], k_ref[j, sl, :], nt, preferred_element_type=F32) for sl in slabs]
         for j in range(hp)]
    for j in range(hp):
        m = functools.reduce(jnp.maximum, [jnp.max(sc, axis=-1, keepdims=True) for sc in s[j]])
        na = None
        for sc, sl in zip(s[j], slabs):
            t = _bdot(jnp.exp2(sc - m).astype(BF16), va_s[j, sl, :])
            na = t if na is None else na + t
        o_ref[:, j * dv:(j + 1) * dv] = (na[:, :dv] * (1.0 / na[:, dv:dv + 1])).astype(o_ref.dtype)


def _attention(q, k, v, S):
    B, H, T, dk = k.shape
    dv = v.shape[3]
    tq = min(S, 512)
    hp = ATT_HEADS_PER_STEP
    return pl.pallas_call(
        _attn_kernel,
        grid=(B, H // hp, S // tq),
        in_specs=[pl.BlockSpec((None, hp, tq, dk), lambda b, h, i: (b, h, i, 0)),
                  pl.BlockSpec((None, hp, T, dk), lambda b, h, i: (b, h, 0, 0)),
                  pl.BlockSpec((None, hp, T, dv), lambda b, h, i: (b, h, 0, 0))],
        out_specs=pl.BlockSpec((None, tq, hp * dv), lambda b, h, i: (b, i, h)),
        out_shape=jax.ShapeDtypeStruct((B, S, H * dv), BF16),
        scratch_shapes=[pltpu.VMEM((hp, T, 2 * dv), BF16)],
        compiler_params=_params(("parallel", "parallel", "arbitrary"), 48),
        name="mla_attention",
    )(q, k, v)


def kernel(x, c, ctx, c_ctx, ada_w_0, ada_b_0, norm_mix_0, norm_ffn_0, w_in_0, hy_conv_w, hy_conv_b, hy_fw1, hy_fb1, hy_fw2, hy_fb2, hy_fw3, hy_freq, hy_bias, ml_conv_w, ml_gate_b, ml_norm_g, w_out_0, ffn_up_0, ffn_conv_w_0, ffn_conv_b_0, ffn_down_0, ada_w_1, ada_b_1, norm_mix_1, norm_ffn_1, mla_w_down, mla_q_norm, mla_kv_norm, mla_w_uq, mla_w_ukv, mla_w_o, ffn_up_1, ffn_conv_w_1, ffn_conv_b_1, ffn_down_1, final_norm):
    B, S, D = x.shape
    CL = ctx.shape[1]
    T = S + CL
    assert S % TM == 0 and CL % TM == 0 and S % CL == 0 and S % GRID_W == 0 and TM == ML_BLOCK
    nlt = S // TM
    C = HY_WIDTH
    W = ML_WIDTH
    H = ML_HEADS

    mod0 = _modulation(c, c_ctx, ada_w_0, ada_b_0)
    mod1 = _modulation(c, c_ctx, ada_w_1, ada_b_1)

    wi = jnp.pad(w_in_0, ((0, 0), (0, LANES - 4 * H))).astype(BF16)
    b_gate = jnp.zeros((1, LANES), F32).at[0, :4 * H].set(ml_gate_b.astype(F32))
    hx0, hx1, hv, mq, mk, mv, mog, gates = _in_proj((x, ctx), mod0, norm_mix_0, wi, b_gate,
                                                    [C] * 3 + [W] * 4 + [LANES], T, nlt)

    filt = (hy_fw1, hy_fb1, hy_fw2, hy_fb2, hy_fw3, hy_freq, hy_bias)
    hcw = jnp.transpose(hy_conv_w.reshape(SHORT_W, 3, C), (1, 0, 2)).astype(F32)
    hcb = hy_conv_b.reshape(3, C).astype(F32)
    hy = []
    for L, blk in ((S, 0), (CL, S // CL)):
        R = _hyena_radix(L)
        wf_np, wft_np = _dft_tables(L // R)
        wf = jnp.asarray(wf_np).astype(BF16)
        wft = jnp.asarray(wft_np).astype(BF16)
        spectra = _hyena_filter_spectrum(L, R, wf, _hyena_filter_lags(L, *filt))
        hy.append(_hyena_mix(hx0, hx1, hv, hcw, hcb, wf, wft, spectra, L, R, blk))

    nc = T // ML_BLOCK
    g4 = gates[:, :, :4 * H].reshape(B, nc, ML_BLOCK, 4, H)
    gcol = jnp.transpose(g4, (0, 4, 2, 1, 3)).reshape(B, H, ML_BLOCK, nc * 4)
    grow = jnp.transpose(g4, (0, 4, 1, 3, 2))
    mcw = ml_conv_w.astype(F32)
    ml = _mlstm_mix(mq, mk, mv, mog, gcol, grow, mcw[:, :W], mcw[:, W:], ml_norm_g.reshape(1, W).astype(F32), S)

    tml = TAIL_TM_LATENT if S % TAIL_TM_LATENT == 0 else TM
    wo = [w_out_0[:C].astype(BF16), w_out_0[C:].astype(BF16)]
    ffn0 = (mod0, norm_ffn_0, ffn_up_0, ffn_conv_w_0, ffn_conv_b_0, ffn_down_0, final_norm)
    xs_lat = _mixer_tail(x, [hy[0], ml], wo, *ffn0, S, S // tml, False, tml)
    xs_ctx = _mixer_tail(ctx, [hy[1], ml[:, S:]], wo, *ffn0, CL, 0, False, TM)

    q, k, v = _mla_proj((xs_lat, xs_ctx), mod1, norm_mix_1, mla_w_down, mla_q_norm, mla_kv_norm, mla_w_uq,
                        mla_w_ukv, S, T)
    att = _attention(q, k, v, S)
    return _mixer_tail(xs_lat, [att], [mla_w_o.astype(BF16)], mod1, norm_ffn_1, ffn_up_1, ffn_conv_w_1,
                       ffn_conv_b_1, ffn_down_1, final_norm, S, S // tml, True, tml)
```

```python
import functools
import math

import numpy as np
import jax
import jax.numpy as jnp
from jax import lax
from jax.experimental import pallas as pl
from jax.experimental.pallas import tpu as pltpu

F32 = jnp.float32
BF16 = jnp.bfloat16
HI = lax.Precision.HIGHEST

RMS_EPS = 1e-6
GRID_W = 64
SHORT_W = 3
HY_WIDTH = 512
HY_EMB = 33
HY_FAST = 0.3
HY_SLOW = 1.5
HY_TARGET = 1e-2
HY_SHIFT = 0.05
ML_HEADS = 4
ML_HEAD_DIM = 128
ML_WIDTH = ML_HEADS * ML_HEAD_DIM
ML_BLOCK = 256
MLA_HEADS = 8
MLA_NOPE = 128
MLA_ROPE = 64
MLA_V = 128
MLA_Q_RANK = 384
MLA_KV_RANK = 256
MLA_SCALE = (MLA_NOPE + MLA_ROPE) ** -0.5
MLA_DK_PAD = 256
ROPE_AXIS = MLA_ROPE // 2
ROPE_BASE = 10000.0

TM = 256
LANES = 128
MIB = 1024 * 1024


def _params(sem, vmem_mib):
    return pltpu.CompilerParams(dimension_semantics=sem, vmem_limit_bytes=vmem_mib * MIB)


def _sigmoid(x):
    return 0.5 * jnp.tanh(0.5 * x) + 0.5


def _silu(x):
    return x * _sigmoid(x)


def _log_sigmoid(x):
    return jnp.minimum(x, 0.0) - jnp.log(1.0 + jnp.exp(-jnp.abs(x)))


def _rms(x):
    return x * lax.rsqrt(jnp.mean(x * x, axis=-1, keepdims=True) + RMS_EPS)


def _bdot(a, b):
    return jnp.dot(a, b, preferred_element_type=F32)


def _ada_kernel(cv_ref, w_ref, b_ref, o_ref):
    s = _silu(cv_ref[...])
    o_ref[...] = jnp.dot(s, w_ref[...], preferred_element_type=F32, precision=HI) + b_ref[...]


def _ada(cv, w, b):
    R, D = cv.shape
    N = w.shape[1]
    tn = N // 4
    return pl.pallas_call(
        _ada_kernel,
        grid=(N // tn,),
        in_specs=[pl.BlockSpec((R, D), lambda j: (0, 0)),
                  pl.BlockSpec((D, tn), lambda j: (0, j)),
                  pl.BlockSpec((1, tn), lambda j: (0, j))],
        out_specs=pl.BlockSpec((R, tn), lambda j: (0, j)),
        out_shape=jax.ShapeDtypeStruct((R, N), F32),
        compiler_params=_params(("arbitrary",), 40),
        name="ada_mod",
    )(cv, w, b.reshape(1, N))


def _modulation(c, c_ctx, w, b):
    B, D = c.shape
    R = -(-(B + 1) // 8) * 8
    cv = jnp.concatenate([c, c_ctx[None, :], jnp.zeros((R - B - 1, D), F32)], axis=0)
    m = _ada(cv, w, b)
    lat = m[:B].reshape(B, 1, 6, D)
    cx = jnp.broadcast_to(m[B].reshape(1, 1, 6, D), (B, 1, 6, D))
    return jnp.concatenate([lat, cx], axis=1)


def _modnorm(x, mod, g, si):
    sh = mod[si:si + 1, :]
    sc = mod[si + 1:si + 2, :]
    return (_rms(x) * g) * (1.0 + sc) + sh


def _row_specs(a, n_lat_tiles):
    if isinstance(a, tuple):
        w = a[0].shape[2]
        return ([pl.BlockSpec((None, TM, w), lambda b, i: (b, jnp.minimum(i, n_lat_tiles - 1), 0)),
                 pl.BlockSpec((None, TM, w), lambda b, i: (b, jnp.maximum(i - n_lat_tiles, 0), 0))], list(a))
    return [pl.BlockSpec((None, TM, a.shape[2]), lambda b, i: (b, i, 0))], [a]


def _load_rows(refs, n_lat_tiles):
    if len(refs) == 1:
        return refs[0][...]
    return jnp.where(pl.program_id(1) >= n_lat_tiles, refs[1][...], refs[0][...])


def _proj_kernel(n_x, widths, si, n_lat_tiles, *refs):
    x_refs = refs[:n_x]
    mod_ref, g_ref, w_ref, gb_ref = refs[n_x:n_x + 4]
    o_refs = refs[n_x + 4:]
    h = _modnorm(_load_rows(x_refs, n_lat_tiles), mod_ref[...], g_ref[...], si).astype(BF16)
    off = 0
    for j, wd in enumerate(widths):
        y = _bdot(h, w_ref[:, off:off + wd])
        off += wd
        if j == len(widths) - 1:
            y = _mlstm_gate_prep(y + gb_ref[...])
        o_refs[j][...] = y.astype(o_refs[j].dtype)


def _mlstm_gate_prep(g):
    n = g.shape[0]
    H = ML_HEADS
    lf = _log_sigmoid(g)
    t = lax.broadcasted_iota(jnp.int32, (n, n), 0)
    s = lax.broadcasted_iota(jnp.int32, (n, n), 1)
    hi = lf.astype(BF16)
    r1 = lf - hi.astype(F32)
    mid = r1.astype(BF16)
    parts = jnp.concatenate([hi, mid, (r1 - mid.astype(F32)).astype(BF16)], axis=1)
    w = g.shape[1]
    pre3 = _bdot((s <= t).astype(BF16), parts)
    suf3 = _bdot((s >= t).astype(BF16), parts)
    pre = pre3[:, :w] + pre3[:, w:2 * w] + pre3[:, 2 * w:]
    suf = suf3[:, :w] + suf3[:, w:2 * w] + suf3[:, 2 * w:]
    col = lax.broadcasted_iota(jnp.int32, (1, g.shape[1]), 1)
    g = jnp.where((col >= H) & (col < 2 * H), pre, g)
    return jnp.where((col >= 3 * H) & (col < 4 * H), suf, g)


def _in_proj(x, mod, g, w, gate_b, widths, n_rows, n_lat_tiles):
    B, _, _, D = mod.shape
    nt = n_rows // TM
    in_specs, args = _row_specs(x, n_lat_tiles)
    n_x = len(args)
    in_specs += [pl.BlockSpec((None, None, 6, D), lambda b, i: (b, (i >= n_lat_tiles).astype(jnp.int32), 0, 0)),
                 pl.BlockSpec((1, D), lambda b, i: (0, 0)),
                 pl.BlockSpec(w.shape, lambda b, i: (0, 0)),
                 pl.BlockSpec(gate_b.shape, lambda b, i: (0, 0))]
    args += [mod, g.reshape(1, D), w, gate_b]
    dtypes = [BF16] * (len(widths) - 1) + [F32]
    out_specs = [pl.BlockSpec((None, TM, wd), lambda b, i: (b, i, 0)) for wd in widths]
    out_shape = [jax.ShapeDtypeStruct((B, n_rows, wd), dt) for wd, dt in zip(widths, dtypes)]
    return pl.pallas_call(
        functools.partial(_proj_kernel, n_x, tuple(widths), 0, n_lat_tiles),
        grid=(B, nt), in_specs=in_specs, out_specs=out_specs, out_shape=out_shape,
        compiler_params=_params(("parallel", "arbitrary"), 48),
        name="modnorm_proj",
    )(*args)


def _filter_tables(L):
    pos = np.arange(L, dtype=np.float64)
    t = pos / (L - 1)
    bands = (HY_EMB - 1) // 2
    f = np.linspace(1e-4, bands - 1, bands)
    ang = (2.0 * math.pi / L) * pos[:, None] * f[None, :]
    z = np.concatenate([t[:, None], np.cos(ang), -np.sin(ang)], axis=-1)
    zp = np.zeros((2 * L, LANES), np.float64)
    zp[L:, :HY_EMB] = z
    zp[1:L, :HY_EMB] = z[:0:-1]
    return zp.astype(np.float32)


def _filt_mlp_kernel(L, tl, z_ref, w1, b1, w2, b2, w3, fq, absd, bias_ref, o_ref):
    i = pl.program_id(0)
    z = z_ref[...]
    f = fq[...]
    h = jnp.sin(f * (jnp.dot(z, w1[...], preferred_element_type=F32, precision=HI) + b1[...]))
    h = jnp.sin(f * (jnp.dot(h, w2[...], preferred_element_type=F32, precision=HI) + b2[...]))
    h = jnp.dot(h, w3[...], preferred_element_type=F32, precision=HI)
    t = z[:, 0:1]
    h = h * (jnp.exp(-t * absd[...]) + HY_SHIFT)
    lag = lax.broadcasted_iota(jnp.int32, (tl, 1), 0) + (i * tl - L)
    o_ref[...] = jnp.where(lag == -L, 0.0, jnp.where(lag == 0, h + bias_ref[...], h))


def _pad2(a, r, c):
    return jnp.zeros((r, c), F32).at[:a.shape[0], :a.shape[1]].set(a.astype(F32))


def _hyena_filter_lags(L, fw1, fb1, fw2, fb2, fw3, freq, hy_bias):
    tl = min(L, 512)
    nl = L // tl
    C = HY_WIDTH
    deltas = np.linspace(math.log(HY_TARGET) / HY_FAST, math.log(HY_TARGET) / HY_SLOW, C)
    absd = jnp.asarray(np.abs(np.tile(deltas, 2))[None, :].astype(np.float32))
    z = jnp.asarray(_filter_tables(L))
    full = lambda shape: pl.BlockSpec(shape, lambda i: (0, 0))
    half = lambda rows: pl.BlockSpec((rows, C), lambda i: (0, (i < nl).astype(jnp.int32)))
    return pl.pallas_call(
        functools.partial(_filt_mlp_kernel, L, tl),
        grid=(2 * nl,),
        in_specs=[pl.BlockSpec((tl, LANES), lambda i: (i, 0)),
                  full((LANES, LANES)), full((1, LANES)), full((LANES, LANES)), full((1, LANES)),
                  half(LANES), full((1, LANES)), half(1), full((1, C))],
        out_specs=pl.BlockSpec((tl, C), lambda i: (i, 0)),
        out_shape=jax.ShapeDtypeStruct((2 * L, C), F32),
        compiler_params=_params(("arbitrary",), 32),
        name="hyena_filter_mlp",
    )(z, _pad2(fw1, LANES, LANES), _pad2(fb1[None, :], 1, LANES), _pad2(fw2, LANES, LANES),
      _pad2(fb2[None, :], 1, LANES), _pad2(fw3, LANES, 2 * C), _pad2(freq[None, :], 1, LANES), absd,
      hy_bias.reshape(1, C).astype(F32))


def _dft_tables(L):
    n = 2 * L
    f = np.arange(L, dtype=np.int64)[:, None]
    s = np.arange(L, dtype=np.int64)[None, :]
    ang = (2.0 * math.pi / n) * ((f * s) % n).astype(np.float64)
    wc = np.cos(ang)
    ws = np.sin(ang)
    ws[0, :] = np.where(np.arange(L) % 2 == 0, 1.0, -1.0)
    w = np.concatenate([wc, ws], axis=0).astype(np.float32)
    return w, np.ascontiguousarray(w.T)


def _hyena_radix(L):
    return max(r for r in (4, 2, 1) if L % r == 0 and L // r >= 256)


def _filt_dft_kernel(M, wf_ref, k1_ref, k0_ref, p_ref, q_ref, r_ref):
    row = lax.broadcasted_iota(jnp.int32, (M, 1), 0)
    row0 = row == 0
    wf = wf_ref[...]
    t1 = _bdot(wf, k1_ref[...].astype(BF16))
    t0 = _bdot(wf, jnp.where(row0, 0.0, k0_ref[...]).astype(BF16))
    sgn = jnp.where((row & 1) == 1, -1.0, 1.0)
    sc = jnp.where(row0, 0.5 / M, 1.0 / M)
    ka = (t1[:M] + sgn * t0[:M]) * sc
    kb = (t1[M:] + sgn * t0[M:]) * sc
    p_ref[...] = ka
    q_ref[...] = jnp.where(row0, 0.0, kb)
    r_ref[...] = jnp.where(row0, kb, ka)


def _hyena_filter_spectrum(L, R, wf, lags):
    C = HY_WIDTH
    M = L // R
    lags = lags.reshape(2 * R, M, C)
    nd = 2 * R - 1
    out = pl.BlockSpec((None, M, C), lambda d: (d, 0, 0))
    return pl.pallas_call(
        functools.partial(_filt_dft_kernel, M),
        grid=(nd,),
        in_specs=[pl.BlockSpec((2 * M, M), lambda d: (0, 0)),
                  pl.BlockSpec((None, M, C), lambda d: (d + 1, 0, 0)),
                  pl.BlockSpec((None, M, C), lambda d: (d, 0, 0))],
        out_specs=[out, out, out],
        out_shape=[jax.ShapeDtypeStruct((nd, M, C), F32)] * 3,
        compiler_params=_params(("arbitrary",), 32),
        name="hyena_filter_dft",
    )(wf, lags, lags)


def _conv3_rows(x, w, first, last):
    n = x.shape[0]
    xm = jnp.where(first, 0.0, pltpu.roll(x, 1, 0))
    xp = jnp.where(last, 0.0, pltpu.roll(x, n - 1, 0))
    return w[0:1, :] * xm + w[1:2, :] * x + w[2:3, :] * xp


def _hyena_kernel(L, R, x0_ref, x1_ref, v_ref, cw_ref, cb_ref, wf_ref, wft_ref, p_ref, q_ref, r_ref, o_ref):
    M = L // R
    row = lax.broadcasted_iota(jnp.int32, (L, 1), 0)
    first = row == 0
    last = row == L - 1
    x1 = _conv3_rows(x1_ref[...].astype(F32), cw_ref[1], first, last) + cb_ref[1:2, :]
    v = _conv3_rows(v_ref[...].astype(F32), cw_ref[2], first, last) + cb_ref[2:3, :]
    g = (v * x1).astype(BF16)
    x0 = _conv3_rows(x0_ref[...].astype(F32), cw_ref[0], first, last) + cb_ref[0:1, :]
    wf = wf_ref[...]
    spec = [_bdot(wf, g[j * M:(j + 1) * M]) for j in range(R)]
    for i in range(R):
        yr = yb = None
        for j in range(R):
            a, b = spec[j][:M], spec[j][M:]
            d = i - j + R - 1
            p, q, r = p_ref[d], q_ref[d], r_ref[d]
            tr = a * p - b * q
            tb = a * q + b * r
            yr = tr if yr is None else yr + tr
            yb = tb if yb is None else yb + tb
        y = _bdot(wft_ref[...], jnp.concatenate([yr, yb], axis=0).astype(BF16))
        o_ref[i * M:(i + 1) * M, :] = (y * x0[i * M:(i + 1) * M]).astype(o_ref.dtype)


def _hyena_mix(x0, x1, v, cw, cb, wf, wft, spectra, L, R, row_block):
    B, T, C = x0.shape
    M = L // R
    cbw = 256
    nd = 2 * R - 1
    in_specs = [pl.BlockSpec((None, L, cbw), lambda c, b: (b, row_block, c))] * 3 + [
        pl.BlockSpec((3, SHORT_W, cbw), lambda c, b: (0, 0, c)),
        pl.BlockSpec((3, cbw), lambda c, b: (0, c)),
        pl.BlockSpec((2 * M, M), lambda c, b: (0, 0)),
        pl.BlockSpec((M, 2 * M), lambda c, b: (0, 0)),
    ] + [pl.BlockSpec((nd, M, cbw), lambda c, b: (0, 0, c))] * 3
    return pl.pallas_call(
        functools.partial(_hyena_kernel, L, R),
        grid=(C // cbw, B),
        in_specs=in_specs,
        out_specs=pl.BlockSpec((None, L, cbw), lambda c, b: (b, 0, c)),
        out_shape=jax.ShapeDtypeStruct((B, L, C), BF16),
        compiler_params=_params(("parallel", "arbitrary"), 56),
        name="hyena_mix",
    )(x0, x1, v, cw, cb, wf, wft, *spectra)


def _mlstm_kernel(S, T, q_ref, k_ref, v_ref, og_ref, gcol_ref, grow_ref, cwq_ref, cwk_ref, ng_ref, o_ref,
                  k_s, qt_s, vat_s, hf_s, hb_s, na_s, dc_s, ml_s):
    d = ML_HEAD_DIM
    lc = ML_BLOCK
    nc = T // lc
    ncl = S // lc
    row = lax.broadcasted_iota(jnp.int32, (T, 1), 0)
    first = (row == 0) | (row == S)
    last = (row == S - 1) | (row == T - 1)
    qc = _silu(_conv3_rows(q_ref[...].astype(F32), cwq_ref[...], first, last)) * (d ** -0.5)
    k_s[...] = _silu(_conv3_rows(k_ref[...].astype(F32), cwk_ref[...], first, last)).astype(BF16)
    ones_row = (lax.broadcasted_iota(jnp.int32, (d, lc), 0) == 0).astype(F32)
    for c in range(nc):
        qt_s[c] = qc[c * lc:(c + 1) * lc, :].T.astype(BF16)
        vt = v_ref[c * lc:(c + 1) * lc, :].astype(F32).T
        vat_s[c] = jnp.concatenate([vt, ones_row], axis=0).astype(BF16)

    si = lax.broadcasted_iota(jnp.int32, (lc, lc), 0)
    ti = lax.broadcasted_iota(jnp.int32, (lc, lc), 1)
    past = si <= ti
    future = si >= ti

    scans = ((0, past, lc - 1), (2, future, 0))

    b_tot = [[None] * nc for _ in scans]
    g_max = [[None] * nc for _ in scans]
    for c in range(nc):
        k = k_s[c * lc:(c + 1) * lc, :]
        vat = vat_s[c]
        st = _bdot(k, qt_s[c])
        vatf = vat.astype(F32)
        gc = gcol_ref[:, 4 * c:4 * c + 4]
        gr = grow_ref[c]
        for dn, (kind, mask, end) in enumerate(scans):
            r_col = gc[:, kind:kind + 1] - gc[:, kind + 1:kind + 2]
            b_row = gr[kind + 1:kind + 2, :]
            r_row = gr[kind:kind + 1, :] - b_row
            dlog = jnp.where(mask, r_col + b_row, -jnp.inf)
            m_loc = jnp.max(dlog, axis=0, keepdims=True)
            na_s[dn, c] = _bdot(vat, (st * jnp.exp(dlog - m_loc)).astype(BF16))
            ml_s[dn, c] = m_loc
            b_tot[dn][c] = b_row[:, end:end + 1]
            gs_row = b_tot[dn][c] + r_row
            g_max[dn][c] = jnp.max(gs_row, axis=1, keepdims=True)
            dc_s[dn, c] = _bdot((vatf * jnp.exp(gs_row - g_max[dn][c])).astype(BF16), k)

    def advance(dn, c, cs, m_prev):
        kind = scans[dn][0]
        m_loc = ml_s[dn, c]
        inter = grow_ref[c][kind + 1:kind + 2, :] + m_prev
        m_t = jnp.maximum(inter, m_loc)
        e_in = jnp.exp(m_loc - m_t)
        e_st = jnp.exp(inter - m_t)
        n_in = na_s[dn, c]
        n_st = _bdot(cs.astype(BF16), qt_s[c])
        den = e_in * n_in[d:d + 1, :] + e_st * n_st[d:d + 1, :]
        inv = 1.0 / jnp.maximum(jnp.abs(den), jnp.exp(-m_t))
        ht = (e_in * inv) * n_in[:d, :] + (e_st * inv) * n_st[:d, :]
        m_new = jnp.maximum(b_tot[dn][c] + m_prev, g_max[dn][c])
        cs = jnp.exp(b_tot[dn][c] + m_prev - m_new) * cs + jnp.exp(g_max[dn][c] - m_new) * dc_s[dn, c]
        return ht, cs, m_new

    c_f = c_b = jnp.zeros((2 * d, d), F32)
    m_f = m_b = jnp.zeros((1, 1), F32)
    for i in range(nc):
        cf, cb = (i + ncl) % nc, nc - 1 - i
        hf_s[cf], c_f, m_f = advance(0, cf, c_f, m_f)
        hb_s[cb], c_b, m_b = advance(1, cb, c_b, m_b)

    for c in range(nc):
        h = (hf_s[c] + hb_s[c]).T
        og = og_ref[c * lc:(c + 1) * lc, :].astype(F32)
        o_ref[c * lc:(c + 1) * lc, :] = ((_rms(h) * ng_ref[...]) * _sigmoid(og)).astype(o_ref.dtype)


def _mlstm_mix(q, k, v, og, gcol, grow, cwq, cwk, ng, S):
    B, T, W = q.shape
    H = W // ML_HEAD_DIM
    d = ML_HEAD_DIM
    nc = T // ML_BLOCK
    head = pl.BlockSpec((None, T, d), lambda b, h: (b, 0, h))
    return pl.pallas_call(
        functools.partial(_mlstm_kernel, S, T),
        grid=(B, H),
        in_specs=[head, head, head, head,
                  pl.BlockSpec((None, None, ML_BLOCK, nc * 4), lambda b, h: (b, h, 0, 0)),
                  pl.BlockSpec((None, None, nc, 4, ML_BLOCK), lambda b, h: (b, h, 0, 0, 0)),
                  pl.BlockSpec((SHORT_W, d), lambda b, h: (0, h)),
                  pl.BlockSpec((SHORT_W, d), lambda b, h: (0, h)),
                  pl.BlockSpec((1, d), lambda b, h: (0, h))],
        out_specs=head,
        out_shape=jax.ShapeDtypeStruct((B, T, W), BF16),
        scratch_shapes=[pltpu.VMEM((T, d), BF16), pltpu.VMEM((nc, d, ML_BLOCK), BF16),
                        pltpu.VMEM((nc, 2 * d, ML_BLOCK), BF16),
                        pltpu.VMEM((nc, d, ML_BLOCK), F32), pltpu.VMEM((nc, d, ML_BLOCK), F32),
                        pltpu.VMEM((2, nc, 2 * d, ML_BLOCK), F32), pltpu.VMEM((2, nc, 2 * d, d), F32),
                        pltpu.VMEM((2, nc, 1, ML_BLOCK), F32)],
        compiler_params=_params(("parallel", "parallel"), 48),
        name="mlstm_mix",
    )(q, k, v, og, gcol, grow, cwq, cwk, ng)


FETCH = 16
HALO = 8
FFN_SPLIT = 1
TAIL_TM_LATENT = 512


def _halo_specs(a, n_lat_tiles, tm):
    fb = tm // FETCH
    if isinstance(a, tuple):
        lat, cx = a
        assert cx.shape[1] == tm
        w, nfb = lat.shape[2], lat.shape[1] // FETCH
        li = lambda i: jnp.minimum(i, n_lat_tiles - 1)
        return ([pl.BlockSpec((None, tm, w), lambda b, i: (b, li(i), 0)),
                 pl.BlockSpec((None, FETCH, w), lambda b, i: (b, jnp.maximum(li(i) * fb - 1, 0), 0)),
                 pl.BlockSpec((None, FETCH, w), lambda b, i: (b, jnp.minimum((li(i) + 1) * fb, nfb - 1), 0)),
                 pl.BlockSpec((None, tm, w), lambda b, i: (b, 0, 0))], [lat, lat, lat, cx])
    w, nfb = a.shape[2], a.shape[1] // FETCH
    return ([pl.BlockSpec((None, tm, w), lambda b, i: (b, i, 0)),
             pl.BlockSpec((None, FETCH, w), lambda b, i: (b, jnp.maximum(i * fb - 1, 0), 0)),
             pl.BlockSpec((None, FETCH, w), lambda b, i: (b, jnp.minimum((i + 1) * fb, nfb - 1), 0))], [a, a, a])


def _load_halo(refs, n_lat_tiles):
    main = refs[0][...]
    if len(refs) == 4:
        main = jnp.where(pl.program_id(1) >= n_lat_tiles, refs[3][...], main)
    return jnp.concatenate([refs[1][...], main, refs[2][...]], axis=0)


def _tail_kernel(final, n_lat_tiles, counts, mod_ref, *refs):
    groups = []
    for n in counts:
        groups.append(refs[:n])
        refs = refs[n:]
    n_act = len(counts) - 1
    wo_refs = refs[:n_act]
    g_ref, wup_ref, cw_ref, cb_ref, wd_ref, fn_ref, o_ref = refs[n_act:]
    i = pl.program_id(1)
    nt = pl.num_programs(1)
    tm = o_ref.shape[0]
    mod = mod_ref[...]
    y = None
    for a_refs, w_ref in zip(groups[1:], wo_refs):
        t = _bdot(_load_halo(a_refs, n_lat_tiles).astype(BF16), w_ref[...])
        y = t if y is None else y + t
    xe = _load_halo(groups[0], n_lat_tiles).astype(F32) + mod[2:3, :] * y
    x = xe[FETCH:FETCH + tm]
    hf = _modnorm(xe[FETCH - HALO:FETCH + tm + HALO], mod, g_ref[...], 3)
    he = hf.astype(BF16)
    h = hf[HALO:HALO + tm].astype(BF16)
    has_prev = jnp.logical_and(i != 0, i != n_lat_tiles)
    has_next = jnp.logical_and(i != n_lat_tiles - 1, i != nt - 1)
    row = lax.broadcasted_iota(jnp.int32, (tm + 2 * HALO, 1), 0)
    inside = jnp.logical_and(jnp.logical_or(row >= HALO, has_prev), jnp.logical_or(row < tm + HALO, has_next))
    fh = wd_ref.shape[0]
    fc = fh // FFN_SPLIT
    y = None
    for c in range(FFN_SPLIT):
        cols = slice(c * fc, (c + 1) * fc)
        ge = jnp.where(inside, _bdot(he, wup_ref[:, cols]), 0.0)
        gm = pltpu.roll(ge, 1, 0)[HALO:HALO + tm]
        gq = pltpu.roll(ge, tm + 2 * HALO - 1, 0)[HALO:HALO + tm]
        cw = cw_ref[:, cols]
        gc = cw[0:1, :] * gm + cw[1:2, :] * ge[HALO:HALO + tm] + cw[2:3, :] * gq + cb_ref[:, cols]
        a = (_silu(gc) * _bdot(h, wup_ref[:, fh + c * fc:fh + (c + 1) * fc])).astype(BF16)
        t = _bdot(a, wd_ref[cols, :])
        y = t if y is None else y + t
    o = x + mod[5:6, :] * y
    if final:
        o = _rms(o) * fn_ref[...]
    o_ref[...] = o


def _mixer_tail(x, acts, ws_out, mod, norm_g, w_up, cw, cb, w_down, fn, n_rows, n_lat_tiles, final, tm):
    B, _, _, D = mod.shape
    Fh = w_up.shape[1] // 2
    nt = n_rows // tm
    once = lambda shape: pl.BlockSpec(shape, lambda b, i: (0, 0), pipeline_mode=pl.Buffered(1))
    in_specs = [pl.BlockSpec((None, None, 6, D), lambda b, i: (b, (i >= n_lat_tiles).astype(jnp.int32), 0, 0))]
    args = [mod]
    counts = []
    for a in [x] + list(acts):
        sp, ar = _halo_specs(a, n_lat_tiles, tm)
        in_specs += sp
        args += ar
        counts.append(len(ar))
    in_specs += [once(w.shape) for w in ws_out]
    in_specs += [once((1, D)), once((D, 2 * Fh)), once((SHORT_W, Fh)), once((1, Fh)), once((Fh, D)), once((1, D))]
    return pl.pallas_call(
        functools.partial(_tail_kernel, final, n_lat_tiles, tuple(counts)),
        grid=(B, nt), in_specs=in_specs,
        out_specs=pl.BlockSpec((None, tm, D), lambda b, i: (b, i, 0)),
        out_shape=jax.ShapeDtypeStruct((B, n_rows, D), F32),
        compiler_params=_params(("parallel", "arbitrary"), 56),
        name="mixer_tail",
    )(*args, *ws_out, norm_g.reshape(1, D), w_up.astype(BF16), cw.astype(F32), cb.reshape(1, Fh).astype(F32),
      w_down.astype(BF16), fn.reshape(1, D))


def _rope_tables(S, T):
    p = np.arange(S)
    inv = ROPE_BASE ** (-np.arange(0, ROPE_AXIS, 2, dtype=np.float64) / ROPE_AXIS)
    ar = (p // GRID_W)[:, None] * inv[None, :]
    ac = (p % GRID_W)[:, None] * inv[None, :]
    cos = np.concatenate([np.cos(ar), np.cos(ar), np.cos(ac), np.cos(ac)], axis=1)
    sin = np.concatenate([-np.sin(ar), np.sin(ar), -np.sin(ac), np.sin(ac)], axis=1)
    cos = np.concatenate([cos, np.ones((T - S, MLA_ROPE))], axis=0)
    sin = np.concatenate([sin, np.zeros((T - S, MLA_ROPE))], axis=0)
    return cos.astype(np.float32), sin.astype(np.float32)


def _mla_proj_kernel(n_x, n_lat_tiles, *refs):
    x_refs = refs[:n_x]
    mod_ref, g_ref, wdn_ref, qn_ref, kvn_ref, wuq_ref, wukv_ref, cos_ref, sin_ref, qo_ref, ko_ref, vo_ref = refs[n_x:]
    H, dn, dr = MLA_HEADS, MLA_NOPE, MLA_ROPE
    h = _modnorm(_load_rows(x_refs, n_lat_tiles), mod_ref[...], g_ref[...], 0).astype(BF16)
    dnp = _bdot(h, wdn_ref[...])
    qa = dnp[:, :MLA_Q_RANK]
    kva = dnp[:, MLA_Q_RANK:MLA_Q_RANK + MLA_KV_RANK]
    kr2 = dnp[:, MLA_Q_RANK + MLA_KV_RANK:]
    q = _bdot((_rms(qa) * qn_ref[...]).astype(BF16), wuq_ref[...])
    kv = _bdot((_rms(kva) * kvn_ref[...]).astype(BF16), wukv_ref[...])
    cosr = cos_ref[...]
    sinr = sin_ref[...]
    kr = kr2[:, :dr] * cosr + kr2[:, dr:] * sinr
    zpad = jnp.zeros((kr.shape[0], MLA_DK_PAD - dn - dr), F32)
    ro = H * dn
    for hd in range(H):
        qr = q[:, ro + hd * dr:ro + (hd + 1) * dr] * cosr + q[:, ro + H * dr + hd * dr:ro + H * dr + (hd + 1) * dr] * sinr
        qh = jnp.concatenate([q[:, hd * dn:(hd + 1) * dn], qr, zpad], axis=1) * (MLA_SCALE * math.log2(math.e))
        qo_ref[hd] = qh.astype(BF16)
        ko_ref[hd] = jnp.concatenate([kv[:, hd * dn:(hd + 1) * dn], kr, zpad], axis=1).astype(BF16)
        vo_ref[hd] = kv[:, ro + hd * MLA_V:ro + (hd + 1) * MLA_V].astype(BF16)


def _mla_proj(x, mod, norm_g, w_down, q_norm, kv_norm, w_uq, w_ukv, S, T):
    B, _, _, D = mod.shape
    H, dn, dr, dv = MLA_HEADS, MLA_NOPE, MLA_ROPE, MLA_V
    dk = dn + dr
    nt = T // TM
    n_lat_tiles = S // TM
    swap = np.arange(dr) ^ (ROPE_AXIS // 2)
    wkr = w_down[:, MLA_Q_RANK + MLA_KV_RANK:]
    wdn = jnp.concatenate([w_down, wkr[:, swap]], axis=1).astype(BF16)
    wq = w_uq.reshape(MLA_Q_RANK, H, dk)
    wq_n = wq[:, :, :dn].reshape(MLA_Q_RANK, H * dn)
    wq_r = wq[:, :, dn:]
    wuq = jnp.concatenate([wq_n, wq_r.reshape(MLA_Q_RANK, H * dr), wq_r[:, :, swap].reshape(MLA_Q_RANK, H * dr)],
                          axis=1).astype(BF16)
    wkv = w_ukv.reshape(MLA_KV_RANK, H, dn + dv)
    wukv = jnp.concatenate([wkv[:, :, :dn].reshape(MLA_KV_RANK, H * dn), wkv[:, :, dn:].reshape(MLA_KV_RANK, H * dv)],
                           axis=1).astype(BF16)
    cos, sin = _rope_tables(S, T)
    full = lambda a: pl.BlockSpec(a.shape, lambda b, i: (0,) * a.ndim)
    in_specs, args = _row_specs(x, n_lat_tiles)
    n_x = len(args)
    in_specs += [pl.BlockSpec((None, None, 6, D), lambda b, i: (b, (i >= n_lat_tiles).astype(jnp.int32), 0, 0)),
                 pl.BlockSpec((1, D), lambda b, i: (0, 0)),
                 full(wdn), pl.BlockSpec((1, MLA_Q_RANK), lambda b, i: (0, 0)),
                 pl.BlockSpec((1, MLA_KV_RANK), lambda b, i: (0, 0)), full(wuq), full(wukv),
                 pl.BlockSpec((TM, dr), lambda b, i: (i, 0)), pl.BlockSpec((TM, dr), lambda b, i: (i, 0))]
    headed = lambda w: pl.BlockSpec((None, H, TM, w), lambda b, i: (b, 0, i, 0))
    return pl.pallas_call(
        functools.partial(_mla_proj_kernel, n_x, n_lat_tiles),
        grid=(B, nt), in_specs=in_specs,
        out_specs=[headed(MLA_DK_PAD), headed(MLA_DK_PAD), headed(dv)],
        out_shape=[jax.ShapeDtypeStruct((B, H, T, MLA_DK_PAD), BF16), jax.ShapeDtypeStruct((B, H, T, MLA_DK_PAD), BF16),
                   jax.ShapeDtypeStruct((B, H, T, dv), BF16)],
        compiler_params=_params(("parallel", "arbitrary"), 40),
        name="mla_proj",
    )(*args, mod, norm_g.reshape(1, D), wdn, q_norm.reshape(1, -1), kv_norm.reshape(1, -1), wuq, wukv,
      jnp.asarray(cos), jnp.asarray(sin))


ATT_HEADS_PER_STEP = 4
ATT_KEY_SLAB = 768


def _attn_kernel(q_ref, k_ref, v_ref, o_ref, va_s):
    hp, T, dv = v_ref.shape

    @pl.when(pl.program_id(2) == 0)
    def _():
        ones_col = (lax.broadcasted_iota(jnp.int32, (T, dv), 1) == 0).astype(BF16)
        for j in range(hp):
            va_s[j] = jnp.concatenate([v_ref[j], ones_col], axis=1)

    kw = max(w for w in range(LANES, ATT_KEY_SLAB + 1, LANES) if T % w == 0)
    slabs = [slice(c * kw, (c + 1) * kw) for c in range(T // kw)]
    nt = (((1,), (1,)), ((), ()))
    s = [[lax.dot_general(q_ref[j], k_ref[j, sl, :], nt, preferred_element_type=F32) for sl in slabs]
         for j in range(hp)]
    for j in range(hp):
        m = functools.reduce(jnp.maximum, [jnp.max(sc, axis=-1, keepdims=True) for sc in s[j]])
        na = None
        for sc, sl in zip(s[j], slabs):
            t = _bdot(jnp.exp2(sc - m).astype(BF16), va_s[j, sl, :])
            na = t if na is None else na + t
        o_ref[:, j * dv:(j + 1) * dv] = (na[:, :dv] * (1.0 / na[:, dv:dv + 1])).astype(o_ref.dtype)


def _attention(q, k, v, S):
    B, H, T, dk = k.shape
    dv = v.shape[3]
    tq = min(S, 1024)
    hp = 2
    return pl.pallas_call(
        _attn_kernel,
        grid=(B, H // hp, S // tq),
        in_specs=[pl.BlockSpec((None, hp, tq, dk), lambda b, h, i: (b, h, i, 0)),
                  pl.BlockSpec((None, hp, T, dk), lambda b, h, i: (b, h, 0, 0)),
                  pl.BlockSpec((None, hp, T, dv), lambda b, h, i: (b, h, 0, 0))],
        out_specs=pl.BlockSpec((None, tq, hp * dv), lambda b, h, i: (b, i, h)),
        out_shape=jax.ShapeDtypeStruct((B, S, H * dv), BF16),
        scratch_shapes=[pltpu.VMEM((hp, T, 2 * dv), BF16)],
        compiler_params=_params(("parallel", "parallel", "arbitrary"), 48),
        name="mla_attention",
    )(q, k, v)


def kernel(x, c, ctx, c_ctx, ada_w_0, ada_b_0, norm_mix_0, norm_ffn_0, w_in_0, hy_conv_w, hy_conv_b, hy_fw1, hy_fb1, hy_fw2, hy_fb2, hy_fw3, hy_freq, hy_bias, ml_conv_w, ml_gate_b, ml_norm_g, w_out_0, ffn_up_0, ffn_conv_w_0, ffn_conv_b_0, ffn_down_0, ada_w_1, ada_b_1, norm_mix_1, norm_ffn_1, mla_w_down, mla_q_norm, mla_kv_norm, mla_w_uq, mla_w_ukv, mla_w_o, ffn_up_1, ffn_conv_w_1, ffn_conv_b_1, ffn_down_1, final_norm):
    B, S, D = x.shape
    CL = ctx.shape[1]
    T = S + CL
    assert S % TM == 0 and CL % TM == 0 and S % CL == 0 and S % GRID_W == 0 and TM == ML_BLOCK
    nlt = S // TM
    C = HY_WIDTH
    W = ML_WIDTH
    H = ML_HEADS

    mod0 = _modulation(c, c_ctx, ada_w_0, ada_b_0)
    mod1 = _modulation(c, c_ctx, ada_w_1, ada_b_1)

    wi = jnp.pad(w_in_0, ((0, 0), (0, LANES - 4 * H))).astype(BF16)
    b_gate = jnp.zeros((1, LANES), F32).at[0, :4 * H].set(ml_gate_b.astype(F32))
    hx0, hx1, hv, mq, mk, mv, mog, gates = _in_proj((x, ctx), mod0, norm_mix_0, wi, b_gate,
                                                    [C] * 3 + [W] * 4 + [LANES], T, nlt)

    filt = (hy_fw1, hy_fb1, hy_fw2, hy_fb2, hy_fw3, hy_freq, hy_bias)
    hcw = jnp.transpose(hy_conv_w.reshape(SHORT_W, 3, C), (1, 0, 2)).astype(F32)
    hcb = hy_conv_b.reshape(3, C).astype(F32)
    hy = []
    for L, blk in ((S, 0), (CL, S // CL)):
        R = _hyena_radix(L)
        wf_np, wft_np = _dft_tables(L // R)
        wf = jnp.asarray(wf_np).astype(BF16)
        wft = jnp.asarray(wft_np).astype(BF16)
        spectra = _hyena_filter_spectrum(L, R, wf, _hyena_filter_lags(L, *filt))
        hy.append(_hyena_mix(hx0, hx1, hv, hcw, hcb, wf, wft, spectra, L, R, blk))

    nc = T // ML_BLOCK
    g4 = gates[:, :, :4 * H].reshape(B, nc, ML_BLOCK, 4, H)
    gcol = jnp.transpose(g4, (0, 4, 2, 1, 3)).reshape(B, H, ML_BLOCK, nc * 4)
    grow = jnp.transpose(g4, (0, 4, 1, 3, 2))
    mcw = ml_conv_w.astype(F32)
    ml = _mlstm_mix(mq, mk, mv, mog, gcol, grow, mcw[:, :W], mcw[:, W:], ml_norm_g.reshape(1, W).astype(F32), S)

    tml = TAIL_TM_LATENT if S % TAIL_TM_LATENT == 0 else TM
    wo = [w_out_0[:C].astype(BF16), w_out_0[C:].astype(BF16)]
    ffn0 = (mod0, norm_ffn_0, ffn_up_0, ffn_conv_w_0, ffn_conv_b_0, ffn_down_0, final_norm)
    xs_lat = _mixer_tail(x, [hy[0], ml], wo, *ffn0, S, S // tml, False, tml)
    xs_ctx = _mixer_tail(ctx, [hy[1], ml[:, S:]], wo, *ffn0, CL, 0, False, TM)

    q, k, v = _mla_proj((xs_lat, xs_ctx), mod1, norm_mix_1, mla_w_down, mla_q_norm, mla_kv_norm, mla_w_uq,
                        mla_w_ukv, S, T)
    att = _attention(q, k, v, S)
    return _mixer_tail(xs_lat, [att], [mla_w_o.astype(BF16)], mod1, norm_ffn_1, ffn_up_1, ffn_conv_w_1,
                       ffn_conv_b_1, ffn_down_1, final_norm, S, S // tml, True, tml)
```

```python
import functools
import math

import numpy as np
import jax
import jax.numpy as jnp
from jax import lax
from jax.experimental import pallas as pl
from jax.experimental.pallas import tpu as pltpu

F32 = jnp.float32
BF16 = jnp.bfloat16
HI = lax.Precision.HIGHEST

RMS_EPS = 1e-6
GRID_W = 64
SHORT_W = 3
HY_WIDTH = 512
HY_EMB = 33
HY_FAST = 0.3
HY_SLOW = 1.5
HY_TARGET = 1e-2
HY_SHIFT = 0.05
ML_HEADS = 4
ML_HEAD_DIM = 128
ML_WIDTH = ML_HEADS * ML_HEAD_DIM
ML_BLOCK = 256
MLA_HEADS = 8
MLA_NOPE = 128
MLA_ROPE = 64
MLA_V = 128
MLA_Q_RANK = 384
MLA_KV_RANK = 256
MLA_SCALE = (MLA_NOPE + MLA_ROPE) ** -0.5
MLA_DK_PAD = 256
ROPE_AXIS = MLA_ROPE // 2
ROPE_BASE = 10000.0

TM = 256
LANES = 128
MIB = 1024 * 1024


def _params(sem, vmem_mib):
    return pltpu.CompilerParams(dimension_semantics=sem, vmem_limit_bytes=vmem_mib * MIB)


def _sigmoid(x):
    return 0.5 * jnp.tanh(0.5 * x) + 0.5


def _silu(x):
    return x * _sigmoid(x)


def _log_sigmoid(x):
    return jnp.minimum(x, 0.0) - jnp.log(1.0 + jnp.exp(-jnp.abs(x)))


def _rms(x):
    return x * lax.rsqrt(jnp.mean(x * x, axis=-1, keepdims=True) + RMS_EPS)


def _bdot(a, b):
    return jnp.dot(a, b, preferred_element_type=F32)


def _ada_kernel(cv_ref, w_ref, b_ref, o_ref):
    s = _silu(cv_ref[...])
    o_ref[...] = jnp.dot(s, w_ref[...], preferred_element_type=F32, precision=HI) + b_ref[...]


def _ada(cv, w, b):
    R, D = cv.shape
    N = w.shape[1]
    tn = N // 4
    return pl.pallas_call(
        _ada_kernel,
        grid=(N // tn,),
        in_specs=[pl.BlockSpec((R, D), lambda j: (0, 0)),
                  pl.BlockSpec((D, tn), lambda j: (0, j)),
                  pl.BlockSpec((1, tn), lambda j: (0, j))],
        out_specs=pl.BlockSpec((R, tn), lambda j: (0, j)),
        out_shape=jax.ShapeDtypeStruct((R, N), F32),
        compiler_params=_params(("arbitrary",), 40),
        name="ada_mod",
    )(cv, w, b.reshape(1, N))


def _modulation(c, c_ctx, w, b):
    B, D = c.shape
    R = -(-(B + 1) // 8) * 8
    cv = jnp.concatenate([c, c_ctx[None, :], jnp.zeros((R - B - 1, D), F32)], axis=0)
    m = _ada(cv, w, b)
    lat = m[:B].reshape(B, 1, 6, D)
    cx = jnp.broadcast_to(m[B].reshape(1, 1, 6, D), (B, 1, 6, D))
    return jnp.concatenate([lat, cx], axis=1)


def _modnorm(x, mod, g, si):
    sh = mod[si:si + 1, :]
    sc = mod[si + 1:si + 2, :]
    return (_rms(x) * g) * (1.0 + sc) + sh


def _row_specs(a, n_lat_tiles):
    if isinstance(a, tuple):
        w = a[0].shape[2]
        return ([pl.BlockSpec((None, TM, w), lambda b, i: (b, jnp.minimum(i, n_lat_tiles - 1), 0)),
                 pl.BlockSpec((None, TM, w), lambda b, i: (b, jnp.maximum(i - n_lat_tiles, 0), 0))], list(a))
    return [pl.BlockSpec((None, TM, a.shape[2]), lambda b, i: (b, i, 0))], [a]


def _load_rows(refs, n_lat_tiles):
    if len(refs) == 1:
        return refs[0][...]
    return jnp.where(pl.program_id(1) >= n_lat_tiles, refs[1][...], refs[0][...])


def _proj_kernel(n_x, widths, si, n_lat_tiles, *refs):
    x_refs = refs[:n_x]
    mod_ref, g_ref, w_ref, gb_ref = refs[n_x:n_x + 4]
    o_refs = refs[n_x + 4:]
    h = _modnorm(_load_rows(x_refs, n_lat_tiles), mod_ref[...], g_ref[...], si).astype(BF16)
    off = 0
    for j, wd in enumerate(widths):
        y = _bdot(h, w_ref[:, off:off + wd])
        off += wd
        if j == len(widths) - 1:
            y = _mlstm_gate_prep(y + gb_ref[...])
        o_refs[j][...] = y.astype(o_refs[j].dtype)


def _mlstm_gate_prep(g):
    n = g.shape[0]
    H = ML_HEADS
    lf = _log_sigmoid(g)
    t = lax.broadcasted_iota(jnp.int32, (n, n), 0)
    s = lax.broadcasted_iota(jnp.int32, (n, n), 1)
    hi = lf.astype(BF16)
    r1 = lf - hi.astype(F32)
    mid = r1.astype(BF16)
    parts = jnp.concatenate([hi, mid, (r1 - mid.astype(F32)).astype(BF16)], axis=1)
    w = g.shape[1]
    pre3 = _bdot((s <= t).astype(BF16), parts)
    suf3 = _bdot((s >= t).astype(BF16), parts)
    pre = pre3[:, :w] + pre3[:, w:2 * w] + pre3[:, 2 * w:]
    suf = suf3[:, :w] + suf3[:, w:2 * w] + suf3[:, 2 * w:]
    col = lax.broadcasted_iota(jnp.int32, (1, g.shape[1]), 1)
    g = jnp.where((col >= H) & (col < 2 * H), pre, g)
    return jnp.where((col >= 3 * H) & (col < 4 * H), suf, g)


def _in_proj(x, mod, g, w, gate_b, widths, n_rows, n_lat_tiles):
    B, _, _, D = mod.shape
    nt = n_rows // TM
    in_specs, args = _row_specs(x, n_lat_tiles)
    n_x = len(args)
    in_specs += [pl.BlockSpec((None, None, 6, D), lambda b, i: (b, (i >= n_lat_tiles).astype(jnp.int32), 0, 0)),
                 pl.BlockSpec((1, D), lambda b, i: (0, 0)),
                 pl.BlockSpec(w.shape, lambda b, i: (0, 0)),
                 pl.BlockSpec(gate_b.shape, lambda b, i: (0, 0))]
    args += [mod, g.reshape(1, D), w, gate_b]
    dtypes = [BF16] * (len(widths) - 1) + [F32]
    out_specs = [pl.BlockSpec((None, TM, wd), lambda b, i: (b, i, 0)) for wd in widths]
    out_shape = [jax.ShapeDtypeStruct((B, n_rows, wd), dt) for wd, dt in zip(widths, dtypes)]
    return pl.pallas_call(
        functools.partial(_proj_kernel, n_x, tuple(widths), 0, n_lat_tiles),
        grid=(B, nt), in_specs=in_specs, out_specs=out_specs, out_shape=out_shape,
        compiler_params=_params(("parallel", "arbitrary"), 48),
        name="modnorm_proj",
    )(*args)


def _filter_tables(L):
    pos = np.arange(L, dtype=np.float64)
    t = pos / (L - 1)
    bands = (HY_EMB - 1) // 2
    f = np.linspace(1e-4, bands - 1, bands)
    ang = (2.0 * math.pi / L) * pos[:, None] * f[None, :]
    z = np.concatenate([t[:, None], np.cos(ang), -np.sin(ang)], axis=-1)
    zp = np.zeros((2 * L, LANES), np.float64)
    zp[L:, :HY_EMB] = z
    zp[1:L, :HY_EMB] = z[:0:-1]
    return zp.astype(np.float32)


def _filt_mlp_kernel(L, tl, z_ref, w1, b1, w2, b2, w3, fq, absd, bias_ref, o_ref):
    i = pl.program_id(0)
    z = z_ref[...]
    f = fq[...]
    h = jnp.sin(f * (jnp.dot(z, w1[...], preferred_element_type=F32, precision=HI) + b1[...]))
    h = jnp.sin(f * (jnp.dot(h, w2[...], preferred_element_type=F32, precision=HI) + b2[...]))
    h = jnp.dot(h, w3[...], preferred_element_type=F32, precision=HI)
    t = z[:, 0:1]
    h = h * (jnp.exp(-t * absd[...]) + HY_SHIFT)
    lag = lax.broadcasted_iota(jnp.int32, (tl, 1), 0) + (i * tl - L)
    o_ref[...] = jnp.where(lag == -L, 0.0, jnp.where(lag == 0, h + bias_ref[...], h))


def _pad2(a, r, c):
    return jnp.zeros((r, c), F32).at[:a.shape[0], :a.shape[1]].set(a.astype(F32))


def _hyena_filter_lags(L, fw1, fb1, fw2, fb2, fw3, freq, hy_bias):
    tl = min(L, 512)
    nl = L // tl
    C = HY_WIDTH
    deltas = np.linspace(math.log(HY_TARGET) / HY_FAST, math.log(HY_TARGET) / HY_SLOW, C)
    absd = jnp.asarray(np.abs(np.tile(deltas, 2))[None, :].astype(np.float32))
    z = jnp.asarray(_filter_tables(L))
    full = lambda shape: pl.BlockSpec(shape, lambda i: (0, 0))
    half = lambda rows: pl.BlockSpec((rows, C), lambda i: (0, (i < nl).astype(jnp.int32)))
    return pl.pallas_call(
        functools.partial(_filt_mlp_kernel, L, tl),
        grid=(2 * nl,),
        in_specs=[pl.BlockSpec((tl, LANES), lambda i: (i, 0)),
                  full((LANES, LANES)), full((1, LANES)), full((LANES, LANES)), full((1, LANES)),
                  half(LANES), full((1, LANES)), half(1), full((1, C))],
        out_specs=pl.BlockSpec((tl, C), lambda i: (i, 0)),
        out_shape=jax.ShapeDtypeStruct((2 * L, C), F32),
        compiler_params=_params(("arbitrary",), 32),
        name="hyena_filter_mlp",
    )(z, _pad2(fw1, LANES, LANES), _pad2(fb1[None, :], 1, LANES), _pad2(fw2, LANES, LANES),
      _pad2(fb2[None, :], 1, LANES), _pad2(fw3, LANES, 2 * C), _pad2(freq[None, :], 1, LANES), absd,
      hy_bias.reshape(1, C).astype(F32))


def _dft_tables(L):
    n = 2 * L
    f = np.arange(L, dtype=np.int64)[:, None]
    s = np.arange(L, dtype=np.int64)[None, :]
    ang = (2.0 * math.pi / n) * ((f * s) % n).astype(np.float64)
    wc = np.cos(ang)
    ws = np.sin(ang)
    ws[0, :] = np.where(np.arange(L) % 2 == 0, 1.0, -1.0)
    w = np.concatenate([wc, ws], axis=0).astype(np.float32)
    return w, np.ascontiguousarray(w.T)


def _hyena_radix(L):
    return max(r for r in (4, 2, 1) if L % r == 0 and L // r >= 256)


def _filt_dft_kernel(M, wf_ref, k1_ref, k0_ref, p_ref, q_ref, r_ref):
    row = lax.broadcasted_iota(jnp.int32, (M, 1), 0)
    row0 = row == 0
    wf = wf_ref[...]
    t1 = _bdot(wf, k1_ref[...].astype(BF16))
    t0 = _bdot(wf, jnp.where(row0, 0.0, k0_ref[...]).astype(BF16))
    sgn = jnp.where((row & 1) == 1, -1.0, 1.0)
    sc = jnp.where(row0, 0.5 / M, 1.0 / M)
    ka = (t1[:M] + sgn * t0[:M]) * sc
    kb = (t1[M:] + sgn * t0[M:]) * sc
    p_ref[...] = ka
    q_ref[...] = jnp.where(row0, 0.0, kb)
    r_ref[...] = jnp.where(row0, kb, ka)


def _hyena_filter_spectrum(L, R, wf, lags):
    C = HY_WIDTH
    M = L // R
    lags = lags.reshape(2 * R, M, C)
    nd = 2 * R - 1
    out = pl.BlockSpec((None, M, C), lambda d: (d, 0, 0))
    return pl.pallas_call(
        functools.partial(_filt_dft_kernel, M),
        grid=(nd,),
        in_specs=[pl.BlockSpec((2 * M, M), lambda d: (0, 0)),
                  pl.BlockSpec((None, M, C), lambda d: (d + 1, 0, 0)),
                  pl.BlockSpec((None, M, C), lambda d: (d, 0, 0))],
        out_specs=[out, out, out],
        out_shape=[jax.ShapeDtypeStruct((nd, M, C), F32)] * 3,
        compiler_params=_params(("arbitrary",), 32),
        name="hyena_filter_dft",
    )(wf, lags, lags)


def _conv3_rows(x, w, first, last):
    n = x.shape[0]
    xm = jnp.where(first, 0.0, pltpu.roll(x, 1, 0))
    xp = jnp.where(last, 0.0, pltpu.roll(x, n - 1, 0))
    return w[0:1, :] * xm + w[1:2, :] * x + w[2:3, :] * xp


def _hyena_kernel(L, R, x0_ref, x1_ref, v_ref, cw_ref, cb_ref, wf_ref, wft_ref, p_ref, q_ref, r_ref, o_ref):
    M = L // R
    row = lax.broadcasted_iota(jnp.int32, (L, 1), 0)
    first = row == 0
    last = row == L - 1
    x1 = _conv3_rows(x1_ref[...].astype(F32), cw_ref[1], first, last) + cb_ref[1:2, :]
    v = _conv3_rows(v_ref[...].astype(F32), cw_ref[2], first, last) + cb_ref[2:3, :]
    g = (v * x1).astype(BF16)
    x0 = _conv3_rows(x0_ref[...].astype(F32), cw_ref[0], first, last) + cb_ref[0:1, :]
    wf = wf_ref[...]
    spec = [_bdot(wf, g[j * M:(j + 1) * M]) for j in range(R)]
    for i in range(R):
        yr = yb = None
        for j in range(R):
            a, b = spec[j][:M], spec[j][M:]
            d = i - j + R - 1
            p, q, r = p_ref[d], q_ref[d], r_ref[d]
            tr = a * p - b * q
            tb = a * q + b * r
            yr = tr if yr is None else yr + tr
            yb = tb if yb is None else yb + tb
        y = _bdot(wft_ref[...], jnp.concatenate([yr, yb], axis=0).astype(BF16))
        o_ref[i * M:(i + 1) * M, :] = (y * x0[i * M:(i + 1) * M]).astype(o_ref.dtype)


def _hyena_mix(x0, x1, v, cw, cb, wf, wft, spectra, L, R, row_block):
    B, T, C = x0.shape
    M = L // R
    cbw = 256
    nd = 2 * R - 1
    in_specs = [pl.BlockSpec((None, L, cbw), lambda c, b: (b, row_block, c))] * 3 + [
        pl.BlockSpec((3, SHORT_W, cbw), lambda c, b: (0, 0, c)),
        pl.BlockSpec((3, cbw), lambda c, b: (0, c)),
        pl.BlockSpec((2 * M, M), lambda c, b: (0, 0)),
        pl.BlockSpec((M, 2 * M), lambda c, b: (0, 0)),
    ] + [pl.BlockSpec((nd, M, cbw), lambda c, b: (0, 0, c))] * 3
    return pl.pallas_call(
        functools.partial(_hyena_kernel, L, R),
        grid=(C // cbw, B),
        in_specs=in_specs,
        out_specs=pl.BlockSpec((None, L, cbw), lambda c, b: (b, 0, c)),
        out_shape=jax.ShapeDtypeStruct((B, L, C), BF16),
        compiler_params=_params(("parallel", "arbitrary"), 56),
        name="hyena_mix",
    )(x0, x1, v, cw, cb, wf, wft, *spectra)


def _mlstm_kernel(S, T, q_ref, k_ref, v_ref, og_ref, gcol_ref, grow_ref, cwq_ref, cwk_ref, ng_ref, o_ref,
                  k_s, qt_s, vat_s, hf_s, hb_s, na_s, dc_s, ml_s):
    d = ML_HEAD_DIM
    lc = ML_BLOCK
    nc = T // lc
    ncl = S // lc
    row = lax.broadcasted_iota(jnp.int32, (T, 1), 0)
    first = (row == 0) | (row == S)
    last = (row == S - 1) | (row == T - 1)
    qc = _silu(_conv3_rows(q_ref[...].astype(F32), cwq_ref[...], first, last)) * (d ** -0.5)
    k_s[...] = _silu(_conv3_rows(k_ref[...].astype(F32), cwk_ref[...], first, last)).astype(BF16)
    ones_row = (lax.broadcasted_iota(jnp.int32, (d, lc), 0) == 0).astype(F32)
    for c in range(nc):
        qt_s[c] = qc[c * lc:(c + 1) * lc, :].T.astype(BF16)
        vt = v_ref[c * lc:(c + 1) * lc, :].astype(F32).T
        vat_s[c] = jnp.concatenate([vt, ones_row], axis=0).astype(BF16)

    si = lax.broadcasted_iota(jnp.int32, (lc, lc), 0)
    ti = lax.broadcasted_iota(jnp.int32, (lc, lc), 1)
    past = si <= ti
    future = si >= ti

    scans = ((0, past, lc - 1), (2, future, 0))

    b_tot = [[None] * nc for _ in scans]
    g_max = [[None] * nc for _ in scans]
    for c in range(nc):
        k = k_s[c * lc:(c + 1) * lc, :]
        vat = vat_s[c]
        st = _bdot(k, qt_s[c])
        vatf = vat.astype(F32)
        gc = gcol_ref[:, 4 * c:4 * c + 4]
        gr = grow_ref[c]
        for dn, (kind, mask, end) in enumerate(scans):
            r_col = gc[:, kind:kind + 1] - gc[:, kind + 1:kind + 2]
            b_row = gr[kind + 1:kind + 2, :]
            r_row = gr[kind:kind + 1, :] - b_row
            dlog = jnp.where(mask, r_col + b_row, -jnp.inf)
            m_loc = jnp.max(dlog, axis=0, keepdims=True)
            na_s[dn, c] = _bdot(vat, (st * jnp.exp(dlog - m_loc)).astype(BF16))
            ml_s[dn, c] = m_loc
            b_tot[dn][c] = b_row[:, end:end + 1]
            gs_row = b_tot[dn][c] + r_row
            g_max[dn][c] = jnp.max(gs_row, axis=1, keepdims=True)
            dc_s[dn, c] = _bdot((vatf * jnp.exp(gs_row - g_max[dn][c])).astype(BF16), k)

    def advance(dn, c, cs, m_prev):
        kind = scans[dn][0]
        m_loc = ml_s[dn, c]
        inter = grow_ref[c][kind + 1:kind + 2, :] + m_prev
        m_t = jnp.maximum(inter, m_loc)
        e_in = jnp.exp(m_loc - m_t)
        e_st = jnp.exp(inter - m_t)
        n_in = na_s[dn, c]
        n_st = _bdot(cs.astype(BF16), qt_s[c])
        den = e_in * n_in[d:d + 1, :] + e_st * n_st[d:d + 1, :]
        inv = 1.0 / jnp.maximum(jnp.abs(den), jnp.exp(-m_t))
        ht = (e_in * inv) * n_in[:d, :] + (e_st * inv) * n_st[:d, :]
        m_new = jnp.maximum(b_tot[dn][c] + m_prev, g_max[dn][c])
        cs = jnp.exp(b_tot[dn][c] + m_prev - m_new) * cs + jnp.exp(g_max[dn][c] - m_new) * dc_s[dn, c]
        return ht, cs, m_new

    c_f = c_b = jnp.zeros((2 * d, d), F32)
    m_f = m_b = jnp.zeros((1, 1), F32)
    for i in range(nc):
        cf, cb = (i + ncl) % nc, nc - 1 - i
        hf_s[cf], c_f, m_f = advance(0, cf, c_f, m_f)
        hb_s[cb], c_b, m_b = advance(1, cb, c_b, m_b)

    for c in range(nc):
        h = (hf_s[c] + hb_s[c]).T
        og = og_ref[c * lc:(c + 1) * lc, :].astype(F32)
        o_ref[c * lc:(c + 1) * lc, :] = ((_rms(h) * ng_ref[...]) * _sigmoid(og)).astype(o_ref.dtype)


def _mlstm_mix(q, k, v, og, gcol, grow, cwq, cwk, ng, S):
    B, T, W = q.shape
    H = W // ML_HEAD_DIM
    d = ML_HEAD_DIM
    nc = T // ML_BLOCK
    head = pl.BlockSpec((None, T, d), lambda b, h: (b, 0, h))
    return pl.pallas_call(
        functools.partial(_mlstm_kernel, S, T),
        grid=(B, H),
        in_specs=[head, head, head, head,
                  pl.BlockSpec((None, None, ML_BLOCK, nc * 4), lambda b, h: (b, h, 0, 0)),
                  pl.BlockSpec((None, None, nc, 4, ML_BLOCK), lambda b, h: (b, h, 0, 0, 0)),
                  pl.BlockSpec((SHORT_W, d), lambda b, h: (0, h)),
                  pl.BlockSpec((SHORT_W, d), lambda b, h: (0, h)),
                  pl.BlockSpec((1, d), lambda b, h: (0, h))],
        out_specs=head,
        out_shape=jax.ShapeDtypeStruct((B, T, W), BF16),
        scratch_shapes=[pltpu.VMEM((T, d), BF16), pltpu.VMEM((nc, d, ML_BLOCK), BF16),
                        pltpu.VMEM((nc, 2 * d, ML_BLOCK), BF16),
                        pltpu.VMEM((nc, d, ML_BLOCK), F32), pltpu.VMEM((nc, d, ML_BLOCK), F32),
                        pltpu.VMEM((2, nc, 2 * d, ML_BLOCK), F32), pltpu.VMEM((2, nc, 2 * d, d), F32),
                        pltpu.VMEM((2, nc, 1, ML_BLOCK), F32)],
        compiler_params=_params(("parallel", "parallel"), 48),
        name="mlstm_mix",
    )(q, k, v, og, gcol, grow, cwq, cwk, ng)


FETCH = 16
HALO = 8
FFN_SPLIT = 1
TAIL_TM_LATENT = 512


def _halo_specs(a, n_lat_tiles, tm):
    fb = tm // FETCH
    if isinstance(a, tuple):
        lat, cx = a
        assert cx.shape[1] == tm
        w, nfb = lat.shape[2], lat.shape[1] // FETCH
        li = lambda i: jnp.minimum(i, n_lat_tiles - 1)
        return ([pl.BlockSpec((None, tm, w), lambda b, i: (b, li(i), 0)),
                 pl.BlockSpec((None, FETCH, w), lambda b, i: (b, jnp.maximum(li(i) * fb - 1, 0), 0)),
                 pl.BlockSpec((None, FETCH, w), lambda b, i: (b, jnp.minimum((li(i) + 1) * fb, nfb - 1), 0)),
                 pl.BlockSpec((None, tm, w), lambda b, i: (b, 0, 0))], [lat, lat, lat, cx])
    w, nfb = a.shape[2], a.shape[1] // FETCH
    return ([pl.BlockSpec((None, tm, w), lambda b, i: (b, i, 0)),
             pl.BlockSpec((None, FETCH, w), lambda b, i: (b, jnp.maximum(i * fb - 1, 0), 0)),
             pl.BlockSpec((None, FETCH, w), lambda b, i: (b, jnp.minimum((i + 1) * fb, nfb - 1), 0))], [a, a, a])


def _load_halo(refs, n_lat_tiles):
    main = refs[0][...]
    if len(refs) == 4:
        main = jnp.where(pl.program_id(1) >= n_lat_tiles, refs[3][...], main)
    return jnp.concatenate([refs[1][...], main, refs[2][...]], axis=0)


def _tail_kernel(final, n_lat_tiles, counts, mod_ref, *refs):
    groups = []
    for n in counts:
        groups.append(refs[:n])
        refs = refs[n:]
    n_act = len(counts) - 1
    wo_refs = refs[:n_act]
    g_ref, wup_ref, cw_ref, cb_ref, wd_ref, fn_ref, o_ref = refs[n_act:]
    i = pl.program_id(1)
    nt = pl.num_programs(1)
    tm = o_ref.shape[0]
    mod = mod_ref[...]
    y = None
    for a_refs, w_ref in zip(groups[1:], wo_refs):
        t = _bdot(_load_halo(a_refs, n_lat_tiles).astype(BF16), w_ref[...])
        y = t if y is None else y + t
    xe = _load_halo(groups[0], n_lat_tiles).astype(F32) + mod[2:3, :] * y
    x = xe[FETCH:FETCH + tm]
    hf = _modnorm(xe[FETCH - HALO:FETCH + tm + HALO], mod, g_ref[...], 3)
    he = hf.astype(BF16)
    h = hf[HALO:HALO + tm].astype(BF16)
    has_prev = jnp.logical_and(i != 0, i != n_lat_tiles)
    has_next = jnp.logical_and(i != n_lat_tiles - 1, i != nt - 1)
    row = lax.broadcasted_iota(jnp.int32, (tm + 2 * HALO, 1), 0)
    inside = jnp.logical_and(jnp.logical_or(row >= HALO, has_prev), jnp.logical_or(row < tm + HALO, has_next))
    fh = wd_ref.shape[0]
    fc = fh // FFN_SPLIT
    y = None
    for c in range(FFN_SPLIT):
        cols = slice(c * fc, (c + 1) * fc)
        ge = jnp.where(inside, _bdot(he, wup_ref[:, cols]), 0.0)
        gm = pltpu.roll(ge, 1, 0)[HALO:HALO + tm]
        gq = pltpu.roll(ge, tm + 2 * HALO - 1, 0)[HALO:HALO + tm]
        cw = cw_ref[:, cols]
        gc = cw[0:1, :] * gm + cw[1:2, :] * ge[HALO:HALO + tm] + cw[2:3, :] * gq + cb_ref[:, cols]
        a = (_silu(gc) * _bdot(h, wup_ref[:, fh + c * fc:fh + (c + 1) * fc])).astype(BF16)
        t = _bdot(a, wd_ref[cols, :])
        y = t if y is None else y + t
    o = x + mod[5:6, :] * y
    if final:
        o = _rms(o) * fn_ref[...]
    o_ref[...] = o


def _mixer_tail(x, acts, ws_out, mod, norm_g, w_up, cw, cb, w_down, fn, n_rows, n_lat_tiles, final, tm):
    B, _, _, D = mod.shape
    Fh = w_up.shape[1] // 2
    nt = n_rows // tm
    once = lambda shape: pl.BlockSpec(shape, lambda b, i: (0, 0), pipeline_mode=pl.Buffered(1))
    in_specs = [pl.BlockSpec((None, None, 6, D), lambda b, i: (b, (i >= n_lat_tiles).astype(jnp.int32), 0, 0))]
    args = [mod]
    counts = []
    for a in [x] + list(acts):
        sp, ar = _halo_specs(a, n_lat_tiles, tm)
        in_specs += sp
        args += ar
        counts.append(len(ar))
    in_specs += [once(w.shape) for w in ws_out]
    in_specs += [once((1, D)), once((D, 2 * Fh)), once((SHORT_W, Fh)), once((1, Fh)), once((Fh, D)), once((1, D))]
    return pl.pallas_call(
        functools.partial(_tail_kernel, final, n_lat_tiles, tuple(counts)),
        grid=(B, nt), in_specs=in_specs,
        out_specs=pl.BlockSpec((None, tm, D), lambda b, i: (b, i, 0)),
        out_shape=jax.ShapeDtypeStruct((B, n_rows, D), F32),
        compiler_params=_params(("parallel", "arbitrary"), 56),
        name="mixer_tail",
    )(*args, *ws_out, norm_g.reshape(1, D), w_up.astype(BF16), cw.astype(F32), cb.reshape(1, Fh).astype(F32),
      w_down.astype(BF16), fn.reshape(1, D))


def _rope_tables(S, T):
    p = np.arange(S)
    inv = ROPE_BASE ** (-np.arange(0, ROPE_AXIS, 2, dtype=np.float64) / ROPE_AXIS)
    ar = (p // GRID_W)[:, None] * inv[None, :]
    ac = (p % GRID_W)[:, None] * inv[None, :]
    cos = np.concatenate([np.cos(ar), np.cos(ar), np.cos(ac), np.cos(ac)], axis=1)
    sin = np.concatenate([-np.sin(ar), np.sin(ar), -np.sin(ac), np.sin(ac)], axis=1)
    cos = np.concatenate([cos, np.ones((T - S, MLA_ROPE))], axis=0)
    sin = np.concatenate([sin, np.zeros((T - S, MLA_ROPE))], axis=0)
    return cos.astype(np.float32), sin.astype(np.float32)


def _mla_proj_kernel(n_x, n_lat_tiles, *refs):
    x_refs = refs[:n_x]
    mod_ref, g_ref, wdn_ref, qn_ref, kvn_ref, wuq_ref, wukv_ref, cos_ref, sin_ref, qo_ref, ko_ref, vo_ref = refs[n_x:]
    H, dn, dr = MLA_HEADS, MLA_NOPE, MLA_ROPE
    h = _modnorm(_load_rows(x_refs, n_lat_tiles), mod_ref[...], g_ref[...], 0).astype(BF16)
    dnp = _bdot(h, wdn_ref[...])
    qa = dnp[:, :MLA_Q_RANK]
    kva = dnp[:, MLA_Q_RANK:MLA_Q_RANK + MLA_KV_RANK]
    kr2 = dnp[:, MLA_Q_RANK + MLA_KV_RANK:]
    q = _bdot((_rms(qa) * qn_ref[...]).astype(BF16), wuq_ref[...])
    kv = _bdot((_rms(kva) * kvn_ref[...]).astype(BF16), wukv_ref[...])
    cosr = cos_ref[...]
    sinr = sin_ref[...]
    kr = kr2[:, :dr] * cosr + kr2[:, dr:] * sinr
    zpad = jnp.zeros((kr.shape[0], MLA_DK_PAD - dn - dr), F32)
    ro = H * dn
    for hd in range(H):
        qr = q[:, ro + hd * dr:ro + (hd + 1) * dr] * cosr + q[:, ro + H * dr + hd * dr:ro + H * dr + (hd + 1) * dr] * sinr
        qh = jnp.concatenate([q[:, hd * dn:(hd + 1) * dn], qr, zpad], axis=1) * (MLA_SCALE * math.log2(math.e))
        qo_ref[hd] = qh.astype(BF16)
        ko_ref[hd] = jnp.concatenate([kv[:, hd * dn:(hd + 1) * dn], kr, zpad], axis=1).astype(BF16)
        vo_ref[hd] = kv[:, ro + hd * MLA_V:ro + (hd + 1) * MLA_V].astype(BF16)


def _mla_proj(x, mod, norm_g, w_down, q_norm, kv_norm, w_uq, w_ukv, S, T):
    B, _, _, D = mod.shape
    H, dn, dr, dv = MLA_HEADS, MLA_NOPE, MLA_ROPE, MLA_V
    dk = dn + dr
    nt = T // TM
    n_lat_tiles = S // TM
    swap = np.arange(dr) ^ (ROPE_AXIS // 2)
    wkr = w_down[:, MLA_Q_RANK + MLA_KV_RANK:]
    wdn = jnp.concatenate([w_down, wkr[:, swap]], axis=1).astype(BF16)
    wq = w_uq.reshape(MLA_Q_RANK, H, dk)
    wq_n = wq[:, :, :dn].reshape(MLA_Q_RANK, H * dn)
    wq_r = wq[:, :, dn:]
    wuq = jnp.concatenate([wq_n, wq_r.reshape(MLA_Q_RANK, H * dr), wq_r[:, :, swap].reshape(MLA_Q_RANK, H * dr)],
                          axis=1).astype(BF16)
    wkv = w_ukv.reshape(MLA_KV_RANK, H, dn + dv)
    wukv = jnp.concatenate([wkv[:, :, :dn].reshape(MLA_KV_RANK, H * dn), wkv[:, :, dn:].reshape(MLA_KV_RANK, H * dv)],
                           axis=1).astype(BF16)
    cos, sin = _rope_tables(S, T)
    full = lambda a: pl.BlockSpec(a.shape, lambda b, i: (0,) * a.ndim)
    in_specs, args = _row_specs(x, n_lat_tiles)
    n_x = len(args)
    in_specs += [pl.BlockSpec((None, None, 6, D), lambda b, i: (b, (i >= n_lat_tiles).astype(jnp.int32), 0, 0)),
                 pl.BlockSpec((1, D), lambda b, i: (0, 0)),
                 full(wdn), pl.BlockSpec((1, MLA_Q_RANK), lambda b, i: (0, 0)),
                 pl.BlockSpec((1, MLA_KV_RANK), lambda b, i: (0, 0)), full(wuq), full(wukv),
                 pl.BlockSpec((TM, dr), lambda b, i: (i, 0)), pl.BlockSpec((TM, dr), lambda b, i: (i, 0))]
    headed = lambda w: pl.BlockSpec((None, H, TM, w), lambda b, i: (b, 0, i, 0))
    return pl.pallas_call(
        functools.partial(_mla_proj_kernel, n_x, n_lat_tiles),
        grid=(B, nt), in_specs=in_specs,
        out_specs=[headed(MLA_DK_PAD), headed(MLA_DK_PAD), headed(dv)],
        out_shape=[jax.ShapeDtypeStruct((B, H, T, MLA_DK_PAD), BF16), jax.ShapeDtypeStruct((B, H, T, MLA_DK_PAD), BF16),
                   jax.ShapeDtypeStruct((B, H, T, dv), BF16)],
        compiler_params=_params(("parallel", "arbitrary"), 40),
        name="mla_proj",
    )(*args, mod, norm_g.reshape(1, D), wdn, q_norm.reshape(1, -1), kv_norm.reshape(1, -1), wuq, wukv,
      jnp.asarray(cos), jnp.asarray(sin))


ATT_HEADS_PER_STEP = 2
ATT_QUERY_TILE = 1024
ATT_KEY_SLAB = 768


def _attn_kernel(q_ref, k_ref, v_ref, o_ref, va_s):
    hp, T, dv = v_ref.shape

    @pl.when(pl.program_id(2) == 0)
    def _():
        ones_col = (lax.broadcasted_iota(jnp.int32, (T, dv), 1) == 0).astype(BF16)
        for j in range(hp):
            va_s[j] = jnp.concatenate([v_ref[j], ones_col], axis=1)

    kw = max(w for w in range(LANES, ATT_KEY_SLAB + 1, LANES) if T % w == 0)
    slabs = [slice(c * kw, (c + 1) * kw) for c in range(T // kw)]
    nt = (((1,), (1,)), ((), ()))
    s = [[lax.dot_general(q_ref[j], k_ref[j, sl, :], nt, preferred_element_type=F32) for sl in slabs]
         for j in range(hp)]
    for j in range(hp):
        m = functools.reduce(jnp.maximum, [jnp.max(sc, axis=-1, keepdims=True) for sc in s[j]])
        na = None
        for sc, sl in zip(s[j], slabs):
            t = _bdot(jnp.exp2(sc - m).astype(BF16), va_s[j, sl, :])
            na = t if na is None else na + t
        o_ref[:, j * dv:(j + 1) * dv] = (na[:, :dv] * (1.0 / na[:, dv:dv + 1])).astype(o_ref.dtype)


def _attention(q, k, v, S):
    B, H, T, dk = k.shape
    dv = v.shape[3]
    tq = min(S, ATT_QUERY_TILE)
    hp = ATT_HEADS_PER_STEP
    return pl.pallas_call(
        _attn_kernel,
        grid=(B, H // hp, S // tq),
        in_specs=[pl.BlockSpec((None, hp, tq, dk), lambda b, h, i: (b, h, i, 0)),
                  pl.BlockSpec((None, hp, T, dk), lambda b, h, i: (b, h, 0, 0)),
                  pl.BlockSpec((None, hp, T, dv), lambda b, h, i: (b, h, 0, 0))],
        out_specs=pl.BlockSpec((None, tq, hp * dv), lambda b, h, i: (b, i, h)),
        out_shape=jax.ShapeDtypeStruct((B, S, H * dv), BF16),
        scratch_shapes=[pltpu.VMEM((hp, T, 2 * dv), BF16)],
        compiler_params=_params(("parallel", "parallel", "arbitrary"), 48),
        name="mla_attention",
    )(q, k, v)


def kernel(x, c, ctx, c_ctx, ada_w_0, ada_b_0, norm_mix_0, norm_ffn_0, w_in_0, hy_conv_w, hy_conv_b, hy_fw1, hy_fb1, hy_fw2, hy_fb2, hy_fw3, hy_freq, hy_bias, ml_conv_w, ml_gate_b, ml_norm_g, w_out_0, ffn_up_0, ffn_conv_w_0, ffn_conv_b_0, ffn_down_0, ada_w_1, ada_b_1, norm_mix_1, norm_ffn_1, mla_w_down, mla_q_norm, mla_kv_norm, mla_w_uq, mla_w_ukv, mla_w_o, ffn_up_1, ffn_conv_w_1, ffn_conv_b_1, ffn_down_1, final_norm):
    B, S, D = x.shape
    CL = ctx.shape[1]
    T = S + CL
    assert S % TM == 0 and CL % TM == 0 and S % CL == 0 and S % GRID_W == 0 and TM == ML_BLOCK
    nlt = S // TM
    C = HY_WIDTH
    W = ML_WIDTH
    H = ML_HEADS

    mod0 = _modulation(c, c_ctx, ada_w_0, ada_b_0)
    mod1 = _modulation(c, c_ctx, ada_w_1, ada_b_1)

    wi = jnp.pad(w_in_0, ((0, 0), (0, LANES - 4 * H))).astype(BF16)
    b_gate = jnp.zeros((1, LANES), F32).at[0, :4 * H].set(ml_gate_b.astype(F32))
    hx0, hx1, hv, mq, mk, mv, mog, gates = _in_proj((x, ctx), mod0, norm_mix_0, wi, b_gate,
                                                    [C] * 3 + [W] * 4 + [LANES], T, nlt)

    filt = (hy_fw1, hy_fb1, hy_fw2, hy_fb2, hy_fw3, hy_freq, hy_bias)
    hcw = jnp.transpose(hy_conv_w.reshape(SHORT_W, 3, C), (1, 0, 2)).astype(F32)
    hcb = hy_conv_b.reshape(3, C).astype(F32)
    hy = []
    for L, blk in ((S, 0), (CL, S // CL)):
        R = _hyena_radix(L)
        wf_np, wft_np = _dft_tables(L // R)
        wf = jnp.asarray(wf_np).astype(BF16)
        wft = jnp.asarray(wft_np).astype(BF16)
        spectra = _hyena_filter_spectrum(L, R, wf, _hyena_filter_lags(L, *filt))
        hy.append(_hyena_mix(hx0, hx1, hv, hcw, hcb, wf, wft, spectra, L, R, blk))

    nc = T // ML_BLOCK
    g4 = gates[:, :, :4 * H].reshape(B, nc, ML_BLOCK, 4, H)
    gcol = jnp.transpose(g4, (0, 4, 2, 1, 3)).reshape(B, H, ML_BLOCK, nc * 4)
    grow = jnp.transpose(g4, (0, 4, 1, 3, 2))
    mcw = ml_conv_w.astype(F32)
    ml = _mlstm_mix(mq, mk, mv, mog, gcol, grow, mcw[:, :W], mcw[:, W:], ml_norm_g.reshape(1, W).astype(F32), S)

    tml = TAIL_TM_LATENT if S % TAIL_TM_LATENT == 0 else TM
    wo = [w_out_0[:C].astype(BF16), w_out_0[C:].astype(BF16)]
    ffn0 = (mod0, norm_ffn_0, ffn_up_0, ffn_conv_w_0, ffn_conv_b_0, ffn_down_0, final_norm)
    xs_lat = _mixer_tail(x, [hy[0], ml], wo, *ffn0, S, S // tml, False, tml)
    xs_ctx = _mixer_tail(ctx, [hy[1], ml[:, S:]], wo, *ffn0, CL, 0, False, TM)

    q, k, v = _mla_proj((xs_lat, xs_ctx), mod1, norm_mix_1, mla_w_down, mla_q_norm, mla_kv_norm, mla_w_uq,
                        mla_w_ukv, S, T)
    att = _attention(q, k, v, S)
    return _mixer_tail(xs_lat, [att], [mla_w_o.astype(BF16)], mod1, norm_ffn_1, ffn_up_1, ffn_conv_w_1,
                       ffn_conv_b_1, ffn_down_1, final_norm, S, S // tml, True, tml)
```
